```python
import jax
import jax.numpy as jnp
from jax import lax
import numpy as np

D_MODEL = 1024
BATCH = 16
SEQ = 256
DEPTH = 4
DEC_BATCH = 2
DEC_SEQ = 2048
PAST_LEN = 512

GRID_W = 64
HEAD_DIM = 64
A_HEADS = 8
A_KV = 2
A_GROUP = A_HEADS // A_KV
WINDOW = 128
Q_BLOCK = 128
B_HEADS = 8
B_WIDTH = B_HEADS * HEAD_DIM
LORA_W = 64
LORA_A = 64
LORA_G = 128
GN_EPS = 64e-5
C_HEADS = 8
NA_ROWS = 8
NA_COLS = 16
N_BRANCH = 3
A_COLS = (A_HEADS + 2 * A_KV) * HEAD_DIM
B_COLS = 3 * B_WIDTH + 2 * LORA_W + 2 * LORA_A + LORA_G
C_COLS = 3 * C_HEADS * HEAD_DIM
IN_COLS = A_COLS + B_COLS + C_COLS + N_BRANCH * D_MODEL
PEER_HEADS = 8
N_KEYS = 128
N_EXPERTS = N_KEYS * N_KEYS
PEER_KEY_DIM = 256
PEER_TOPK = 16
PEER_CHUNK = 128
ROPE_BASE = 10000.0
RMS_EPS = 1e-6
N_MOD = 6

kernel_name = "hybrid_diffusion_prefix_trunk_step"


def _rmsnorm(x, g):
    xf = x.astype(jnp.float32)
    y = xf * lax.rsqrt(jnp.mean(xf * xf, axis=-1, keepdims=True) + RMS_EPS)
    return (y * g.astype(jnp.float32)).astype(x.dtype)


def _modulate(x, shift, scale):
    return x * (1 + scale) + shift


def _axial_rope(x):
    T = x.shape[1]
    quarter = x.shape[-1] // 4
    t = jnp.arange(T)
    pos = jnp.stack([t // GRID_W, t % GRID_W], axis=-1).astype(jnp.float32)
    freqs = ROPE_BASE ** (-jnp.arange(quarter, dtype=jnp.float32) / quarter)
    ang = pos[:, :, None] * freqs
    bshape = (T,) + (1,) * (x.ndim - 3) + (2, quarter)
    cos = jnp.cos(ang).reshape(bshape).astype(x.dtype)
    sin = jnp.sin(ang).reshape(bshape).astype(x.dtype)
    xr = x.reshape(x.shape[:-1] + (2, 2, quarter))
    x1 = xr[..., 0, :]
    x2 = xr[..., 1, :]
    out = jnp.stack([x1 * cos - x2 * sin, x2 * cos + x1 * sin], axis=-2)
    return out.reshape(x.shape)


def _dense_attention(q, k, v, sink):
    B, S, KV, G, D = q.shape
    nb = S // Q_BLOCK
    scale = D ** -0.5
    vf = v.astype(jnp.float32)
    qb = jnp.moveaxis(q.reshape(B, nb, Q_BLOCK, KV, G, D), 1, 0)

    def block(qi):
        s = jnp.einsum("bqkgd,bskd->bkgqs", qi, k).astype(jnp.float32) * scale
        m = jnp.max(s, axis=-1)
        if sink is not None:
            sk = sink.astype(jnp.float32)[None, :, :, None]
            m = jnp.maximum(m, sk)
        p = jnp.exp(s - m[..., None])
        den = jnp.sum(p, axis=-1)
        if sink is not None:
            den = den + jnp.exp(sk - m)
        o = jnp.einsum("bkgqs,bskd->bqkgd", p, vf)
        return o / jnp.moveaxis(den, -1, 1)[..., None]

    o = lax.map(block, qb)
    return jnp.moveaxis(o, 0, 1).reshape(B, S, KV * G * D).astype(q.dtype)


def _window_attention(q, k, v, ck, cv, sink):
    B, T, KV, G, D = q.shape
    nb = T // Q_BLOCK
    span = Q_BLOCK + 2 * WINDOW
    scale = D ** -0.5
    pad = ((0, 0), (WINDOW, WINDOW), (0, 0), (0, 0))
    kp = jnp.pad(k, pad)
    vp = jnp.pad(v, pad)
    idx = jnp.arange(nb)[:, None] * Q_BLOCK + jnp.arange(span)[None, :]
    kb = kp[:, idx]
    vb = vp[:, idx].astype(jnp.float32)
    qb = q.reshape(B, nb, Q_BLOCK, KV, G, D)
    s = jnp.einsum("bnqkgd,bnskd->bnkgqs", qb, kb).astype(jnp.float32) * scale
    qpos = jnp.arange(nb)[:, None] * Q_BLOCK + jnp.arange(Q_BLOCK)[None, :]
    kpos = idx - WINDOW
    valid = ((jnp.abs(qpos[:, :, None] - kpos[:, None, :]) <= WINDOW)
             & (kpos >= 0)[:, None, :] & (kpos < T)[:, None, :])
    s = jnp.where(valid[None, :, None, None], s, -jnp.inf)
    sc = jnp.einsum("bnqkgd,bpkd->bnkgqp", qb, ck).astype(jnp.float32) * scale
    sk = sink.astype(jnp.float32)[None, None, :, :, None]
    m = jnp.maximum(jnp.maximum(jnp.max(s, axis=-1), jnp.max(sc, axis=-1)), sk)
    ps = jnp.exp(s - m[..., None])
    pc = jnp.exp(sc - m[..., None])
    den = jnp.sum(ps, axis=-1) + jnp.sum(pc, axis=-1) + jnp.exp(sk - m)
    o = (jnp.einsum("bnkgqs,bnskd->bnqkgd", ps, vb)
         + jnp.einsum("bnkgqp,bpkd->bnqkgd", pc, cv.astype(jnp.float32)))
    o = o / jnp.moveaxis(den, -1, 2)[..., None]
    return o.reshape(B, T, KV * G * D).astype(q.dtype)


def _neighbourhood_attention(q, k, v, ck, cv, rpb):
    B, T, H, D = q.shape
    rows = T // GRID_W
    wr = min(NA_ROWS, rows)
    scale = D ** -0.5
    qg = q.reshape(B, rows, GRID_W, H, D)
    kg = k.reshape(B, rows, GRID_W, H, D)
    vg = v.reshape(B, rows, GRID_W, H, D)
    r = jnp.arange(rows)
    rs = jnp.clip(r - wr // 2, 0, rows - wr)
    row_idx = rs[:, None] + jnp.arange(wr)[None, :]
    k_rows = kg[:, row_idx]
    v_rows = vg[:, row_idx].astype(jnp.float32)
    col = jnp.arange(GRID_W)
    cs = jnp.clip(col - NA_COLS // 2, 0, GRID_W - NA_COLS)
    col_ok = (col[None, :] >= cs[:, None]) & (col[None, :] < cs[:, None] + NA_COLS)
    dr = row_idx - r[:, None] + NA_ROWS - 1
    dc = jnp.clip(col[None, :] - col[:, None], -(NA_COLS - 1), NA_COLS - 1) + NA_COLS - 1
    bias = rpb[:, dr[:, None, :, None], dc[None, :, None, :]].astype(jnp.float32)
    s = jnp.einsum("brchd,brawhd->bhrcaw", qg, k_rows).astype(jnp.float32) * scale + bias[None]
    s = jnp.where(col_ok[None, None, None, :, None, :], s, -jnp.inf)
    sc = jnp.einsum("brchd,bphd->bhrcp", qg, ck).astype(jnp.float32) * scale
    m = jnp.maximum(jnp.max(s, axis=(-2, -1)), jnp.max(sc, axis=-1))
    pn = jnp.exp(s - m[..., None, None])
    pc = jnp.exp(sc - m[..., None])
    den = jnp.sum(pn, axis=(-2, -1)) + jnp.sum(pc, axis=-1)
    o = (jnp.einsum("bhrcaw,brawhd->brchd", pn, v_rows)
         + jnp.einsum("bhrcp,bphd->brchd", pc, cv.astype(jnp.float32)))
    o = o / jnp.transpose(den, (0, 2, 3, 1))[..., None]
    return o.reshape(B, T, H * D).astype(q.dtype)


def _heads(z):
    return z.reshape(z.shape[:2] + (B_HEADS, HEAD_DIM))


def _rwkv7_scan(s0, r, w, k, v, kk, a, reverse):
    xs = tuple(jnp.moveaxis(z, 1, 0) for z in (r, w, k, v, kk, a))

    def step(S, xt):
        rt, wt, kt, vt, kkt, at = xt
        sa = jnp.einsum("bhij,bhj->bhi", S, -kkt)
        S = (S * wt[:, :, None, :] + sa[..., None] * (kkt * at)[:, :, None, :]
             + vt[..., None] * kt[:, :, None, :])
        return S, jnp.einsum("bhij,bhj->bhi", S, rt)

    S, ys = lax.scan(step, s0.astype(jnp.float32), xs, reverse=reverse)
    return jnp.moveaxis(ys, 0, 1), S


def _rwkv7_bidir(pb, lp, s0_f, s0_b):
    B, T, _ = pb.shape
    prev = jnp.pad(pb[:, :-1], ((0, 0), (1, 0), (0, 0)))
    nxt = jnp.pad(pb[:, 1:], ((0, 0), (0, 1), (0, 0)))
    xb = pb + lp["rw_mu"] * (0.5 * (prev + nxt) - pb)
    r, k, v, wd, ad, gd = jnp.split(
        xb, [B_WIDTH, 2 * B_WIDTH, 3 * B_WIDTH, 3 * B_WIDTH + 2 * LORA_W,
             3 * B_WIDTH + 2 * LORA_W + 2 * LORA_A], axis=-1)
    wd = wd.reshape(B, T, 2, LORA_W)
    ad = ad.reshape(B, T, 2, LORA_A)
    wlog = -jax.nn.softplus(-(lp["rw_w0"] + jnp.einsum("btzl,zlc->btzc", jnp.tanh(wd), lp["rw_w2"]))) - 0.5
    decay = jnp.exp(-jnp.exp(wlog.astype(jnp.float32)))
    a = jax.nn.sigmoid(lp["rw_a0"] + jnp.einsum("btzl,zlc->btzc", ad, lp["rw_a2"])).astype(jnp.float32)
    g = (jax.nn.sigmoid(gd) @ lp["rw_g2"]).astype(jnp.float32)
    rf = r.astype(jnp.float32)
    kf = k.astype(jnp.float32)
    vf = v.astype(jnp.float32)
    kk = _heads(kf * lp["rw_kk"])
    kk = kk * lax.rsqrt(jnp.sum(kk * kk, axis=-1, keepdims=True) + 1e-12)
    kd = kf[:, :, None, :] * (1 + (a - 1) * lp["rw_ka"])
    y_f, s_f = _rwkv7_scan(s0_f, _heads(rf), _heads(decay[:, :, 0]), _heads(kd[:, :, 0]),
                           _heads(vf), kk, _heads(a[:, :, 0]), False)
    y_b, s_b = _rwkv7_scan(s0_b, _heads(rf), _heads(decay[:, :, 1]), _heads(kd[:, :, 1]),
                           _heads(vf), kk, _heads(a[:, :, 1]), True)
    y = y_f + y_b
    mu = jnp.mean(y, axis=-1, keepdims=True)
    var = jnp.mean(jnp.square(y - mu), axis=-1, keepdims=True)
    y = ((y - mu) * lax.rsqrt(var + GN_EPS)).reshape(B, T, B_WIDTH) * lp["rw_lnx_g"] + lp["rw_lnx_b"]
    bonus = jnp.sum(_heads(rf * (kd[:, :, 0] + kd[:, :, 1])) * lp["rw_rk"], axis=-1, keepdims=True) * _heads(vf)
    y = (y + bonus.reshape(B, T, B_WIDTH)) * g
    return y.astype(pb.dtype), s_f, s_b


def _mixer(h, lp, ctx):
    B, T, _ = h.shape
    p = h @ lp["w_in"]
    pa, pb, pc, pg = jnp.split(p, [A_COLS, A_COLS + B_COLS, A_COLS + B_COLS + C_COLS], axis=-1)
    qa, ka, va = jnp.split(pa, [A_HEADS * HEAD_DIM, (A_HEADS + A_KV) * HEAD_DIM], axis=-1)
    qa = qa.reshape(B, T, A_KV, A_GROUP, HEAD_DIM)
    ka = ka.reshape(B, T, A_KV, HEAD_DIM)
    va = va.reshape(B, T, A_KV, HEAD_DIM)
    qc, kc, vc = jnp.split(pc, 3, axis=-1)
    qc = qc.reshape(B, T, C_HEADS, HEAD_DIM)
    kc = kc.reshape(B, T, C_HEADS, HEAD_DIM)
    vc = vc.reshape(B, T, C_HEADS, HEAD_DIM)
    sink = lp["a_sink"].reshape(A_KV, A_GROUP)
    if ctx is None:
        ya = _dense_attention(qa, ka, va, sink)
        yc = _dense_attention(qc[:, :, :, None, :], kc, vc, None)
        s0_f = jnp.zeros((B, B_HEADS, HEAD_DIM, HEAD_DIM), jnp.float32)
        s0_b = s0_f
    else:
        ctx_ak, ctx_av, ctx_ck, ctx_cv, s0 = ctx
        ya = _window_attention(_axial_rope(qa), _axial_rope(ka), va, ctx_ak, ctx_av, sink)
        yc = _neighbourhood_attention(qc, kc, vc, ctx_ck, ctx_cv, lp["na_rpb"])
        s0_f = s0[:, 0]
        s0_b = s0[:, 1]
    yb, s_f, s_b = _rwkv7_bidir(pb, lp, s0_f, s0_b)
    gates = jax.nn.sigmoid(pg.reshape(B, T, N_BRANCH, D_MODEL))
    merged = (gates[:, :, 0] * (ya @ lp["a_out"]) + gates[:, :, 1] * (yb @ lp["rw_out"])
              + gates[:, :, 2] * (yc @ lp["na_out"]))
    out = merged @ lp["w_o"]
    if ctx is None:
        return out, (ka, va, kc, vc, jnp.stack([s_f, s_b], axis=1).astype(h.dtype))
    return out, None


def _peer(h, q_w, subkeys, u_tab, v_tab):
    B, T, D = h.shape
    hc_all = h.reshape((B * T) // PEER_CHUNK, PEER_CHUNK, D)

    def chunk(hc):
        q = (hc @ q_w).reshape(PEER_CHUNK, PEER_HEADS, 2, PEER_KEY_DIM // 2)
        s = jnp.einsum("chzk,hznk->chzn", q, subkeys).astype(jnp.float32)
        v1, i1 = lax.top_k(s[:, :, 0], PEER_TOPK)
        v2, i2 = lax.top_k(s[:, :, 1], PEER_TOPK)
        cand = (v1[..., :, None] + v2[..., None, :]).reshape(PEER_CHUNK, PEER_HEADS, PEER_TOPK * PEER_TOPK)
        sv, si = lax.top_k(cand, PEER_TOPK)
        e = (jnp.take_along_axis(i1, si // PEER_TOPK, axis=-1) * N_KEYS
             + jnp.take_along_axis(i2, si % PEER_TOPK, axis=-1))
        gate = jax.nn.softmax(sv, axis=-1)
        u = jnp.take(u_tab, e, axis=0)
        act = jax.nn.gelu(jnp.einsum("cd,chkd->chk", hc, u).astype(jnp.float32))
        vv = jnp.take(v_tab, e, axis=0)
        return jnp.einsum("chk,chkd->cd", (gate * act).astype(hc.dtype), vv)

    return lax.map(chunk, hc_all).reshape(B, T, D)


def setup_inputs(seed: int = 0) -> dict:
    key = jax.random.key(seed)
    ks = iter(jax.random.split(key, 48))

    def nrm(shape, s):
        return jax.random.normal(next(ks), shape, jnp.float32) * s

    L, D = DEPTH, D_MODEL
    return {
        "x_prompt": nrm((BATCH, SEQ, D), 1.0),
        "x_sample": nrm((DEC_BATCH, DEC_SEQ, D), 1.0),
        "cache_a_k": nrm((DEC_BATCH, L, PAST_LEN, A_KV, HEAD_DIM), 1.0),
        "cache_a_v": nrm((DEC_BATCH, L, PAST_LEN, A_KV, HEAD_DIM), 1.0),
        "cache_c_k": nrm((DEC_BATCH, L, PAST_LEN, C_HEADS, HEAD_DIM), 1.0),
        "cache_c_v": nrm((DEC_BATCH, L, PAST_LEN, C_HEADS, HEAD_DIM), 1.0),
        "state_rwkv": nrm((DEC_BATCH, L, 2, B_HEADS, HEAD_DIM, HEAD_DIM), 0.5),
        "c": nrm((DEC_BATCH, D), 1.0),
        "c_ctx": nrm((D,), 1.0),
        "ln1_g": 1.0 + nrm((L, D), 0.02),
        "ln2_g": 1.0 + nrm((L, D), 0.02),
        "lnf_g": 1.0 + nrm((D,), 0.02),
        "ada_w": nrm((L, D, N_MOD * D), 0.5 * D ** -0.5),
        "ada_b": nrm((L, N_MOD * D), 0.02),
        "w_in": nrm((L, D, IN_COLS), D ** -0.5),
        "a_sink": nrm((L, A_HEADS), 1.0),
        "a_out": nrm((L, A_HEADS * HEAD_DIM, D), (A_HEADS * HEAD_DIM) ** -0.5),
        "rw_mu": jax.random.uniform(next(ks), (L, B_COLS), jnp.float32),
        "rw_w0": nrm((L, 2, B_WIDTH), 0.5),
        "rw_w2": nrm((L, 2, LORA_W, B_WIDTH), 0.5 * LORA_W ** -0.5),
        "rw_a0": nrm((L, 2, B_WIDTH), 0.5),
        "rw_a2": nrm((L, 2, LORA_A, B_WIDTH), 0.5 * LORA_A ** -0.5),
        "rw_g2": nrm((L, LORA_G, B_WIDTH), LORA_G ** -0.5),
        "rw_kk": 1.0 + nrm((L, B_WIDTH), 0.1),
        "rw_ka": 1.0 + nrm((L, B_WIDTH), 0.1),
        "rw_rk": nrm((L, B_HEADS, HEAD_DIM), 0.1),
        "rw_lnx_g": 1.0 + nrm((L, B_WIDTH), 0.02),
        "rw_lnx_b": nrm((L, B_WIDTH), 0.02),
        "rw_out": nrm((L, B_WIDTH, D), B_WIDTH ** -0.5),
        "na_rpb": nrm((L, C_HEADS, 2 * NA_ROWS - 1, 2 * NA_COLS - 1), 0.5),
        "na_out": nrm((L, C_HEADS * HEAD_DIM, D), (C_HEADS * HEAD_DIM) ** -0.5),
        "w_o": nrm((L, D, D), D ** -0.5),
        "pe_q": nrm((L, D, PEER_HEADS * PEER_KEY_DIM), D ** -0.5),
        "pe_subkeys": nrm((L, PEER_HEADS, 2, N_KEYS, PEER_KEY_DIM // 2), (PEER_KEY_DIM // 2) ** -0.5),
        "pe_u": nrm((L, N_EXPERTS, D), D ** -0.5),
        "pe_v": nrm((L, N_EXPERTS, D), (PEER_HEADS * PEER_TOPK) ** -0.5),
    }


def reference(x_prompt, x_sample, cache_a_k, cache_a_v, cache_c_k, cache_c_v, state_rwkv,
              c, c_ctx, ln1_g, ln2_g, lnf_g, ada_w, ada_b, w_in, a_sink, a_out,
              rw_mu, rw_w0, rw_w2, rw_a0, rw_a2, rw_g2, rw_kk, rw_ka, rw_rk,
              rw_lnx_g, rw_lnx_b, rw_out, na_rpb, na_out, w_o,
              pe_q, pe_subkeys, pe_u, pe_v):
    xp = x_prompt
    xs = x_sample
    silu_ctx = jax.nn.silu(c_ctx)[None, :]
    silu_lat = jax.nn.silu(c)
    list_ak, list_av, list_ck, list_cv, list_st = [], [], [], [], []
    for l in range(DEPTH):
        lp = {
            "w_in": w_in[l], "a_sink": a_sink[l], "a_out": a_out[l],
            "rw_mu": rw_mu[l], "rw_w0": rw_w0[l], "rw_w2": rw_w2[l], "rw_a0": rw_a0[l],
            "rw_a2": rw_a2[l], "rw_g2": rw_g2[l], "rw_kk": rw_kk[l], "rw_ka": rw_ka[l],
            "rw_rk": rw_rk[l], "rw_lnx_g": rw_lnx_g[l], "rw_lnx_b": rw_lnx_b[l],
            "rw_out": rw_out[l], "na_rpb": na_rpb[l], "na_out": na_out[l], "w_o": w_o[l],
        }
        mod_p = (silu_ctx @ ada_w[l] + ada_b[l]).reshape(1, 1, N_MOD, D_MODEL)
        mod_s = (silu_lat @ ada_w[l] + ada_b[l]).reshape(-1, 1, N_MOD, D_MODEL)

        h = _modulate(_rmsnorm(xp, ln1_g[l]), mod_p[:, :, 0], mod_p[:, :, 1])
        o, (ak, av, ck, cv, st) = _mixer(h, lp, None)
        xp = xp + mod_p[:, :, 2] * o
        h = _modulate(_rmsnorm(xp, ln2_g[l]), mod_p[:, :, 3], mod_p[:, :, 4])
        xp = xp + mod_p[:, :, 5] * _peer(h, pe_q[l], pe_subkeys[l], pe_u[l], pe_v[l])
        list_ak.append(ak)
        list_av.append(av)
        list_ck.append(ck)
        list_cv.append(cv)
        list_st.append(st)

        ctx = (cache_a_k[:, l], cache_a_v[:, l], cache_c_k[:, l], cache_c_v[:, l], state_rwkv[:, l])
        h = _modulate(_rmsnorm(xs, ln1_g[l]), mod_s[:, :, 0], mod_s[:, :, 1])
        o, _ = _mixer(h, lp, ctx)
        xs = xs + mod_s[:, :, 2] * o
        h = _modulate(_rmsnorm(xs, ln2_g[l]), mod_s[:, :, 3], mod_s[:, :, 4])
        xs = xs + mod_s[:, :, 5] * _peer(h, pe_q[l], pe_subkeys[l], pe_u[l], pe_v[l])

    y_prompt = _rmsnorm(xp, lnf_g)
    y_sample = _rmsnorm(xs, lnf_g)
    new_a_k = jnp.stack(list_ak, axis=1)
    new_a_v = jnp.stack(list_av, axis=1)
    new_c_k = jnp.stack(list_ck, axis=1)
    new_c_v = jnp.stack(list_cv, axis=1)
    new_state_rwkv = jnp.stack(list_st, axis=1)
    return (y_prompt, y_sample, new_a_k, new_a_v, new_c_k, new_c_v, new_state_rwkv)
```

```python
import functools

import jax
import jax.numpy as jnp
from jax import lax
from jax.experimental import pallas as pl
from jax.experimental.pallas import tpu as pltpu

D_MODEL = 1024
BATCH = 16
SEQ = 256
DEPTH = 4
DEC_BATCH = 2
DEC_SEQ = 2048
PAST_LEN = 512
GRID_W = 64
HEAD_DIM = 64
A_HEADS = 8
A_KV = 2
A_GROUP = A_HEADS // A_KV
WINDOW = 128
Q_BLOCK = 128
B_HEADS = 8
B_WIDTH = B_HEADS * HEAD_DIM
LORA_W = 64
LORA_A = 64
LORA_G = 128
GN_EPS = 64e-5
C_HEADS = 8
NA_ROWS = 8
NA_COLS = 16
N_BRANCH = 3
A_COLS = (A_HEADS + 2 * A_KV) * HEAD_DIM
B_COLS = 3 * B_WIDTH + 2 * LORA_W + 2 * LORA_A + LORA_G
C_COLS = 3 * C_HEADS * HEAD_DIM
IN_COLS = A_COLS + B_COLS + C_COLS + N_BRANCH * D_MODEL
PEER_HEADS = 8
N_KEYS = 128
N_EXPERTS = N_KEYS * N_KEYS
PEER_KEY_DIM = 256
PEER_TOPK = 16
PEER_CHUNK = 128
ROPE_BASE = 10000.0
RMS_EPS = 1e-6
N_MOD = 6

N_CTX_TOK = BATCH * SEQ
N_LAT_TOK = DEC_BATCH * DEC_SEQ
N_TOK = N_CTX_TOK + N_LAT_TOK
N_SEG = 1 + DEC_BATCH

VMEM_LIMIT_BYTES = 48 * 1024 * 1024

ROW_TILE = 512
assert N_CTX_TOK % ROW_TILE == 0 and DEC_SEQ % ROW_TILE == 0


def _segment_of_row_block(i):
    n_ctx_blocks = N_CTX_TOK // ROW_TILE
    blocks_per_lat = DEC_SEQ // ROW_TILE
    return jnp.where(i < n_ctx_blocks, 0, 1 + (i - n_ctx_blocks) // blocks_per_lat)


def _norm_proj_kernel(x_ref, g_ref, mod_ref, w_ref, o_ref, h_scr, *, shift_idx, scale_idx):
    @pl.when(pl.program_id(1) == 0)
    def _():
        x = x_ref[...]
        y = x * lax.rsqrt(jnp.mean(x * x, axis=-1, keepdims=True) + RMS_EPS) * g_ref[...]
        h = y * (1.0 + mod_ref[0, scale_idx:scale_idx + 1, :]) + mod_ref[0, shift_idx:shift_idx + 1, :]
        h_scr[...] = h.astype(jnp.bfloat16)

    o_ref[...] = jnp.dot(h_scr[...], w_ref[...], preferred_element_type=jnp.float32)


def _norm_proj(x, g, mod, w_bf16, shift_idx, scale_idx, col_tile):
    n_rows, d = x.shape
    n_cols = w_bf16.shape[1]
    assert n_cols % col_tile == 0
    return pl.pallas_call(
        functools.partial(_norm_proj_kernel, shift_idx=shift_idx, scale_idx=scale_idx),
        grid=(n_rows // ROW_TILE, n_cols // col_tile),
        in_specs=[
            pl.BlockSpec((ROW_TILE, d), lambda i, j: (i, 0)),
            pl.BlockSpec((1, d), lambda i, j: (0, 0)),
            pl.BlockSpec((1, N_MOD, d), lambda i, j: (_segment_of_row_block(i), 0, 0)),
            pl.BlockSpec((d, col_tile), lambda i, j: (0, j)),
        ],
        out_specs=pl.BlockSpec((ROW_TILE, col_tile), lambda i, j: (i, j)),
        out_shape=jax.ShapeDtypeStruct((n_rows, n_cols), jnp.float32),
        scratch_shapes=[pltpu.VMEM((ROW_TILE, d), jnp.bfloat16)],
        compiler_params=pltpu.CompilerParams(
            dimension_semantics=("arbitrary", "arbitrary"), vmem_limit_bytes=VMEM_LIMIT_BYTES),
        name="norm_proj",
    )(x, g, mod, w_bf16)


def _merge_kernel(x_ref, ya_ref, yb_ref, yc_ref, ga_ref, gb_ref, gc_ref, mod_ref,
                  wa_ref, wb_ref, wc_ref, wo_ref, o_ref, *, gate_idx):
    def branch(y_ref, gate_ref, w_ref):
        proj = jnp.dot(y_ref[...].astype(jnp.bfloat16), w_ref[...], preferred_element_type=jnp.float32)
        return jax.nn.sigmoid(gate_ref[...]) * proj

    merged = branch(ya_ref, ga_ref, wa_ref) + branch(yb_ref, gb_ref, wb_ref) + branch(yc_ref, gc_ref, wc_ref)
    out = jnp.dot(merged.astype(jnp.bfloat16), wo_ref[...], preferred_element_type=jnp.float32)
    o_ref[...] = x_ref[...] + mod_ref[0, gate_idx:gate_idx + 1, :] * out


def _merge(x, ya, yb, yc, p, mod, wa, wb, wc, wo):
    n_rows, d = x.shape
    gate_col0 = 0
    row = lambda i: (i, 0)
    const = lambda i: (0, 0)
    y_spec = pl.BlockSpec((ROW_TILE, ya.shape[1]), row)
    w_spec = pl.BlockSpec(wa.shape, const)
    return pl.pallas_call(
        functools.partial(_merge_kernel, gate_idx=2),
        grid=(n_rows // ROW_TILE,),
        in_specs=[
            pl.BlockSpec((ROW_TILE, d), row), y_spec, y_spec, y_spec,
            pl.BlockSpec((ROW_TILE, d), lambda i: (i, gate_col0)),
            pl.BlockSpec((ROW_TILE, d), lambda i: (i, gate_col0 + 1)),
            pl.BlockSpec((ROW_TILE, d), lambda i: (i, gate_col0 + 2)),
            pl.BlockSpec((1, N_MOD, d), lambda i: (_segment_of_row_block(i), 0, 0)),
            w_spec, w_spec, w_spec, pl.BlockSpec(wo.shape, const),
        ],
        out_specs=pl.BlockSpec((ROW_TILE, d), row),
        out_shape=jax.ShapeDtypeStruct((n_rows, d), jnp.float32),
        compiler_params=pltpu.CompilerParams(
            dimension_semantics=("arbitrary",), vmem_limit_bytes=VMEM_LIMIT_BYTES),
        name="merge_out_proj",
    )(x, ya, yb, yc, p, p, p, mod, wa, wb, wc, wo)


def _rmsnorm(x, g):
    y = x * lax.rsqrt(jnp.mean(x * x, axis=-1, keepdims=True) + RMS_EPS)
    return y * g


def _axial_rope(x):
    T = x.shape[1]
    quarter = x.shape[-1] // 4
    t = jnp.arange(T)
    pos = jnp.stack([t // GRID_W, t % GRID_W], axis=-1).astype(jnp.float32)
    freqs = ROPE_BASE ** (-jnp.arange(quarter, dtype=jnp.float32) / quarter)
    ang = pos[:, :, None] * freqs
    bshape = (T,) + (1,) * (x.ndim - 3) + (2, quarter)
    cos = jnp.cos(ang).reshape(bshape).astype(x.dtype)
    sin = jnp.sin(ang).reshape(bshape).astype(x.dtype)
    xr = x.reshape(x.shape[:-1] + (2, 2, quarter))
    x1 = xr[..., 0, :]
    x2 = xr[..., 1, :]
    out = jnp.stack([x1 * cos - x2 * sin, x2 * cos + x1 * sin], axis=-2)
    return out.reshape(x.shape)


def _dense_attention(q, k, v, sink):
    B, S, KV, G, D = q.shape
    nb = S // Q_BLOCK
    scale = D ** -0.5
    vf = v.astype(jnp.float32)
    qb = jnp.moveaxis(q.reshape(B, nb, Q_BLOCK, KV, G, D), 1, 0)

    def block(qi):
        s = jnp.einsum("bqkgd,bskd->bkgqs", qi, k).astype(jnp.float32) * scale
        m = jnp.max(s, axis=-1)
        if sink is not None:
            sk = sink.astype(jnp.float32)[None, :, :, None]
            m = jnp.maximum(m, sk)
        p = jnp.exp(s - m[..., None])
        den = jnp.sum(p, axis=-1)
        if sink is not None:
            den = den + jnp.exp(sk - m)
        o = jnp.einsum("bkgqs,bskd->bqkgd", p, vf)
        return o / jnp.moveaxis(den, -1, 1)[..., None]

    o = lax.map(block, qb)
    return jnp.moveaxis(o, 0, 1).reshape(B, S, KV * G * D).astype(q.dtype)


def _window_attention(q, k, v, ck, cv, sink):
    B, T, KV, G, D = q.shape
    nb = T // Q_BLOCK
    span = Q_BLOCK + 2 * WINDOW
    scale = D ** -0.5
    pad = ((0, 0), (WINDOW, WINDOW), (0, 0), (0, 0))
    kp = jnp.pad(k, pad)
    vp = jnp.pad(v, pad)
    idx = jnp.arange(nb)[:, None] * Q_BLOCK + jnp.arange(span)[None, :]
    kb = kp[:, idx]
    vb = vp[:, idx].astype(jnp.float32)
    qb = q.reshape(B, nb, Q_BLOCK, KV, G, D)
    s = jnp.einsum("bnqkgd,bnskd->bnkgqs", qb, kb).astype(jnp.float32) * scale
    qpos = jnp.arange(nb)[:, None] * Q_BLOCK + jnp.arange(Q_BLOCK)[None, :]
    kpos = idx - WINDOW
    valid = ((jnp.abs(qpos[:, :, None] - kpos[:, None, :]) <= WINDOW)
             & (kpos >= 0)[:, None, :] & (kpos < T)[:, None, :])
    s = jnp.where(valid[None, :, None, None], s, -jnp.inf)
    sc = jnp.einsum("bnqkgd,bpkd->bnkgqp", qb, ck).astype(jnp.float32) * scale
    sk = sink.astype(jnp.float32)[None, None, :, :, None]
    m = jnp.maximum(jnp.maximum(jnp.max(s, axis=-1), jnp.max(sc, axis=-1)), sk)
    ps = jnp.exp(s - m[..., None])
    pc = jnp.exp(sc - m[..., None])
    den = jnp.sum(ps, axis=-1) + jnp.sum(pc, axis=-1) + jnp.exp(sk - m)
    o = (jnp.einsum("bnkgqs,bnskd->bnqkgd", ps, vb)
         + jnp.einsum("bnkgqp,bpkd->bnqkgd", pc, cv.astype(jnp.float32)))
    o = o / jnp.moveaxis(den, -1, 2)[..., None]
    return o.reshape(B, T, KV * G * D).astype(q.dtype)


def _neighbourhood_attention(q, k, v, ck, cv, rpb):
    B, T, H, D = q.shape
    rows = T // GRID_W
    wr = min(NA_ROWS, rows)
    scale = D ** -0.5
    qg = q.reshape(B, rows, GRID_W, H, D)
    kg = k.reshape(B, rows, GRID_W, H, D)
    vg = v.reshape(B, rows, GRID_W, H, D)
    r = jnp.arange(rows)
    rs = jnp.clip(r - wr // 2, 0, rows - wr)
    row_idx = rs[:, None] + jnp.arange(wr)[None, :]
    k_rows = kg[:, row_idx]
    v_rows = vg[:, row_idx].astype(jnp.float32)
    col = jnp.arange(GRID_W)
    cs = jnp.clip(col - NA_COLS // 2, 0, GRID_W - NA_COLS)
    col_ok = (col[None, :] >= cs[:, None]) & (col[None, :] < cs[:, None] + NA_COLS)
    dr = row_idx - r[:, None] + NA_ROWS - 1
    dc = jnp.clip(col[None, :] - col[:, None], -(NA_COLS - 1), NA_COLS - 1) + NA_COLS - 1
    bias = rpb[:, dr[:, None, :, None], dc[None, :, None, :]].astype(jnp.float32)
    s = jnp.einsum("brchd,brawhd->bhrcaw", qg, k_rows).astype(jnp.float32) * scale + bias[None]
    s = jnp.where(col_ok[None, None, None, :, None, :], s, -jnp.inf)
    sc = jnp.einsum("brchd,bphd->bhrcp", qg, ck).astype(jnp.float32) * scale
    m = jnp.maximum(jnp.max(s, axis=(-2, -1)), jnp.max(sc, axis=-1))
    pn = jnp.exp(s - m[..., None, None])
    pc = jnp.exp(sc - m[..., None])
    den = jnp.sum(pn, axis=(-2, -1)) + jnp.sum(pc, axis=-1)
    o = (jnp.einsum("bhrcaw,brawhd->brchd", pn, v_rows)
         + jnp.einsum("bhrcp,bphd->brchd", pc, cv.astype(jnp.float32)))
    o = o / jnp.transpose(den, (0, 2, 3, 1))[..., None]
    return o.reshape(B, T, H * D).astype(q.dtype)


def _heads(z):
    return z.reshape(z.shape[:2] + (B_HEADS, HEAD_DIM))


def _rwkv7_scan(s0, r, w, k, v, kk, a, reverse):
    xs = tuple(jnp.moveaxis(z, 1, 0) for z in (r, w, k, v, kk, a))

    def step(S, xt):
        rt, wt, kt, vt, kkt, at = xt
        sa = jnp.einsum("bhij,bhj->bhi", S, -kkt)
        S = (S * wt[:, :, None, :] + sa[..., None] * (kkt * at)[:, :, None, :]
             + vt[..., None] * kt[:, :, None, :])
        return S, jnp.einsum("bhij,bhj->bhi", S, rt)

    S, ys = lax.scan(step, s0.astype(jnp.float32), xs, reverse=reverse)
    return jnp.moveaxis(ys, 0, 1), S


def _rwkv7_bidir(pb, lp, s0_f, s0_b):
    B, T, _ = pb.shape
    prev = jnp.pad(pb[:, :-1], ((0, 0), (1, 0), (0, 0)))
    nxt = jnp.pad(pb[:, 1:], ((0, 0), (0, 1), (0, 0)))
    xb = pb + lp["rw_mu"] * (0.5 * (prev + nxt) - pb)
    r, k, v, wd, ad, gd = jnp.split(
        xb, [B_WIDTH, 2 * B_WIDTH, 3 * B_WIDTH, 3 * B_WIDTH + 2 * LORA_W,
             3 * B_WIDTH + 2 * LORA_W + 2 * LORA_A], axis=-1)
    wd = wd.reshape(B, T, 2, LORA_W)
    ad = ad.reshape(B, T, 2, LORA_A)
    wlog = -jax.nn.softplus(-(lp["rw_w0"] + jnp.einsum("btzl,zlc->btzc", jnp.tanh(wd), lp["rw_w2"]))) - 0.5
    decay = jnp.exp(-jnp.exp(wlog.astype(jnp.float32)))
    a = jax.nn.sigmoid(lp["rw_a0"] + jnp.einsum("btzl,zlc->btzc", ad, lp["rw_a2"])).astype(jnp.float32)
    g = (jax.nn.sigmoid(gd) @ lp["rw_g2"]).astype(jnp.float32)
    kk = _heads(k * lp["rw_kk"])
    kk = kk * lax.rsqrt(jnp.sum(kk * kk, axis=-1, keepdims=True) + 1e-12)
    kd = k[:, :, None, :] * (1 + (a - 1) * lp["rw_ka"])
    y_f, s_f = _rwkv7_scan(s0_f, _heads(r), _heads(decay[:, :, 0]), _heads(kd[:, :, 0]),
                           _heads(v), kk, _heads(a[:, :, 0]), False)
    y_b, s_b = _rwkv7_scan(s0_b, _heads(r), _heads(decay[:, :, 1]), _heads(kd[:, :, 1]),
                           _heads(v), kk, _heads(a[:, :, 1]), True)
    y = y_f + y_b
    mu = jnp.mean(y, axis=-1, keepdims=True)
    var = jnp.mean(jnp.square(y - mu), axis=-1, keepdims=True)
    y = ((y - mu) * lax.rsqrt(var + GN_EPS)).reshape(B, T, B_WIDTH) * lp["rw_lnx_g"] + lp["rw_lnx_b"]
    bonus = jnp.sum(_heads(r * (kd[:, :, 0] + kd[:, :, 1])) * lp["rw_rk"], axis=-1, keepdims=True) * _heads(v)
    y = (y + bonus.reshape(B, T, B_WIDTH)) * g
    return y, s_f, s_b


def _mixers(p, lp, ctx):
    B, T, _ = p.shape
    pa, pb, pc = jnp.split(p, [A_COLS, A_COLS + B_COLS], axis=-1)
    qa, ka, va = jnp.split(pa, [A_HEADS * HEAD_DIM, (A_HEADS + A_KV) * HEAD_DIM], axis=-1)
    qa = qa.reshape(B, T, A_KV, A_GROUP, HEAD_DIM)
    ka = ka.reshape(B, T, A_KV, HEAD_DIM)
    va = va.reshape(B, T, A_KV, HEAD_DIM)
    qc, kc, vc = jnp.split(pc, 3, axis=-1)
    qc = qc.reshape(B, T, C_HEADS, HEAD_DIM)
    kc = kc.reshape(B, T, C_HEADS, HEAD_DIM)
    vc = vc.reshape(B, T, C_HEADS, HEAD_DIM)
    sink = lp["a_sink"].reshape(A_KV, A_GROUP)
    if ctx is None:
        ya = _dense_attention(qa, ka, va, sink)
        yc = _dense_attention(qc[:, :, :, None, :], kc, vc, None)
        s0_f = jnp.zeros((B, B_HEADS, HEAD_DIM, HEAD_DIM), jnp.float32)
        s0_b = s0_f
    else:
        ctx_ak, ctx_av, ctx_ck, ctx_cv, s0 = ctx
        ya = _window_attention(_axial_rope(qa), _axial_rope(ka), va, ctx_ak, ctx_av, sink)
        yc = _neighbourhood_attention(qc, kc, vc, ctx_ck, ctx_cv, lp["na_rpb"])
        s0_f = s0[:, 0]
        s0_b = s0[:, 1]
    yb, s_f, s_b = _rwkv7_bidir(pb, lp, s0_f, s0_b)
    return ya, yb, yc, (ka, va, kc, vc, jnp.stack([s_f, s_b], axis=1))


def _peer(q, h, subkeys, u_tab, v_tab):
    n = h.shape[0]
    hc_all = h.reshape(n // PEER_CHUNK, PEER_CHUNK, D_MODEL)
    q_all = q.reshape(n // PEER_CHUNK, PEER_CHUNK, PEER_HEADS, 2, PEER_KEY_DIM // 2)

    def chunk(args):
        hc, qq = args
        s = jnp.einsum("chzk,hznk->chzn", qq, subkeys).astype(jnp.float32)
        v1, i1 = lax.top_k(s[:, :, 0], PEER_TOPK)
        v2, i2 = lax.top_k(s[:, :, 1], PEER_TOPK)
        cand = (v1[..., :, None] + v2[..., None, :]).reshape(PEER_CHUNK, PEER_HEADS, PEER_TOPK * PEER_TOPK)
        sv, si = lax.top_k(cand, PEER_TOPK)
        e = (jnp.take_along_axis(i1, si // PEER_TOPK, axis=-1) * N_KEYS
             + jnp.take_along_axis(i2, si % PEER_TOPK, axis=-1))
        gate = jax.nn.softmax(sv, axis=-1)
        u = jnp.take(u_tab, e, axis=0)
        act = jax.nn.gelu(jnp.einsum("cd,chkd->chk", hc, u).astype(jnp.float32))
        vv = jnp.take(v_tab, e, axis=0)
        return jnp.einsum("chk,chkd->cd", (gate * act).astype(hc.dtype), vv)

    return lax.map(chunk, (hc_all, q_all)).reshape(n, D_MODEL)


def kernel(x_prompt, x_sample, cache_a_k, cache_a_v, cache_c_k, cache_c_v, state_rwkv, c, c_ctx, ln1_g, ln2_g, lnf_g, ada_w, ada_b, w_in, a_sink, a_out, rw_mu, rw_w0, rw_w2, rw_a0, rw_a2, rw_g2, rw_kk, rw_ka, rw_rk, rw_lnx_g, rw_lnx_b, rw_out, na_rpb, na_out, w_o, pe_q, pe_subkeys, pe_u, pe_v):
    bf16 = jnp.bfloat16
    x = jnp.concatenate([x_prompt.reshape(N_CTX_TOK, D_MODEL), x_sample.reshape(N_LAT_TOK, D_MODEL)], axis=0)
    silu_all = jnp.concatenate([jax.nn.silu(c_ctx)[None, :], jax.nn.silu(c)], axis=0)
    n_mix = A_COLS + B_COLS + C_COLS
    list_ak, list_av, list_ck, list_cv, list_st = [], [], [], [], []
    for l in range(DEPTH):
        lp = {
            "a_sink": a_sink[l], "rw_mu": rw_mu[l], "rw_w0": rw_w0[l], "rw_w2": rw_w2[l], "rw_a0": rw_a0[l],
            "rw_a2": rw_a2[l], "rw_g2": rw_g2[l], "rw_kk": rw_kk[l], "rw_ka": rw_ka[l],
            "rw_rk": rw_rk[l], "rw_lnx_g": rw_lnx_g[l], "rw_lnx_b": rw_lnx_b[l], "na_rpb": na_rpb[l],
        }
        mod = (silu_all @ ada_w[l] + ada_b[l]).reshape(N_SEG, N_MOD, D_MODEL)

        w_in_l = jnp.concatenate([w_in[l][:, n_mix:], w_in[l][:, :n_mix]], axis=1).astype(bf16)
        p = _norm_proj(x, ln1_g[l][None, :], mod, w_in_l, 0, 1, col_tile=IN_COLS // 3)
        pm_ctx = p[:N_CTX_TOK, N_BRANCH * D_MODEL:].reshape(BATCH, SEQ, n_mix)
        pm_lat = p[N_CTX_TOK:, N_BRANCH * D_MODEL:].reshape(DEC_BATCH, DEC_SEQ, n_mix)
        ya_c, yb_c, yc_c, (ak, av, ck, cv, st) = _mixers(pm_ctx, lp, None)
        ctx = (cache_a_k[:, l], cache_a_v[:, l], cache_c_k[:, l], cache_c_v[:, l], state_rwkv[:, l])
        ya_l, yb_l, yc_l, _ = _mixers(pm_lat, lp, ctx)
        list_ak.append(ak)
        list_av.append(av)
        list_ck.append(ck)
        list_cv.append(cv)
        list_st.append(st)
        cat = lambda u, v: jnp.concatenate([u.reshape(N_CTX_TOK, -1), v.reshape(N_LAT_TOK, -1)], axis=0)
        x = _merge(x, cat(ya_c, ya_l), cat(yb_c, yb_l), cat(yc_c, yc_l), p, mod,
                   a_out[l].astype(bf16), rw_out[l].astype(bf16), na_out[l].astype(bf16), w_o[l].astype(bf16))

        q = _norm_proj(x, ln2_g[l][None, :], mod, pe_q[l].astype(bf16), 3, 4, col_tile=PEER_HEADS * PEER_KEY_DIM)
        seg_mod = jnp.concatenate([jnp.repeat(mod[:1], N_CTX_TOK, axis=0),
                                   jnp.repeat(mod[1:], DEC_SEQ, axis=0)], axis=0)
        h = _rmsnorm(x, ln2_g[l]) * (1 + seg_mod[:, 4]) + seg_mod[:, 3]
        x = x + seg_mod[:, 5] * _peer(q, h, pe_subkeys[l], pe_u[l], pe_v[l])

    y = _rmsnorm(x, lnf_g)
    y_prompt = y[:N_CTX_TOK].reshape(BATCH, SEQ, D_MODEL)
    y_sample = y[N_CTX_TOK:].reshape(DEC_BATCH, DEC_SEQ, D_MODEL)
    return (y_prompt, y_sample, jnp.stack(list_ak, axis=1), jnp.stack(list_av, axis=1),
            jnp.stack(list_ck, axis=1), jnp.stack(list_cv, axis=1), jnp.stack(list_st, axis=1))
```

```python
import functools

import jax
import jax.numpy as jnp
from jax import lax
from jax.experimental import pallas as pl
from jax.experimental.pallas import tpu as pltpu

D_MODEL = 1024
BATCH = 16
SEQ = 256
DEPTH = 4
DEC_BATCH = 2
DEC_SEQ = 2048
PAST_LEN = 512
GRID_W = 64
HEAD_DIM = 64
A_HEADS = 8
A_KV = 2
A_GROUP = A_HEADS // A_KV
WINDOW = 128
Q_BLOCK = 128
B_HEADS = 8
B_WIDTH = B_HEADS * HEAD_DIM
LORA_W = 64
LORA_A = 64
LORA_G = 128
GN_EPS = 64e-5
C_HEADS = 8
NA_ROWS = 8
NA_COLS = 16
N_BRANCH = 3
A_COLS = (A_HEADS + 2 * A_KV) * HEAD_DIM
B_COLS = 3 * B_WIDTH + 2 * LORA_W + 2 * LORA_A + LORA_G
C_COLS = 3 * C_HEADS * HEAD_DIM
IN_COLS = A_COLS + B_COLS + C_COLS + N_BRANCH * D_MODEL
PEER_HEADS = 8
N_KEYS = 128
N_EXPERTS = N_KEYS * N_KEYS
PEER_KEY_DIM = 256
PEER_TOPK = 16
PEER_CHUNK = 128
ROPE_BASE = 10000.0
RMS_EPS = 1e-6
N_MOD = 6

N_CTX_TOK = BATCH * SEQ
N_LAT_TOK = DEC_BATCH * DEC_SEQ
N_TOK = N_CTX_TOK + N_LAT_TOK
N_SEG = 1 + DEC_BATCH

VMEM_LIMIT_BYTES = 48 * 1024 * 1024

ROW_TILE = 512
assert N_CTX_TOK % ROW_TILE == 0 and DEC_SEQ % ROW_TILE == 0


def _segment_of_row_block(i):
    n_ctx_blocks = N_CTX_TOK // ROW_TILE
    blocks_per_lat = DEC_SEQ // ROW_TILE
    return jnp.where(i < n_ctx_blocks, 0, 1 + (i - n_ctx_blocks) // blocks_per_lat)


def _norm_proj_kernel(x_ref, g_ref, mod_ref, w_ref, o_ref, h_ref, *, shift_idx, scale_idx):
    @pl.when(pl.program_id(1) == 0)
    def _():
        x = x_ref[...]
        y = x * lax.rsqrt(jnp.mean(x * x, axis=-1, keepdims=True) + RMS_EPS) * g_ref[...]
        h = y * (1.0 + mod_ref[0, scale_idx:scale_idx + 1, :]) + mod_ref[0, shift_idx:shift_idx + 1, :]
        h_ref[...] = h.astype(jnp.bfloat16)

    o_ref[...] = jnp.dot(h_ref[...], w_ref[...], preferred_element_type=jnp.float32)


def _norm_proj(x, g, mod, w_bf16, shift_idx, scale_idx, col_tile):
    n_rows, d = x.shape
    n_cols = w_bf16.shape[1]
    assert n_cols % col_tile == 0
    return pl.pallas_call(
        functools.partial(_norm_proj_kernel, shift_idx=shift_idx, scale_idx=scale_idx),
        grid=(n_rows // ROW_TILE, n_cols // col_tile),
        in_specs=[
            pl.BlockSpec((ROW_TILE, d), lambda i, j: (i, 0)),
            pl.BlockSpec((1, d), lambda i, j: (0, 0)),
            pl.BlockSpec((1, N_MOD, d), lambda i, j: (_segment_of_row_block(i), 0, 0)),
            pl.BlockSpec((d, col_tile), lambda i, j: (0, j)),
        ],
        out_specs=[pl.BlockSpec((ROW_TILE, col_tile), lambda i, j: (i, j)),
                   pl.BlockSpec((ROW_TILE, d), lambda i, j: (i, 0))],
        out_shape=[jax.ShapeDtypeStruct((n_rows, n_cols), jnp.float32),
                   jax.ShapeDtypeStruct((n_rows, d), jnp.bfloat16)],
        compiler_params=pltpu.CompilerParams(
            dimension_semantics=("arbitrary", "arbitrary"), vmem_limit_bytes=VMEM_LIMIT_BYTES),
        name="norm_proj",
    )(x, g, mod, w_bf16)


LANE = 128
ROUTE_TOK = 512
PEER_TOK = 512
PEER_EXP = 1024
NEG_INF = float("-inf")


def _take_top(vals, k):
    rows = lax.broadcasted_iota(jnp.int32, vals.shape, 0)
    taken = []
    for _ in range(k):
        m = jnp.max(vals, axis=0, keepdims=True)
        first = jnp.min(jnp.where(vals == m, rows, vals.shape[0]), axis=0, keepdims=True)
        vals = jnp.where(rows == first, NEG_INF, vals)
        taken.append(m)
    return jnp.concatenate(taken, axis=0), vals


def _peer_route_kernel(q_ref, sk_ref, s1m_ref, s2m_ref, thr_ref, c0_ref):
    def sub_tile(t, carry):
        tok = pl.ds(pl.multiple_of(t * LANE, LANE), LANE)
        masked, top_vals = [], []
        for z in range(2):
            qz = q_ref[tok, z * LANE:(z + 1) * LANE]
            s = lax.dot_general(sk_ref[0, z], qz, (((1,), (1,)), ((), ())),
                                preferred_element_type=jnp.float32)
            v, rest = _take_top(s, PEER_TOPK)
            masked.append(jnp.where(rest == NEG_INF, s, NEG_INF))
            top_vals.append(v)
        cand = jnp.concatenate([top_vals[0][i:i + 1] + top_vals[1] for i in range(PEER_TOPK)], axis=0)
        sv, _ = _take_top(cand, PEER_TOPK)
        z_sum = jnp.sum(jnp.exp(sv - sv[0:1]), axis=0, keepdims=True)
        s1m_ref[0, :, tok] = masked[0]
        s2m_ref[0, :, tok] = masked[1]
        thr_ref[0, :, tok] = sv[PEER_TOPK - 1:PEER_TOPK]
        c0_ref[0, :, tok] = -sv[0:1] - jnp.log(z_sum)
        return carry

    lax.fori_loop(0, ROUTE_TOK // LANE, sub_tile, 0)


def _peer_route(q, subkeys):
    n = q.shape[0]
    big = pl.BlockSpec((1, N_KEYS, ROUTE_TOK), lambda i, h: (h, 0, i))
    small = pl.BlockSpec((1, 1, ROUTE_TOK), lambda i, h: (h, 0, i))
    return pl.pallas_call(
        _peer_route_kernel,
        grid=(n // ROUTE_TOK, PEER_HEADS),
        in_specs=[
            pl.BlockSpec((ROUTE_TOK, PEER_KEY_DIM), lambda i, h: (i, h)),
            pl.BlockSpec((1, 2, N_KEYS, PEER_KEY_DIM // 2), lambda i, h: (h, 0, 0, 0)),
        ],
        out_specs=[big, big, small, small],
        out_shape=[jax.ShapeDtypeStruct((PEER_HEADS, N_KEYS, n), jnp.float32)] * 2
        + [jax.ShapeDtypeStruct((PEER_HEADS, 1, n), jnp.float32)] * 2,
        compiler_params=pltpu.CompilerParams(
            dimension_semantics=("arbitrary", "arbitrary"), vmem_limit_bytes=VMEM_LIMIT_BYTES),
        name="peer_route",
    )(q, subkeys)


def _gelu_tanh(x):
    return 0.5 * x * (1.0 + jnp.tanh(0.7978845608028654 * (x + 0.044715 * (x * x * x))))


def _peer_dense_kernel(h_ref, x_ref, mod_ref, s1m_ref, s2m_ref, thr_ref, c0_ref, u_ref, vt_ref, o_ref,
                       ht_scr, s2c_scr, coef_scr, acc_scr, *, gate_idx):
    e = pl.program_id(1)

    @pl.when(e == 0)
    def _():
        ht_scr[...] = h_ref[...].astype(jnp.float32).T.astype(jnp.bfloat16)
        s2c_scr[...] = s2m_ref[...] + c0_ref[...]
        acc_scr[...] = jnp.zeros_like(acc_scr)

    act = jnp.dot(u_ref[...], ht_scr[...], preferred_element_type=jnp.float32)
    a_per_step = PEER_EXP // N_KEYS
    a_rows = pl.ds(pl.multiple_of(e * a_per_step, a_per_step), a_per_step)
    for t in range(PEER_TOK // LANE):
        tok = slice(t * LANE, (t + 1) * LANE)
        s1_all = [s1m_ref[hh, a_rows, tok] for hh in range(PEER_HEADS)]
        for a in range(a_per_step):
            rows = slice(a * N_KEYS, (a + 1) * N_KEYS)
            w = jnp.zeros((N_KEYS, LANE), jnp.float32)
            for hh in range(PEER_HEADS):
                s1 = s1_all[hh][a:a + 1]
                picked = (s1 + s2m_ref[hh, :, tok]) >= thr_ref[hh, :, tok]
                w = w + jnp.where(picked, jnp.exp(s1 + s2c_scr[hh, :, tok]), 0.0)
            coef_scr[rows, tok] = (w * _gelu_tanh(act[rows, tok])).astype(jnp.bfloat16)
    acc_scr[...] += jnp.dot(vt_ref[...], coef_scr[...], preferred_element_type=jnp.float32)

    @pl.when(e == pl.num_programs(1) - 1)
    def _():
        o_ref[...] = x_ref[...] + mod_ref[0, gate_idx:gate_idx + 1, :] * acc_scr[...].T


def _peer_dense(h, x, mod, s1m, s2m, thr, c0, u_bf16, vt_bf16):
    n, d = x.shape
    assert PEER_TOK == ROW_TILE
    tok_row = lambda i, e: (i, 0)
    big = pl.BlockSpec((PEER_HEADS, N_KEYS, PEER_TOK), lambda i, e: (0, 0, i))
    small = pl.BlockSpec((PEER_HEADS, 1, PEER_TOK), lambda i, e: (0, 0, i))
    return pl.pallas_call(
        functools.partial(_peer_dense_kernel, gate_idx=5),
        grid=(n // PEER_TOK, N_EXPERTS // PEER_EXP),
        in_specs=[
            pl.BlockSpec((PEER_TOK, d), tok_row),
            pl.BlockSpec((PEER_TOK, d), tok_row),
            pl.BlockSpec((1, N_MOD, d), lambda i, e: (_segment_of_row_block(i), 0, 0)),
            big, big, small, small,
            pl.BlockSpec((PEER_EXP, d), lambda i, e: (e, 0)),
            pl.BlockSpec((d, PEER_EXP), lambda i, e: (0, e)),
        ],
        out_specs=pl.BlockSpec((PEER_TOK, d), tok_row),
        out_shape=jax.ShapeDtypeStruct((n, d), jnp.float32),
        scratch_shapes=[
            pltpu.VMEM((d, PEER_TOK), jnp.bfloat16),
            pltpu.VMEM((PEER_HEADS, N_KEYS, PEER_TOK), jnp.float32),
            pltpu.VMEM((PEER_EXP, PEER_TOK), jnp.bfloat16),
            pltpu.VMEM((d, PEER_TOK), jnp.float32),
        ],
        compiler_params=pltpu.CompilerParams(
            dimension_semantics=("arbitrary", "arbitrary"), vmem_limit_bytes=VMEM_LIMIT_BYTES),
        name="peer_dense",
    )(h, x, mod, s1m, s2m, thr, c0, u_bf16, vt_bf16)


def _merge_kernel(x_ref, ya_ref, yb_ref, yc_ref, ga_ref, gb_ref, gc_ref, mod_ref,
                  wa_ref, wb_ref, wc_ref, wo_ref, o_ref, *, gate_idx):
    def branch(y_ref, gate_ref, w_ref):
        proj = jnp.dot(y_ref[...].astype(jnp.bfloat16), w_ref[...], preferred_element_type=jnp.float32)
        return jax.nn.sigmoid(gate_ref[...]) * proj

    merged = branch(ya_ref, ga_ref, wa_ref) + branch(yb_ref, gb_ref, wb_ref) + branch(yc_ref, gc_ref, wc_ref)
    out = jnp.dot(merged.astype(jnp.bfloat16), wo_ref[...], preferred_element_type=jnp.float32)
    o_ref[...] = x_ref[...] + mod_ref[0, gate_idx:gate_idx + 1, :] * out


def _merge(x, ya, yb, yc, p, mod, wa, wb, wc, wo):
    n_rows, d = x.shape
    gate_col0 = 0
    row = lambda i: (i, 0)
    const = lambda i: (0, 0)
    y_spec = pl.BlockSpec((ROW_TILE, ya.shape[1]), row)
    w_spec = pl.BlockSpec(wa.shape, const)
    return pl.pallas_call(
        functools.partial(_merge_kernel, gate_idx=2),
        grid=(n_rows // ROW_TILE,),
        in_specs=[
            pl.BlockSpec((ROW_TILE, d), row), y_spec, y_spec, y_spec,
            pl.BlockSpec((ROW_TILE, d), lambda i: (i, gate_col0)),
            pl.BlockSpec((ROW_TILE, d), lambda i: (i, gate_col0 + 1)),
            pl.BlockSpec((ROW_TILE, d), lambda i: (i, gate_col0 + 2)),
            pl.BlockSpec((1, N_MOD, d), lambda i: (_segment_of_row_block(i), 0, 0)),
            w_spec, w_spec, w_spec, pl.BlockSpec(wo.shape, const),
        ],
        out_specs=pl.BlockSpec((ROW_TILE, d), row),
        out_shape=jax.ShapeDtypeStruct((n_rows, d), jnp.float32),
        compiler_params=pltpu.CompilerParams(
            dimension_semantics=("arbitrary",), vmem_limit_bytes=VMEM_LIMIT_BYTES),
        name="merge_out_proj",
    )(x, ya, yb, yc, p, p, p, mod, wa, wb, wc, wo)


HEAD_PAIRS = B_HEADS // 2
PREP_TOK = 256
SCAN_CHUNK = 128
assert PREP_TOK == SEQ and DEC_SEQ % PREP_TOK == 0 and 2 * HEAD_DIM == LANE
_HI = lax.Precision.HIGHEST


def _head_ones(n):
    r = lax.broadcasted_iota(jnp.int32, (n, n), 0) // HEAD_DIM
    c = lax.broadcasted_iota(jnp.int32, (n, n), 1) // HEAD_DIM
    return (r == c).astype(jnp.float32)


def _softplus(y):
    return jnp.maximum(y, 0.0) + jnp.log(1.0 + jnp.exp(-jnp.abs(y)))


def _rwkv_prep_kernel(pb_ref, prev_ref, next_ref, mu_ref, w0_ref, w2_ref, a0_ref, a2_ref, g2_ref,
                      kks_ref, ka_ref, rk_ref, ones_ref,
                      r_o, kk_o, v_o, w0_o, w1_o, k0_o, k1_o, nb0_o, nb1_o, g_o, bonus_o):
    i = pl.program_id(0)
    n_ctx = N_CTX_TOK // PREP_TOK
    per_lat = DEC_SEQ // PREP_TOK
    j = (i - n_ctx) % per_lat
    starts_seq = jnp.logical_or(i < n_ctx, j == 0)
    ends_seq = jnp.logical_or(i < n_ctx, j == per_lat - 1)

    pb = pb_ref[...]
    rows = lax.broadcasted_iota(jnp.int32, (PREP_TOK, 1), 0)
    before = jnp.where(starts_seq, 0.0, prev_ref[7:8, :])
    after = jnp.where(ends_seq, 0.0, next_ref[0:1, :])
    prev = jnp.where(rows == 0, before, pltpu.roll(pb, 1, 0))
    nxt = jnp.where(rows == PREP_TOK - 1, after, pltpu.roll(pb, PREP_TOK - 1, 0))
    xb = pb + mu_ref[...] * (0.5 * (prev + nxt) - pb)

    w = B_WIDTH
    r, k, v = xb[:, 0:w], xb[:, w:2 * w], xb[:, 2 * w:3 * w]
    wd = xb[:, 3 * w:3 * w + 2 * LORA_W]
    ad = xb[:, 3 * w + 2 * LORA_W:3 * w + 2 * LORA_W + 2 * LORA_A]
    gd = xb[:, 3 * w + 2 * LORA_W + 2 * LORA_A:]
    ones = ones_ref[...]

    def head_sum(x):
        return jnp.dot(x, ones, preferred_element_type=jnp.float32, precision=_HI)

    lw = jnp.dot(jnp.tanh(wd), w2_ref[...], preferred_element_type=jnp.float32, precision=_HI)
    la = jnp.dot(ad, a2_ref[...], preferred_element_type=jnp.float32, precision=_HI)
    kk = k * kks_ref[...]
    kk = kk * lax.rsqrt(head_sum(kk * kk) + 1e-12)
    kd_sum = jnp.zeros_like(k)
    for z, (w_o, k_o, nb_o) in enumerate(((w0_o, k0_o, nb0_o), (w1_o, k1_o, nb1_o))):
        wlog = -_softplus(-(w0_ref[z:z + 1, :] + lw[:, z * w:(z + 1) * w])) - 0.5
        a = jax.nn.sigmoid(a0_ref[z:z + 1, :] + la[:, z * w:(z + 1) * w])
        kd = k * (1.0 + (a - 1.0) * ka_ref[...])
        kd_sum = kd_sum + kd
        w_o[...] = jnp.exp(-jnp.exp(wlog))
        k_o[...] = kd
        nb_o[...] = -(kk * a)
    r_o[...] = r
    kk_o[...] = kk
    v_o[...] = v
    g_o[...] = jnp.dot(jax.nn.sigmoid(gd), g2_ref[...], preferred_element_type=jnp.float32, precision=_HI)
    bonus_o[...] = head_sum(r * kd_sum * rk_ref[...]) * v


def _rwkv_prep(p, mu, w0, w2_pad, a0, a2_pad, g2, kk_scale, ka, rk):
    n = p.shape[0]
    col_blk = (N_BRANCH * D_MODEL + A_COLS) // B_COLS
    assert col_blk * B_COLS == N_BRANCH * D_MODEL + A_COLS
    halo = 8
    per = PREP_TOK // halo
    full = lambda a: pl.BlockSpec(a.shape, lambda i: (0,) * a.ndim)
    ones = _head_ones(B_WIDTH)
    consts = (mu, w0, w2_pad, a0, a2_pad, g2, kk_scale, ka, rk, ones)
    out_spec = pl.BlockSpec((PREP_TOK, B_WIDTH), lambda i: (i, 0))
    return pl.pallas_call(
        _rwkv_prep_kernel,
        grid=(n // PREP_TOK,),
        in_specs=[
            pl.BlockSpec((PREP_TOK, B_COLS), lambda i: (i, col_blk)),
            pl.BlockSpec((halo, B_COLS), lambda i: (jnp.maximum(i * per - 1, 0), col_blk)),
            pl.BlockSpec((halo, B_COLS), lambda i: (jnp.minimum((i + 1) * per, n // halo - 1), col_blk)),
        ] + [full(a) for a in consts],
        out_specs=[out_spec] * 11,
        out_shape=[jax.ShapeDtypeStruct((n, B_WIDTH), jnp.float32)] * 11,
        compiler_params=pltpu.CompilerParams(
            dimension_semantics=("arbitrary",), vmem_limit_bytes=VMEM_LIMIT_BYTES),
        name="rwkv_prep",
    )(p, p, p, *consts)


def _half_sum(x, pair_ones, passes):
    hi = x.astype(jnp.bfloat16)
    out = jnp.dot(hi, pair_ones, preferred_element_type=jnp.float32)
    if passes == 2:
        lo = (x - hi.astype(jnp.float32)).astype(jnp.bfloat16)
        out = out + jnp.dot(lo, pair_ones, preferred_element_type=jnp.float32)
    return out


def _rwkv_scan_kernel(rf, kkf, vf, wf, kf, nbf, rb, kkb, vb, wb, kb, nbb, s0_ref, ones_ref, eye_ref,
                      yf_ref, yb_ref, sfin_ref, s_scr, *, n_seq):
    c = pl.program_id(1)

    @pl.when(c == 0)
    def _():
        s_scr[...] = s0_ref[...]

    pair_ones = ones_ref[...]
    eye2 = eye_ref[...]
    directions = ((rf, kkf, vf, wf, kf, nbf, yf_ref), (rb, kkb, vb, wb, kb, nbb, yb_ref))

    def step(t, carry):
        for z, (r_, kk_, v_, w_, k_, nb_, y_) in enumerate(directions):
            row = pl.ds(t if z == 0 else SCAN_CHUNK - 1 - t, 1)
            for g in range(n_seq):
                r_t, kk_t, v_t, w_t, k_t, nb_t = (a[g, row, :] for a in (r_, kk_, v_, w_, k_, nb_))
                y_parts = []
                for p in range(HEAD_PAIRS):
                    lanes = slice(p * LANE, (p + 1) * LANE)
                    s = s_scr[g, z, p]
                    s_kk = _half_sum(s * kk_t[:, lanes], pair_ones, 2)
                    v_col = _half_sum(eye2 * v_t[:, lanes], pair_ones, 2)
                    s = s * w_t[:, lanes] + s_kk * nb_t[:, lanes] + v_col * k_t[:, lanes]
                    s_scr[g, z, p] = s
                    y_all = _half_sum(s * r_t[:, lanes], pair_ones, 1)
                    y_parts.append(jnp.sum(y_all * eye2, axis=0, keepdims=True))
                y_[g, row, :] = jnp.concatenate(y_parts, axis=1)
        return carry

    lax.fori_loop(0, SCAN_CHUNK, step, 0)

    @pl.when(c == pl.num_programs(1) - 1)
    def _():
        sfin_ref[...] = s_scr[...]


def _rwkv_scan(seqs, s0, seq_group):
    r, kk, v, w0, w1, k0, k1, nb0, nb1 = seqs
    n_seq, t_len, _ = r.shape
    n_chunks = t_len // SCAN_CHUNK
    assert n_seq % seq_group == 0 and n_chunks * SCAN_CHUNK == t_len
    blk = (seq_group, SCAN_CHUNK, B_WIDTH)
    fwd = pl.BlockSpec(blk, lambda i, c: (i, c, 0))
    bwd = pl.BlockSpec(blk, lambda i, c: (i, n_chunks - 1 - c, 0))
    state = pl.BlockSpec((seq_group, 2, HEAD_PAIRS, HEAD_DIM, LANE), lambda i, c: (i, 0, 0, 0, 0))
    pair_ones = _head_ones(LANE).astype(jnp.bfloat16)
    eye = (lax.broadcasted_iota(jnp.int32, (HEAD_DIM, LANE), 0)
           == lax.broadcasted_iota(jnp.int32, (HEAD_DIM, LANE), 1) % HEAD_DIM).astype(jnp.float32)
    const = lambda a: pl.BlockSpec(a.shape, lambda i, c: (0, 0))
    y_shape = jax.ShapeDtypeStruct((n_seq, t_len, B_WIDTH), jnp.float32)
    return pl.pallas_call(
        functools.partial(_rwkv_scan_kernel, n_seq=seq_group),
        grid=(n_seq // seq_group, n_chunks),
        in_specs=[fwd] * 6 + [bwd] * 6 + [state, const(pair_ones), const(eye)],
        out_specs=[fwd, bwd, state],
        out_shape=[y_shape, y_shape, jax.ShapeDtypeStruct(s0.shape, jnp.float32)],
        scratch_shapes=[pltpu.VMEM((seq_group, 2, HEAD_PAIRS, HEAD_DIM, LANE), jnp.float32)],
        compiler_params=pltpu.CompilerParams(
            dimension_semantics=("arbitrary", "arbitrary"), vmem_limit_bytes=VMEM_LIMIT_BYTES),
        name="rwkv_scan",
    )(r, kk, v, w0, k0, nb0, r, kk, v, w1, k1, nb1, s0, pair_ones, eye)


def _rwkv_post_kernel(yf_ref, yb_ref, g_ref, bonus_ref, lng_ref, lnb_ref, ones_ref, o_ref):
    ones = ones_ref[...]
    y = yf_ref[...] + yb_ref[...]
    mean = jnp.dot(y, ones, preferred_element_type=jnp.float32, precision=_HI) * (1.0 / HEAD_DIM)
    d = y - mean
    var = jnp.dot(d * d, ones, preferred_element_type=jnp.float32, precision=_HI) * (1.0 / HEAD_DIM)
    y = d * lax.rsqrt(var + GN_EPS) * lng_ref[...] + lnb_ref[...]
    o_ref[...] = (y + bonus_ref[...]) * g_ref[...]


def _rwkv_post(yf, yb, g, bonus, lnx_g, lnx_b):
    n = yf.shape[0]
    row = pl.BlockSpec((PREP_TOK, B_WIDTH), lambda i: (i, 0))
    vec = pl.BlockSpec((1, B_WIDTH), lambda i: (0, 0))
    ones = _head_ones(B_WIDTH)
    return pl.pallas_call(
        _rwkv_post_kernel,
        grid=(n // PREP_TOK,),
        in_specs=[row, row, row, row, vec, vec, pl.BlockSpec(ones.shape, lambda i: (0, 0))],
        out_specs=row,
        out_shape=jax.ShapeDtypeStruct((n, B_WIDTH), jnp.float32),
        compiler_params=pltpu.CompilerParams(
            dimension_semantics=("arbitrary",), vmem_limit_bytes=VMEM_LIMIT_BYTES),
        name="rwkv_post",
    )(yf, yb, g, bonus, lnx_g, lnx_b, ones)


def _state_to_pairs(s):
    b = s.shape[0]
    s = s.reshape(b, 2, HEAD_PAIRS, 2, HEAD_DIM, HEAD_DIM)
    return jnp.transpose(s, (0, 1, 2, 4, 3, 5)).reshape(b, 2, HEAD_PAIRS, HEAD_DIM, 2 * HEAD_DIM)


def _state_from_pairs(s):
    b = s.shape[0]
    s = s.reshape(b, 2, HEAD_PAIRS, HEAD_DIM, 2, HEAD_DIM)
    return jnp.transpose(s, (0, 1, 2, 4, 3, 5)).reshape(b, 2, B_HEADS, HEAD_DIM, HEAD_DIM)


def _rwkv_mixer(p, lp, state_lat):
    zeros = jnp.zeros((LORA_W, B_WIDTH), jnp.float32)
    w2_pad = jnp.concatenate([jnp.concatenate([lp["rw_w2"][0], zeros], axis=1),
                              jnp.concatenate([zeros, lp["rw_w2"][1]], axis=1)], axis=0)
    a2_pad = jnp.concatenate([jnp.concatenate([lp["rw_a2"][0], zeros], axis=1),
                              jnp.concatenate([zeros, lp["rw_a2"][1]], axis=1)], axis=0)
    row = lambda a: a.reshape(1, -1)
    prep = _rwkv_prep(p, row(lp["rw_mu"]), lp["rw_w0"], w2_pad, lp["rw_a0"], a2_pad, lp["rw_g2"],
                      row(lp["rw_kk"]), row(lp["rw_ka"]), row(lp["rw_rk"]))
    scan_in, (g, bonus) = prep[:9], prep[9:]
    ctx_in = [a[:N_CTX_TOK].reshape(BATCH, SEQ, B_WIDTH) for a in scan_in]
    lat_in = [a[N_CTX_TOK:].reshape(DEC_BATCH, DEC_SEQ, B_WIDTH) for a in scan_in]
    s0_ctx = jnp.zeros((BATCH, 2, HEAD_PAIRS, HEAD_DIM, LANE), jnp.float32)
    yf_c, yb_c, s_ctx = _rwkv_scan(ctx_in, s0_ctx, seq_group=4)
    yf_l, yb_l, _ = _rwkv_scan(lat_in, _state_to_pairs(state_lat), seq_group=DEC_BATCH)
    flat = lambda u, v: jnp.concatenate([u.reshape(N_CTX_TOK, B_WIDTH), v.reshape(N_LAT_TOK, B_WIDTH)], axis=0)
    y = _rwkv_post(flat(yf_c, yf_l), flat(yb_c, yb_l), g, bonus, row(lp["rw_lnx_g"]), row(lp["rw_lnx_b"]))
    return y, _state_from_pairs(s_ctx)


def _rmsnorm(x, g):
    y = x * lax.rsqrt(jnp.mean(x * x, axis=-1, keepdims=True) + RMS_EPS)
    return y * g


def _axial_rope(x):
    T = x.shape[1]
    quarter = x.shape[-1] // 4
    t = jnp.arange(T)
    pos = jnp.stack([t // GRID_W, t % GRID_W], axis=-1).astype(jnp.float32)
    freqs = ROPE_BASE ** (-jnp.arange(quarter, dtype=jnp.float32) / quarter)
    ang = pos[:, :, None] * freqs
    bshape = (T,) + (1,) * (x.ndim - 3) + (2, quarter)
    cos = jnp.cos(ang).reshape(bshape).astype(x.dtype)
    sin = jnp.sin(ang).reshape(bshape).astype(x.dtype)
    xr = x.reshape(x.shape[:-1] + (2, 2, quarter))
    x1 = xr[..., 0, :]
    x2 = xr[..., 1, :]
    out = jnp.stack([x1 * cos - x2 * sin, x2 * cos + x1 * sin], axis=-2)
    return out.reshape(x.shape)


def _dense_attention(q, k, v, sink):
    B, S, KV, G, D = q.shape
    nb = S // Q_BLOCK
    scale = D ** -0.5
    vf = v.astype(jnp.float32)
    qb = jnp.moveaxis(q.reshape(B, nb, Q_BLOCK, KV, G, D), 1, 0)

    def block(qi):
        s = jnp.einsum("bqkgd,bskd->bkgqs", qi, k).astype(jnp.float32) * scale
        m = jnp.max(s, axis=-1)
        if sink is not None:
            sk = sink.astype(jnp.float32)[None, :, :, None]
            m = jnp.maximum(m, sk)
        p = jnp.exp(s - m[..., None])
        den = jnp.sum(p, axis=-1)
        if sink is not None:
            den = den + jnp.exp(sk - m)
        o = jnp.einsum("bkgqs,bskd->bqkgd", p, vf)
        return o / jnp.moveaxis(den, -1, 1)[..., None]

    o = lax.map(block, qb)
    return jnp.moveaxis(o, 0, 1).reshape(B, S, KV * G * D).astype(q.dtype)


def _window_attention(q, k, v, ck, cv, sink):
    B, T, KV, G, D = q.shape
    nb = T // Q_BLOCK
    span = Q_BLOCK + 2 * WINDOW
    scale = D ** -0.5
    pad = ((0, 0), (WINDOW, WINDOW), (0, 0), (0, 0))
    kp = jnp.pad(k, pad)
    vp = jnp.pad(v, pad)
    idx = jnp.arange(nb)[:, None] * Q_BLOCK + jnp.arange(span)[None, :]
    kb = kp[:, idx]
    vb = vp[:, idx].astype(jnp.float32)
    qb = q.reshape(B, nb, Q_BLOCK, KV, G, D)
    s = jnp.einsum("bnqkgd,bnskd->bnkgqs", qb, kb).astype(jnp.float32) * scale
    qpos = jnp.arange(nb)[:, None] * Q_BLOCK + jnp.arange(Q_BLOCK)[None, :]
    kpos = idx - WINDOW
    valid = ((jnp.abs(qpos[:, :, None] - kpos[:, None, :]) <= WINDOW)
             & (kpos >= 0)[:, None, :] & (kpos < T)[:, None, :])
    s = jnp.where(valid[None, :, None, None], s, -jnp.inf)
    sc = jnp.einsum("bnqkgd,bpkd->bnkgqp", qb, ck).astype(jnp.float32) * scale
    sk = sink.astype(jnp.float32)[None, None, :, :, None]
    m = jnp.maximum(jnp.maximum(jnp.max(s, axis=-1), jnp.max(sc, axis=-1)), sk)
    ps = jnp.exp(s - m[..., None])
    pc = jnp.exp(sc - m[..., None])
    den = jnp.sum(ps, axis=-1) + jnp.sum(pc, axis=-1) + jnp.exp(sk - m)
    o = (jnp.einsum("bnkgqs,bnskd->bnqkgd", ps, vb)
         + jnp.einsum("bnkgqp,bpkd->bnqkgd", pc, cv.astype(jnp.float32)))
    o = o / jnp.moveaxis(den, -1, 2)[..., None]
    return o.reshape(B, T, KV * G * D).astype(q.dtype)


def _neighbourhood_attention(q, k, v, ck, cv, rpb):
    B, T, H, D = q.shape
    rows = T // GRID_W
    wr = min(NA_ROWS, rows)
    scale = D ** -0.5
    qg = q.reshape(B, rows, GRID_W, H, D)
    kg = k.reshape(B, rows, GRID_W, H, D)
    vg = v.reshape(B, rows, GRID_W, H, D)
    r = jnp.arange(rows)
    rs = jnp.clip(r - wr // 2, 0, rows - wr)
    row_idx = rs[:, None] + jnp.arange(wr)[None, :]
    k_rows = kg[:, row_idx]
    v_rows = vg[:, row_idx].astype(jnp.float32)
    col = jnp.arange(GRID_W)
    cs = jnp.clip(col - NA_COLS // 2, 0, GRID_W - NA_COLS)
    col_ok = (col[None, :] >= cs[:, None]) & (col[None, :] < cs[:, None] + NA_COLS)
    dr = row_idx - r[:, None] + NA_ROWS - 1
    dc = jnp.clip(col[None, :] - col[:, None], -(NA_COLS - 1), NA_COLS - 1) + NA_COLS - 1
    bias = rpb[:, dr[:, None, :, None], dc[None, :, None, :]].astype(jnp.float32)
    s = jnp.einsum("brchd,brawhd->bhrcaw", qg, k_rows).astype(jnp.float32) * scale + bias[None]
    s = jnp.where(col_ok[None, None, None, :, None, :], s, -jnp.inf)
    sc = jnp.einsum("brchd,bphd->bhrcp", qg, ck).astype(jnp.float32) * scale
    m = jnp.maximum(jnp.max(s, axis=(-2, -1)), jnp.max(sc, axis=-1))
    pn = jnp.exp(s - m[..., None, None])
    pc = jnp.exp(sc - m[..., None])
    den = jnp.sum(pn, axis=(-2, -1)) + jnp.sum(pc, axis=-1)
    o = (jnp.einsum("bhrcaw,brawhd->brchd", pn, v_rows)
         + jnp.einsum("bhrcp,bphd->brchd", pc, cv.astype(jnp.float32)))
    o = o / jnp.transpose(den, (0, 2, 3, 1))[..., None]
    return o.reshape(B, T, H * D).astype(q.dtype)


def _attention_mixers(p, lp, ctx):
    B, T, _ = p.shape
    pa, _, pc = jnp.split(p, [A_COLS, A_COLS + B_COLS], axis=-1)
    qa, ka, va = jnp.split(pa, [A_HEADS * HEAD_DIM, (A_HEADS + A_KV) * HEAD_DIM], axis=-1)
    qa = qa.reshape(B, T, A_KV, A_GROUP, HEAD_DIM)
    ka = ka.reshape(B, T, A_KV, HEAD_DIM)
    va = va.reshape(B, T, A_KV, HEAD_DIM)
    qc, kc, vc = jnp.split(pc, 3, axis=-1)
    qc = qc.reshape(B, T, C_HEADS, HEAD_DIM)
    kc = kc.reshape(B, T, C_HEADS, HEAD_DIM)
    vc = vc.reshape(B, T, C_HEADS, HEAD_DIM)
    sink = lp["a_sink"].reshape(A_KV, A_GROUP)
    if ctx is None:
        ya = _dense_attention(qa, ka, va, sink)
        yc = _dense_attention(qc[:, :, :, None, :], kc, vc, None)
    else:
        ctx_ak, ctx_av, ctx_ck, ctx_cv = ctx
        ya = _window_attention(_axial_rope(qa), _axial_rope(ka), va, ctx_ak, ctx_av, sink)
        yc = _neighbourhood_attention(qc, kc, vc, ctx_ck, ctx_cv, lp["na_rpb"])
    return ya, yc, (ka, va, kc, vc)


def kernel(x_prompt, x_sample, cache_a_k, cache_a_v, cache_c_k, cache_c_v, state_rwkv, c, c_ctx, ln1_g, ln2_g, lnf_g, ada_w, ada_b, w_in, a_sink, a_out, rw_mu, rw_w0, rw_w2, rw_a0, rw_a2, rw_g2, rw_kk, rw_ka, rw_rk, rw_lnx_g, rw_lnx_b, rw_out, na_rpb, na_out, w_o, pe_q, pe_subkeys, pe_u, pe_v):
    bf16 = jnp.bfloat16
    x = jnp.concatenate([x_prompt.reshape(N_CTX_TOK, D_MODEL), x_sample.reshape(N_LAT_TOK, D_MODEL)], axis=0)
    silu_all = jnp.concatenate([jax.nn.silu(c_ctx)[None, :], jax.nn.silu(c)], axis=0)
    n_mix = A_COLS + B_COLS + C_COLS
    list_ak, list_av, list_ck, list_cv, list_st = [], [], [], [], []
    for l in range(DEPTH):
        lp = {
            "a_sink": a_sink[l], "rw_mu": rw_mu[l], "rw_w0": rw_w0[l], "rw_w2": rw_w2[l], "rw_a0": rw_a0[l],
            "rw_a2": rw_a2[l], "rw_g2": rw_g2[l], "rw_kk": rw_kk[l], "rw_ka": rw_ka[l],
            "rw_rk": rw_rk[l], "rw_lnx_g": rw_lnx_g[l], "rw_lnx_b": rw_lnx_b[l], "na_rpb": na_rpb[l],
        }
        mod = (silu_all @ ada_w[l] + ada_b[l]).reshape(N_SEG, N_MOD, D_MODEL)

        w_in_l = jnp.concatenate([w_in[l][:, n_mix:], w_in[l][:, :n_mix]], axis=1).astype(bf16)
        p, _ = _norm_proj(x, ln1_g[l][None, :], mod, w_in_l, 0, 1, col_tile=IN_COLS // 3)
        pm_ctx = p[:N_CTX_TOK, N_BRANCH * D_MODEL:].reshape(BATCH, SEQ, n_mix)
        pm_lat = p[N_CTX_TOK:, N_BRANCH * D_MODEL:].reshape(DEC_BATCH, DEC_SEQ, n_mix)
        ya_c, yc_c, (ak, av, ck, cv) = _attention_mixers(pm_ctx, lp, None)
        ctx = (cache_a_k[:, l], cache_a_v[:, l], cache_c_k[:, l], cache_c_v[:, l])
        ya_l, yc_l, _ = _attention_mixers(pm_lat, lp, ctx)
        yb, st = _rwkv_mixer(p, lp, state_rwkv[:, l])
        list_ak.append(ak)
        list_av.append(av)
        list_ck.append(ck)
        list_cv.append(cv)
        list_st.append(st)
        cat = lambda u, v: jnp.concatenate([u.reshape(N_CTX_TOK, -1), v.reshape(N_LAT_TOK, -1)], axis=0)
        x = _merge(x, cat(ya_c, ya_l), yb, cat(yc_c, yc_l), p, mod,
                   a_out[l].astype(bf16), rw_out[l].astype(bf16), na_out[l].astype(bf16), w_o[l].astype(bf16))

        q, h2 = _norm_proj(x, ln2_g[l][None, :], mod, pe_q[l].astype(bf16), 3, 4,
                           col_tile=PEER_HEADS * PEER_KEY_DIM)
        s1m, s2m, thr, c0 = _peer_route(q, pe_subkeys[l])
        x = _peer_dense(h2, x, mod, s1m, s2m, thr, c0, pe_u[l].astype(bf16), pe_v[l].T.astype(bf16))

    y = _rmsnorm(x, lnf_g)
    y_prompt = y[:N_CTX_TOK].reshape(BATCH, SEQ, D_MODEL)
    y_sample = y[N_CTX_TOK:].reshape(DEC_BATCH, DEC_SEQ, D_MODEL)
    return (y_prompt, y_sample, jnp.stack(list_ak, axis=1), jnp.stack(list_av, axis=1),
            jnp.stack(list_ck, axis=1), jnp.stack(list_cv, axis=1), jnp.stack(list_st, axis=1))
```

```python
import functools

import jax
import jax.numpy as jnp
from jax import lax
from jax.experimental import pallas as pl
from jax.experimental.pallas import tpu as pltpu

D_MODEL = 1024
BATCH = 16
SEQ = 256
DEPTH = 4
DEC_BATCH = 2
DEC_SEQ = 2048
PAST_LEN = 512
GRID_W = 64
HEAD_DIM = 64
A_HEADS = 8
A_KV = 2
A_GROUP = A_HEADS // A_KV
WINDOW = 128
Q_BLOCK = 128
B_HEADS = 8
B_WIDTH = B_HEADS * HEAD_DIM
LORA_W = 64
LORA_A = 64
LORA_G = 128
GN_EPS = 64e-5
C_HEADS = 8
NA_ROWS = 8
NA_COLS = 16
N_BRANCH = 3
A_COLS = (A_HEADS + 2 * A_KV) * HEAD_DIM
B_COLS = 3 * B_WIDTH + 2 * LORA_W + 2 * LORA_A + LORA_G
C_COLS = 3 * C_HEADS * HEAD_DIM
IN_COLS = A_COLS + B_COLS + C_COLS + N_BRANCH * D_MODEL
PEER_HEADS = 8
N_KEYS = 128
N_EXPERTS = N_KEYS * N_KEYS
PEER_KEY_DIM = 256
PEER_TOPK = 16
PEER_CHUNK = 128
ROPE_BASE = 10000.0
RMS_EPS = 1e-6
N_MOD = 6

N_CTX_TOK = BATCH * SEQ
N_LAT_TOK = DEC_BATCH * DEC_SEQ
N_TOK = N_CTX_TOK + N_LAT_TOK
N_SEG = 1 + DEC_BATCH

VMEM_LIMIT_BYTES = 48 * 1024 * 1024

ROW_TILE = 512
assert N_CTX_TOK % ROW_TILE == 0 and DEC_SEQ % ROW_TILE == 0


def _segment_of_row_block(i):
    n_ctx_blocks = N_CTX_TOK // ROW_TILE
    blocks_per_lat = DEC_SEQ // ROW_TILE
    return jnp.where(i < n_ctx_blocks, 0, 1 + (i - n_ctx_blocks) // blocks_per_lat)


def _norm_proj_kernel(x_ref, g_ref, mod_ref, w_ref, o_ref, h_ref, *, shift_idx, scale_idx):
    @pl.when(pl.program_id(1) == 0)
    def _():
        x = x_ref[...]
        y = x * lax.rsqrt(jnp.mean(x * x, axis=-1, keepdims=True) + RMS_EPS) * g_ref[...]
        h = y * (1.0 + mod_ref[0, scale_idx:scale_idx + 1, :]) + mod_ref[0, shift_idx:shift_idx + 1, :]
        h_ref[...] = h.astype(jnp.bfloat16)

    o_ref[...] = jnp.dot(h_ref[...], w_ref[...], preferred_element_type=jnp.float32)


def _norm_proj(x, g, mod, w_bf16, shift_idx, scale_idx, col_tile):
    n_rows, d = x.shape
    n_cols = w_bf16.shape[1]
    assert n_cols % col_tile == 0
    return pl.pallas_call(
        functools.partial(_norm_proj_kernel, shift_idx=shift_idx, scale_idx=scale_idx),
        grid=(n_rows // ROW_TILE, n_cols // col_tile),
        in_specs=[
            pl.BlockSpec((ROW_TILE, d), lambda i, j: (i, 0)),
            pl.BlockSpec((1, d), lambda i, j: (0, 0)),
            pl.BlockSpec((1, N_MOD, d), lambda i, j: (_segment_of_row_block(i), 0, 0)),
            pl.BlockSpec((d, col_tile), lambda i, j: (0, j)),
        ],
        out_specs=[pl.BlockSpec((ROW_TILE, col_tile), lambda i, j: (i, j)),
                   pl.BlockSpec((ROW_TILE, d), lambda i, j: (i, 0))],
        out_shape=[jax.ShapeDtypeStruct((n_rows, n_cols), jnp.float32),
                   jax.ShapeDtypeStruct((n_rows, d), jnp.bfloat16)],
        compiler_params=pltpu.CompilerParams(
            dimension_semantics=("arbitrary", "arbitrary"), vmem_limit_bytes=VMEM_LIMIT_BYTES),
        name="norm_proj",
    )(x, g, mod, w_bf16)


LANE = 128
ROUTE_TOK = 512
PEER_TOK = 512
PEER_EXP = 1024
NEG_INF = float("-inf")


def _take_top(vals, k):
    rows = lax.broadcasted_iota(jnp.int32, vals.shape, 0)
    taken = []
    for _ in range(k):
        m = jnp.max(vals, axis=0, keepdims=True)
        first = jnp.min(jnp.where(vals == m, rows, vals.shape[0]), axis=0, keepdims=True)
        vals = jnp.where(rows == first, NEG_INF, vals)
        taken.append(m)
    return jnp.concatenate(taken, axis=0), vals


def _peer_route_kernel(q_ref, sk_ref, s1m_ref, s2m_ref, thr_ref, c0_ref):
    def sub_tile(t, carry):
        tok = pl.ds(pl.multiple_of(t * LANE, LANE), LANE)
        masked, top_vals = [], []
        for z in range(2):
            qz = q_ref[tok, z * LANE:(z + 1) * LANE]
            s = lax.dot_general(sk_ref[0, z], qz, (((1,), (1,)), ((), ())),
                                preferred_element_type=jnp.float32)
            v, rest = _take_top(s, PEER_TOPK)
            masked.append(jnp.where(rest == NEG_INF, s, NEG_INF))
            top_vals.append(v)
        cand = jnp.concatenate([top_vals[0][i:i + 1] + top_vals[1] for i in range(PEER_TOPK)], axis=0)
        sv, _ = _take_top(cand, PEER_TOPK)
        z_sum = jnp.sum(jnp.exp(sv - sv[0:1]), axis=0, keepdims=True)
        s1m_ref[0, :, tok] = masked[0]
        s2m_ref[0, :, tok] = masked[1]
        thr_ref[0, :, tok] = sv[PEER_TOPK - 1:PEER_TOPK]
        c0_ref[0, :, tok] = -sv[0:1] - jnp.log(z_sum)
        return carry

    lax.fori_loop(0, ROUTE_TOK // LANE, sub_tile, 0)


def _peer_route(q, subkeys):
    n = q.shape[0]
    big = pl.BlockSpec((1, N_KEYS, ROUTE_TOK), lambda i, h: (h, 0, i))
    small = pl.BlockSpec((1, 1, ROUTE_TOK), lambda i, h: (h, 0, i))
    return pl.pallas_call(
        _peer_route_kernel,
        grid=(n // ROUTE_TOK, PEER_HEADS),
        in_specs=[
            pl.BlockSpec((ROUTE_TOK, PEER_KEY_DIM), lambda i, h: (i, h)),
            pl.BlockSpec((1, 2, N_KEYS, PEER_KEY_DIM // 2), lambda i, h: (h, 0, 0, 0)),
        ],
        out_specs=[big, big, small, small],
        out_shape=[jax.ShapeDtypeStruct((PEER_HEADS, N_KEYS, n), jnp.float32)] * 2
        + [jax.ShapeDtypeStruct((PEER_HEADS, 1, n), jnp.float32)] * 2,
        compiler_params=pltpu.CompilerParams(
            dimension_semantics=("arbitrary", "arbitrary"), vmem_limit_bytes=VMEM_LIMIT_BYTES),
        name="peer_route",
    )(q, subkeys)


def _gelu_tanh(x):
    return 0.5 * x * (1.0 + jnp.tanh(0.7978845608028654 * (x + 0.044715 * (x * x * x))))


def _peer_dense_kernel(h_ref, x_ref, mod_ref, s1m_ref, s2m_ref, thr_ref, c0_ref, u_ref, vt_ref, o_ref,
                       ht_scr, s2c_scr, coef_scr, acc_scr, *, gate_idx):
    e = pl.program_id(1)

    @pl.when(e == 0)
    def _():
        ht_scr[...] = h_ref[...].astype(jnp.float32).T.astype(jnp.bfloat16)
        s2c_scr[...] = s2m_ref[...] + c0_ref[...]
        acc_scr[...] = jnp.zeros_like(acc_scr)

    act = jnp.dot(u_ref[...], ht_scr[...], preferred_element_type=jnp.float32)
    a_per_step = PEER_EXP // N_KEYS
    a_rows = pl.ds(pl.multiple_of(e * a_per_step, a_per_step), a_per_step)
    for t in range(PEER_TOK // LANE):
        tok = slice(t * LANE, (t + 1) * LANE)
        s1_all = [s1m_ref[hh, a_rows, tok] for hh in range(PEER_HEADS)]
        for a in range(a_per_step):
            rows = slice(a * N_KEYS, (a + 1) * N_KEYS)
            w = jnp.zeros((N_KEYS, LANE), jnp.float32)
            for hh in range(PEER_HEADS):
                s1 = s1_all[hh][a:a + 1]
                picked = (s1 + s2m_ref[hh, :, tok]) >= thr_ref[hh, :, tok]
                w = w + jnp.where(picked, jnp.exp(s1 + s2c_scr[hh, :, tok]), 0.0)
            coef_scr[rows, tok] = (w * _gelu_tanh(act[rows, tok])).astype(jnp.bfloat16)
    acc_scr[...] += jnp.dot(vt_ref[...], coef_scr[...], preferred_element_type=jnp.float32)

    @pl.when(e == pl.num_programs(1) - 1)
    def _():
        o_ref[...] = x_ref[...] + mod_ref[0, gate_idx:gate_idx + 1, :] * acc_scr[...].T


def _peer_dense(h, x, mod, s1m, s2m, thr, c0, u_bf16, vt_bf16):
    n, d = x.shape
    assert PEER_TOK == ROW_TILE
    tok_row = lambda i, e: (i, 0)
    big = pl.BlockSpec((PEER_HEADS, N_KEYS, PEER_TOK), lambda i, e: (0, 0, i))
    small = pl.BlockSpec((PEER_HEADS, 1, PEER_TOK), lambda i, e: (0, 0, i))
    return pl.pallas_call(
        functools.partial(_peer_dense_kernel, gate_idx=5),
        grid=(n // PEER_TOK, N_EXPERTS // PEER_EXP),
        in_specs=[
            pl.BlockSpec((PEER_TOK, d), tok_row),
            pl.BlockSpec((PEER_TOK, d), tok_row),
            pl.BlockSpec((1, N_MOD, d), lambda i, e: (_segment_of_row_block(i), 0, 0)),
            big, big, small, small,
            pl.BlockSpec((PEER_EXP, d), lambda i, e: (e, 0)),
            pl.BlockSpec((d, PEER_EXP), lambda i, e: (0, e)),
        ],
        out_specs=pl.BlockSpec((PEER_TOK, d), tok_row),
        out_shape=jax.ShapeDtypeStruct((n, d), jnp.float32),
        scratch_shapes=[
            pltpu.VMEM((d, PEER_TOK), jnp.bfloat16),
            pltpu.VMEM((PEER_HEADS, N_KEYS, PEER_TOK), jnp.float32),
            pltpu.VMEM((PEER_EXP, PEER_TOK), jnp.bfloat16),
            pltpu.VMEM((d, PEER_TOK), jnp.float32),
        ],
        compiler_params=pltpu.CompilerParams(
            dimension_semantics=("arbitrary", "arbitrary"), vmem_limit_bytes=VMEM_LIMIT_BYTES),
        name="peer_dense",
    )(h, x, mod, s1m, s2m, thr, c0, u_bf16, vt_bf16)


def _merge_kernel(x_ref, ya_ref, yb_ref, yc_ref, ga_ref, gb_ref, gc_ref, mod_ref,
                  wa_ref, wb_ref, wc_ref, wo_ref, o_ref, *, gate_idx):
    def branch(y_ref, gate_ref, w_ref):
        proj = jnp.dot(y_ref[...].astype(jnp.bfloat16), w_ref[...], preferred_element_type=jnp.float32)
        return jax.nn.sigmoid(gate_ref[...]) * proj

    merged = branch(ya_ref, ga_ref, wa_ref) + branch(yb_ref, gb_ref, wb_ref) + branch(yc_ref, gc_ref, wc_ref)
    out = jnp.dot(merged.astype(jnp.bfloat16), wo_ref[...], preferred_element_type=jnp.float32)
    o_ref[...] = x_ref[...] + mod_ref[0, gate_idx:gate_idx + 1, :] * out


def _merge(x, ya, yb, yc, p, mod, wa, wb, wc, wo):
    n_rows, d = x.shape
    gate_col0 = 0
    row = lambda i: (i, 0)
    const = lambda i: (0, 0)
    y_spec = pl.BlockSpec((ROW_TILE, ya.shape[1]), row)
    w_spec = pl.BlockSpec(wa.shape, const)
    return pl.pallas_call(
        functools.partial(_merge_kernel, gate_idx=2),
        grid=(n_rows // ROW_TILE,),
        in_specs=[
            pl.BlockSpec((ROW_TILE, d), row), y_spec, y_spec, y_spec,
            pl.BlockSpec((ROW_TILE, d), lambda i: (i, gate_col0)),
            pl.BlockSpec((ROW_TILE, d), lambda i: (i, gate_col0 + 1)),
            pl.BlockSpec((ROW_TILE, d), lambda i: (i, gate_col0 + 2)),
            pl.BlockSpec((1, N_MOD, d), lambda i: (_segment_of_row_block(i), 0, 0)),
            w_spec, w_spec, w_spec, pl.BlockSpec(wo.shape, const),
        ],
        out_specs=pl.BlockSpec((ROW_TILE, d), row),
        out_shape=jax.ShapeDtypeStruct((n_rows, d), jnp.float32),
        compiler_params=pltpu.CompilerParams(
            dimension_semantics=("arbitrary",), vmem_limit_bytes=VMEM_LIMIT_BYTES),
        name="merge_out_proj",
    )(x, ya, yb, yc, p, p, p, mod, wa, wb, wc, wo)


HEAD_PAIRS = B_HEADS // 2
PREP_TOK = 256
SCAN_CHUNK = 128
assert PREP_TOK == SEQ and DEC_SEQ % PREP_TOK == 0 and 2 * HEAD_DIM == LANE
_HI = lax.Precision.HIGHEST


def _head_ones(n):
    r = lax.broadcasted_iota(jnp.int32, (n, n), 0) // HEAD_DIM
    c = lax.broadcasted_iota(jnp.int32, (n, n), 1) // HEAD_DIM
    return (r == c).astype(jnp.float32)


def _softplus(y):
    return jnp.maximum(y, 0.0) + jnp.log(1.0 + jnp.exp(-jnp.abs(y)))


def _rwkv_prep_kernel(pb_ref, prev_ref, next_ref, mu_ref, w0_ref, w2_ref, a0_ref, a2_ref, g2_ref,
                      kks_ref, ka_ref, rk_ref, ones_ref,
                      r_o, kk_o, v_o, w0_o, w1_o, k0_o, k1_o, nb0_o, nb1_o, g_o, bonus_o):
    i = pl.program_id(0)
    n_ctx = N_CTX_TOK // PREP_TOK
    per_lat = DEC_SEQ // PREP_TOK
    j = (i - n_ctx) % per_lat
    starts_seq = jnp.logical_or(i < n_ctx, j == 0)
    ends_seq = jnp.logical_or(i < n_ctx, j == per_lat - 1)

    pb = pb_ref[...]
    rows = lax.broadcasted_iota(jnp.int32, (PREP_TOK, 1), 0)
    before = jnp.where(starts_seq, 0.0, prev_ref[7:8, :])
    after = jnp.where(ends_seq, 0.0, next_ref[0:1, :])
    prev = jnp.where(rows == 0, before, pltpu.roll(pb, 1, 0))
    nxt = jnp.where(rows == PREP_TOK - 1, after, pltpu.roll(pb, PREP_TOK - 1, 0))
    xb = pb + mu_ref[...] * (0.5 * (prev + nxt) - pb)

    w = B_WIDTH
    r, k, v = xb[:, 0:w], xb[:, w:2 * w], xb[:, 2 * w:3 * w]
    wd = xb[:, 3 * w:3 * w + 2 * LORA_W]
    ad = xb[:, 3 * w + 2 * LORA_W:3 * w + 2 * LORA_W + 2 * LORA_A]
    gd = xb[:, 3 * w + 2 * LORA_W + 2 * LORA_A:]
    ones = ones_ref[...]

    def head_sum(x):
        return jnp.dot(x, ones, preferred_element_type=jnp.float32, precision=_HI)

    lw = jnp.dot(jnp.tanh(wd), w2_ref[...], preferred_element_type=jnp.float32, precision=_HI)
    la = jnp.dot(ad, a2_ref[...], preferred_element_type=jnp.float32, precision=_HI)
    kk = k * kks_ref[...]
    kk = kk * lax.rsqrt(head_sum(kk * kk) + 1e-12)
    kd_sum = jnp.zeros_like(k)
    for z, (w_o, k_o, nb_o) in enumerate(((w0_o, k0_o, nb0_o), (w1_o, k1_o, nb1_o))):
        wlog = -_softplus(-(w0_ref[z:z + 1, :] + lw[:, z * w:(z + 1) * w])) - 0.5
        a = jax.nn.sigmoid(a0_ref[z:z + 1, :] + la[:, z * w:(z + 1) * w])
        kd = k * (1.0 + (a - 1.0) * ka_ref[...])
        kd_sum = kd_sum + kd
        w_o[...] = jnp.exp(-jnp.exp(wlog))
        k_o[...] = kd
        nb_o[...] = -(kk * a)
    r_o[...] = r
    kk_o[...] = kk
    v_o[...] = v
    g_o[...] = jnp.dot(jax.nn.sigmoid(gd), g2_ref[...], preferred_element_type=jnp.float32, precision=_HI)
    bonus_o[...] = head_sum(r * kd_sum * rk_ref[...]) * v


def _rwkv_prep(p, mu, w0, w2_pad, a0, a2_pad, g2, kk_scale, ka, rk):
    n = p.shape[0]
    col_blk = (N_BRANCH * D_MODEL + A_COLS) // B_COLS
    assert col_blk * B_COLS == N_BRANCH * D_MODEL + A_COLS
    halo = 8
    per = PREP_TOK // halo
    full = lambda a: pl.BlockSpec(a.shape, lambda i: (0,) * a.ndim)
    ones = _head_ones(B_WIDTH)
    consts = (mu, w0, w2_pad, a0, a2_pad, g2, kk_scale, ka, rk, ones)
    out_spec = pl.BlockSpec((PREP_TOK, B_WIDTH), lambda i: (i, 0))
    return pl.pallas_call(
        _rwkv_prep_kernel,
        grid=(n // PREP_TOK,),
        in_specs=[
            pl.BlockSpec((PREP_TOK, B_COLS), lambda i: (i, col_blk)),
            pl.BlockSpec((halo, B_COLS), lambda i: (jnp.maximum(i * per - 1, 0), col_blk)),
            pl.BlockSpec((halo, B_COLS), lambda i: (jnp.minimum((i + 1) * per, n // halo - 1), col_blk)),
        ] + [full(a) for a in consts],
        out_specs=[out_spec] * 11,
        out_shape=[jax.ShapeDtypeStruct((n, B_WIDTH), jnp.float32)] * 11,
        compiler_params=pltpu.CompilerParams(
            dimension_semantics=("arbitrary",), vmem_limit_bytes=VMEM_LIMIT_BYTES),
        name="rwkv_prep",
    )(p, p, p, *consts)


SCAN_LANES = 2 * LANE
SCAN_PAIR_GROUP = 4


def _head_sums(head_ones, *terms):
    sums = []
    for x, passes in terms:
        hi = x.astype(jnp.bfloat16)
        out = jnp.dot(hi, head_ones, preferred_element_type=jnp.float32)
        if passes == 2:
            lo = (x - hi.astype(jnp.float32)).astype(jnp.bfloat16)
            out = out + jnp.dot(lo, head_ones, preferred_element_type=jnp.float32)
        sums.append(out)
    return sums


def _rwkv_scan_kernel(rf, kkf, vf, wf, kf, nbf, rb, kkb, vb, wb, kb, nbb, s0_ref, ones_ref, eye_ref,
                      yf_ref, yb_ref, sfin_ref, s_scr, *, n_seq):
    c = pl.program_id(1)

    @pl.when(c == 0)
    def _():
        s_scr[...] = s0_ref[...]

    head_ones = ones_ref[...]
    eye = eye_ref[...]
    fwd_refs = (rf, kkf, vf, wf, kf, nbf)
    bwd_refs = (rb, kkb, vb, wb, kb, nbb)
    R, KK, V, W, K, NB = range(6)

    def step(t, carry):
        t_f = pl.ds(t, 1)
        t_b = pl.ds(SCAN_CHUNK - 1 - t, 1)
        for g in range(n_seq):
            rows_f = [a[g, t_f, :] for a in fwd_refs]
            rows_b = [a[g, t_b, :] for a in bwd_refs]

            def both(which, p):
                lanes = slice(p * LANE, (p + 1) * LANE)
                return jnp.concatenate([rows_f[which][:, lanes], rows_b[which][:, lanes]], axis=1)

            y_rows = []
            for p0 in range(0, HEAD_PAIRS, SCAN_PAIR_GROUP):
                pairs = range(p0, p0 + SCAN_PAIR_GROUP)
                rows = {p: slice((p - p0) * HEAD_DIM, (p - p0 + 1) * HEAD_DIM) for p in pairs}
                s_old = {p: s_scr[g, p] for p in pairs}
                s_kk, v_col = _head_sums(
                    head_ones,
                    (jnp.concatenate([s_old[p] * both(KK, p) for p in pairs], axis=0), 2),
                    (jnp.concatenate([eye * both(V, p) for p in pairs], axis=0), 1))
                s_new = {}
                for p in pairs:
                    s = s_old[p] * both(W, p) + s_kk[rows[p]] * both(NB, p) + v_col[rows[p]] * both(K, p)
                    s_scr[g, p] = s
                    s_new[p] = s
                (y_all,) = _head_sums(
                    head_ones, (jnp.concatenate([s_new[p] * both(R, p) for p in pairs], axis=0), 1))
                y_rows += [jnp.sum(y_all[rows[p]] * eye, axis=0, keepdims=True) for p in pairs]
            yf_ref[g, t_f, :] = jnp.concatenate([y[:, :LANE] for y in y_rows], axis=1)
            yb_ref[g, t_b, :] = jnp.concatenate([y[:, LANE:] for y in y_rows], axis=1)
        return carry

    lax.fori_loop(0, SCAN_CHUNK, step, 0, unroll=2)

    @pl.when(c == pl.num_programs(1) - 1)
    def _():
        sfin_ref[...] = s_scr[...]


def _rwkv_scan(seqs, s0, seq_group):
    r, kk, v, w0, w1, k0, k1, nb0, nb1 = seqs
    n_seq, t_len, _ = r.shape
    n_chunks = t_len // SCAN_CHUNK
    assert n_seq % seq_group == 0 and n_chunks * SCAN_CHUNK == t_len
    blk = (seq_group, SCAN_CHUNK, B_WIDTH)
    fwd = pl.BlockSpec(blk, lambda i, c: (i, c, 0))
    bwd = pl.BlockSpec(blk, lambda i, c: (i, n_chunks - 1 - c, 0))
    state_blk = (seq_group, HEAD_PAIRS, HEAD_DIM, SCAN_LANES)
    state = pl.BlockSpec(state_blk, lambda i, c: (i, 0, 0, 0))
    head_ones = _head_ones(SCAN_LANES).astype(jnp.bfloat16)
    eye = (lax.broadcasted_iota(jnp.int32, (HEAD_DIM, SCAN_LANES), 0)
           == lax.broadcasted_iota(jnp.int32, (HEAD_DIM, SCAN_LANES), 1) % HEAD_DIM).astype(jnp.float32)
    const = lambda a: pl.BlockSpec(a.shape, lambda i, c: (0, 0))
    y_shape = jax.ShapeDtypeStruct((n_seq, t_len, B_WIDTH), jnp.float32)
    return pl.pallas_call(
        functools.partial(_rwkv_scan_kernel, n_seq=seq_group),
        grid=(n_seq // seq_group, n_chunks),
        in_specs=[fwd] * 6 + [bwd] * 6 + [state, const(head_ones), const(eye)],
        out_specs=[fwd, bwd, state],
        out_shape=[y_shape, y_shape, jax.ShapeDtypeStruct(s0.shape, jnp.float32)],
        scratch_shapes=[pltpu.VMEM(state_blk, jnp.float32)],
        compiler_params=pltpu.CompilerParams(
            dimension_semantics=("arbitrary", "arbitrary"), vmem_limit_bytes=VMEM_LIMIT_BYTES),
        name="rwkv_scan",
    )(r, kk, v, w0, k0, nb0, r, kk, v, w1, k1, nb1, s0, head_ones, eye)


def _rwkv_post_kernel(yf_ref, yb_ref, g_ref, bonus_ref, lng_ref, lnb_ref, ones_ref, o_ref):
    ones = ones_ref[...]
    y = yf_ref[...] + yb_ref[...]
    mean = jnp.dot(y, ones, preferred_element_type=jnp.float32, precision=_HI) * (1.0 / HEAD_DIM)
    d = y - mean
    var = jnp.dot(d * d, ones, preferred_element_type=jnp.float32, precision=_HI) * (1.0 / HEAD_DIM)
    y = d * lax.rsqrt(var + GN_EPS) * lng_ref[...] + lnb_ref[...]
    o_ref[...] = (y + bonus_ref[...]) * g_ref[...]


def _rwkv_post(yf, yb, g, bonus, lnx_g, lnx_b):
    n = yf.shape[0]
    row = pl.BlockSpec((PREP_TOK, B_WIDTH), lambda i: (i, 0))
    vec = pl.BlockSpec((1, B_WIDTH), lambda i: (0, 0))
    ones = _head_ones(B_WIDTH)
    return pl.pallas_call(
        _rwkv_post_kernel,
        grid=(n // PREP_TOK,),
        in_specs=[row, row, row, row, vec, vec, pl.BlockSpec(ones.shape, lambda i: (0, 0))],
        out_specs=row,
        out_shape=jax.ShapeDtypeStruct((n, B_WIDTH), jnp.float32),
        compiler_params=pltpu.CompilerParams(
            dimension_semantics=("arbitrary",), vmem_limit_bytes=VMEM_LIMIT_BYTES),
        name="rwkv_post",
    )(yf, yb, g, bonus, lnx_g, lnx_b, ones)


def _state_to_pairs(s):
    b = s.shape[0]
    s = s.reshape(b, 2, HEAD_PAIRS, 2, HEAD_DIM, HEAD_DIM)
    return jnp.transpose(s, (0, 2, 4, 1, 3, 5)).reshape(b, HEAD_PAIRS, HEAD_DIM, SCAN_LANES)


def _state_from_pairs(s):
    b = s.shape[0]
    s = s.reshape(b, HEAD_PAIRS, HEAD_DIM, 2, 2, HEAD_DIM)
    return jnp.transpose(s, (0, 3, 1, 4, 2, 5)).reshape(b, 2, B_HEADS, HEAD_DIM, HEAD_DIM)


def _rwkv_mixer(p, lp, state_lat):
    zeros = jnp.zeros((LORA_W, B_WIDTH), jnp.float32)
    w2_pad = jnp.concatenate([jnp.concatenate([lp["rw_w2"][0], zeros], axis=1),
                              jnp.concatenate([zeros, lp["rw_w2"][1]], axis=1)], axis=0)
    a2_pad = jnp.concatenate([jnp.concatenate([lp["rw_a2"][0], zeros], axis=1),
                              jnp.concatenate([zeros, lp["rw_a2"][1]], axis=1)], axis=0)
    row = lambda a: a.reshape(1, -1)
    prep = _rwkv_prep(p, row(lp["rw_mu"]), lp["rw_w0"], w2_pad, lp["rw_a0"], a2_pad, lp["rw_g2"],
                      row(lp["rw_kk"]), row(lp["rw_ka"]), row(lp["rw_rk"]))
    scan_in, (g, bonus) = prep[:9], prep[9:]
    ctx_in = [a[:N_CTX_TOK].reshape(BATCH, SEQ, B_WIDTH) for a in scan_in]
    lat_in = [a[N_CTX_TOK:].reshape(DEC_BATCH, DEC_SEQ, B_WIDTH) for a in scan_in]
    s0_ctx = jnp.zeros((BATCH, HEAD_PAIRS, HEAD_DIM, SCAN_LANES), jnp.float32)
    yf_c, yb_c, s_ctx = _rwkv_scan(ctx_in, s0_ctx, seq_group=4)
    yf_l, yb_l, _ = _rwkv_scan(lat_in, _state_to_pairs(state_lat), seq_group=DEC_BATCH)
    flat = lambda u, v: jnp.concatenate([u.reshape(N_CTX_TOK, B_WIDTH), v.reshape(N_LAT_TOK, B_WIDTH)], axis=0)
    y = _rwkv_post(flat(yf_c, yf_l), flat(yb_c, yb_l), g, bonus, row(lp["rw_lnx_g"]), row(lp["rw_lnx_b"]))
    return y, _state_from_pairs(s_ctx)


def _rmsnorm(x, g):
    y = x * lax.rsqrt(jnp.mean(x * x, axis=-1, keepdims=True) + RMS_EPS)
    return y * g


def _axial_rope(x):
    T = x.shape[1]
    quarter = x.shape[-1] // 4
    t = jnp.arange(T)
    pos = jnp.stack([t // GRID_W, t % GRID_W], axis=-1).astype(jnp.float32)
    freqs = ROPE_BASE ** (-jnp.arange(quarter, dtype=jnp.float32) / quarter)
    ang = pos[:, :, None] * freqs
    bshape = (T,) + (1,) * (x.ndim - 3) + (2, quarter)
    cos = jnp.cos(ang).reshape(bshape).astype(x.dtype)
    sin = jnp.sin(ang).reshape(bshape).astype(x.dtype)
    xr = x.reshape(x.shape[:-1] + (2, 2, quarter))
    x1 = xr[..., 0, :]
    x2 = xr[..., 1, :]
    out = jnp.stack([x1 * cos - x2 * sin, x2 * cos + x1 * sin], axis=-2)
    return out.reshape(x.shape)


def _dense_attention(q, k, v, sink):
    B, S, KV, G, D = q.shape
    nb = S // Q_BLOCK
    scale = D ** -0.5
    vf = v.astype(jnp.float32)
    qb = jnp.moveaxis(q.reshape(B, nb, Q_BLOCK, KV, G, D), 1, 0)

    def block(qi):
        s = jnp.einsum("bqkgd,bskd->bkgqs", qi, k).astype(jnp.float32) * scale
        m = jnp.max(s, axis=-1)
        if sink is not None:
            sk = sink.astype(jnp.float32)[None, :, :, None]
            m = jnp.maximum(m, sk)
        p = jnp.exp(s - m[..., None])
        den = jnp.sum(p, axis=-1)
        if sink is not None:
            den = den + jnp.exp(sk - m)
        o = jnp.einsum("bkgqs,bskd->bqkgd", p, vf)
        return o / jnp.moveaxis(den, -1, 1)[..., None]

    o = lax.map(block, qb)
    return jnp.moveaxis(o, 0, 1).reshape(B, S, KV * G * D).astype(q.dtype)


def _window_attention(q, k, v, ck, cv, sink):
    B, T, KV, G, D = q.shape
    nb = T // Q_BLOCK
    span = Q_BLOCK + 2 * WINDOW
    scale = D ** -0.5
    pad = ((0, 0), (WINDOW, WINDOW), (0, 0), (0, 0))
    kp = jnp.pad(k, pad)
    vp = jnp.pad(v, pad)
    idx = jnp.arange(nb)[:, None] * Q_BLOCK + jnp.arange(span)[None, :]
    kb = kp[:, idx]
    vb = vp[:, idx].astype(jnp.float32)
    qb = q.reshape(B, nb, Q_BLOCK, KV, G, D)
    s = jnp.einsum("bnqkgd,bnskd->bnkgqs", qb, kb).astype(jnp.float32) * scale
    qpos = jnp.arange(nb)[:, None] * Q_BLOCK + jnp.arange(Q_BLOCK)[None, :]
    kpos = idx - WINDOW
    valid = ((jnp.abs(qpos[:, :, None] - kpos[:, None, :]) <= WINDOW)
             & (kpos >= 0)[:, None, :] & (kpos < T)[:, None, :])
    s = jnp.where(valid[None, :, None, None], s, -jnp.inf)
    sc = jnp.einsum("bnqkgd,bpkd->bnkgqp", qb, ck).astype(jnp.float32) * scale
    sk = sink.astype(jnp.float32)[None, None, :, :, None]
    m = jnp.maximum(jnp.maximum(jnp.max(s, axis=-1), jnp.max(sc, axis=-1)), sk)
    ps = jnp.exp(s - m[..., None])
    pc = jnp.exp(sc - m[..., None])
    den = jnp.sum(ps, axis=-1) + jnp.sum(pc, axis=-1) + jnp.exp(sk - m)
    o = (jnp.einsum("bnkgqs,bnskd->bnqkgd", ps, vb)
         + jnp.einsum("bnkgqp,bpkd->bnqkgd", pc, cv.astype(jnp.float32)))
    o = o / jnp.moveaxis(den, -1, 2)[..., None]
    return o.reshape(B, T, KV * G * D).astype(q.dtype)


def _neighbourhood_attention(q, k, v, ck, cv, rpb):
    B, T, H, D = q.shape
    rows = T // GRID_W
    wr = min(NA_ROWS, rows)
    scale = D ** -0.5
    qg = q.reshape(B, rows, GRID_W, H, D)
    kg = k.reshape(B, rows, GRID_W, H, D)
    vg = v.reshape(B, rows, GRID_W, H, D)
    r = jnp.arange(rows)
    rs = jnp.clip(r - wr // 2, 0, rows - wr)
    row_idx = rs[:, None] + jnp.arange(wr)[None, :]
    k_rows = kg[:, row_idx]
    v_rows = vg[:, row_idx].astype(jnp.float32)
    col = jnp.arange(GRID_W)
    cs = jnp.clip(col - NA_COLS // 2, 0, GRID_W - NA_COLS)
    col_ok = (col[None, :] >= cs[:, None]) & (col[None, :] < cs[:, None] + NA_COLS)
    dr = row_idx - r[:, None] + NA_ROWS - 1
    dc = jnp.clip(col[None, :] - col[:, None], -(NA_COLS - 1), NA_COLS - 1) + NA_COLS - 1
    bias = rpb[:, dr[:, None, :, None], dc[None, :, None, :]].astype(jnp.float32)
    s = jnp.einsum("brchd,brawhd->bhrcaw", qg, k_rows).astype(jnp.float32) * scale + bias[None]
    s = jnp.where(col_ok[None, None, None, :, None, :], s, -jnp.inf)
    sc = jnp.einsum("brchd,bphd->bhrcp", qg, ck).astype(jnp.float32) * scale
    m = jnp.maximum(jnp.max(s, axis=(-2, -1)), jnp.max(sc, axis=-1))
    pn = jnp.exp(s - m[..., None, None])
    pc = jnp.exp(sc - m[..., None])
    den = jnp.sum(pn, axis=(-2, -1)) + jnp.sum(pc, axis=-1)
    o = (jnp.einsum("bhrcaw,brawhd->brchd", pn, v_rows)
         + jnp.einsum("bhrcp,bphd->brchd", pc, cv.astype(jnp.float32)))
    o = o / jnp.transpose(den, (0, 2, 3, 1))[..., None]
    return o.reshape(B, T, H * D).astype(q.dtype)


def _attention_mixers(p, lp, ctx):
    B, T, _ = p.shape
    pa, _, pc = jnp.split(p, [A_COLS, A_COLS + B_COLS], axis=-1)
    qa, ka, va = jnp.split(pa, [A_HEADS * HEAD_DIM, (A_HEADS + A_KV) * HEAD_DIM], axis=-1)
    qa = qa.reshape(B, T, A_KV, A_GROUP, HEAD_DIM)
    ka = ka.reshape(B, T, A_KV, HEAD_DIM)
    va = va.reshape(B, T, A_KV, HEAD_DIM)
    qc, kc, vc = jnp.split(pc, 3, axis=-1)
    qc = qc.reshape(B, T, C_HEADS, HEAD_DIM)
    kc = kc.reshape(B, T, C_HEADS, HEAD_DIM)
    vc = vc.reshape(B, T, C_HEADS, HEAD_DIM)
    sink = lp["a_sink"].reshape(A_KV, A_GROUP)
    if ctx is None:
        ya = _dense_attention(qa, ka, va, sink)
        yc = _dense_attention(qc[:, :, :, None, :], kc, vc, None)
    else:
        ctx_ak, ctx_av, ctx_ck, ctx_cv = ctx
        ya = _window_attention(_axial_rope(qa), _axial_rope(ka), va, ctx_ak, ctx_av, sink)
        yc = _neighbourhood_attention(qc, kc, vc, ctx_ck, ctx_cv, lp["na_rpb"])
    return ya, yc, (ka, va, kc, vc)


def kernel(x_prompt, x_sample, cache_a_k, cache_a_v, cache_c_k, cache_c_v, state_rwkv, c, c_ctx, ln1_g, ln2_g, lnf_g, ada_w, ada_b, w_in, a_sink, a_out, rw_mu, rw_w0, rw_w2, rw_a0, rw_a2, rw_g2, rw_kk, rw_ka, rw_rk, rw_lnx_g, rw_lnx_b, rw_out, na_rpb, na_out, w_o, pe_q, pe_subkeys, pe_u, pe_v):
    bf16 = jnp.bfloat16
    x = jnp.concatenate([x_prompt.reshape(N_CTX_TOK, D_MODEL), x_sample.reshape(N_LAT_TOK, D_MODEL)], axis=0)
    silu_all = jnp.concatenate([jax.nn.silu(c_ctx)[None, :], jax.nn.silu(c)], axis=0)
    n_mix = A_COLS + B_COLS + C_COLS
    list_ak, list_av, list_ck, list_cv, list_st = [], [], [], [], []
    for l in range(DEPTH):
        lp = {
            "a_sink": a_sink[l], "rw_mu": rw_mu[l], "rw_w0": rw_w0[l], "rw_w2": rw_w2[l], "rw_a0": rw_a0[l],
            "rw_a2": rw_a2[l], "rw_g2": rw_g2[l], "rw_kk": rw_kk[l], "rw_ka": rw_ka[l],
            "rw_rk": rw_rk[l], "rw_lnx_g": rw_lnx_g[l], "rw_lnx_b": rw_lnx_b[l], "na_rpb": na_rpb[l],
        }
        mod = (silu_all @ ada_w[l] + ada_b[l]).reshape(N_SEG, N_MOD, D_MODEL)

        w_in_l = jnp.concatenate([w_in[l][:, n_mix:], w_in[l][:, :n_mix]], axis=1).astype(bf16)
        p, _ = _norm_proj(x, ln1_g[l][None, :], mod, w_in_l, 0, 1, col_tile=IN_COLS // 3)
        pm_ctx = p[:N_CTX_TOK, N_BRANCH * D_MODEL:].reshape(BATCH, SEQ, n_mix)
        pm_lat = p[N_CTX_TOK:, N_BRANCH * D_MODEL:].reshape(DEC_BATCH, DEC_SEQ, n_mix)
        ya_c, yc_c, (ak, av, ck, cv) = _attention_mixers(pm_ctx, lp, None)
        ctx = (cache_a_k[:, l], cache_a_v[:, l], cache_c_k[:, l], cache_c_v[:, l])
        ya_l, yc_l, _ = _attention_mixers(pm_lat, lp, ctx)
        yb, st = _rwkv_mixer(p, lp, state_rwkv[:, l])
        list_ak.append(ak)
        list_av.append(av)
        list_ck.append(ck)
        list_cv.append(cv)
        list_st.append(st)
        cat = lambda u, v: jnp.concatenate([u.reshape(N_CTX_TOK, -1), v.reshape(N_LAT_TOK, -1)], axis=0)
        x = _merge(x, cat(ya_c, ya_l), yb, cat(yc_c, yc_l), p, mod,
                   a_out[l].astype(bf16), rw_out[l].astype(bf16), na_out[l].astype(bf16), w_o[l].astype(bf16))

        q, h2 = _norm_proj(x, ln2_g[l][None, :], mod, pe_q[l].astype(bf16), 3, 4,
                           col_tile=PEER_HEADS * PEER_KEY_DIM)
        s1m, s2m, thr, c0 = _peer_route(q, pe_subkeys[l])
        x = _peer_dense(h2, x, mod, s1m, s2m, thr, c0, pe_u[l].astype(bf16), pe_v[l].T.astype(bf16))

    y = _rmsnorm(x, lnf_g)
    y_prompt = y[:N_CTX_TOK].reshape(BATCH, SEQ, D_MODEL)
    y_sample = y[N_CTX_TOK:].reshape(DEC_BATCH, DEC_SEQ, D_MODEL)
    return (y_prompt, y_sample, jnp.stack(list_ak, axis=1), jnp.stack(list_av, axis=1),
            jnp.stack(list_ck, axis=1), jnp.stack(list_cv, axis=1), jnp.stack(list_st, axis=1))
```

```python
import functools

import jax
import jax.numpy as jnp
from jax import lax
from jax.experimental import pallas as pl
from jax.experimental.pallas import tpu as pltpu

D_MODEL = 1024
BATCH = 16
SEQ = 256
DEPTH = 4
DEC_BATCH = 2
DEC_SEQ = 2048
PAST_LEN = 512
GRID_W = 64
HEAD_DIM = 64
A_HEADS = 8
A_KV = 2
A_GROUP = A_HEADS // A_KV
WINDOW = 128
Q_BLOCK = 128
B_HEADS = 8
B_WIDTH = B_HEADS * HEAD_DIM
LORA_W = 64
LORA_A = 64
LORA_G = 128
GN_EPS = 64e-5
C_HEADS = 8
NA_ROWS = 8
NA_COLS = 16
N_BRANCH = 3
A_COLS = (A_HEADS + 2 * A_KV) * HEAD_DIM
B_COLS = 3 * B_WIDTH + 2 * LORA_W + 2 * LORA_A + LORA_G
C_COLS = 3 * C_HEADS * HEAD_DIM
IN_COLS = A_COLS + B_COLS + C_COLS + N_BRANCH * D_MODEL
PEER_HEADS = 8
N_KEYS = 128
N_EXPERTS = N_KEYS * N_KEYS
PEER_KEY_DIM = 256
PEER_TOPK = 16
PEER_CHUNK = 128
ROPE_BASE = 10000.0
RMS_EPS = 1e-6
N_MOD = 6

N_CTX_TOK = BATCH * SEQ
N_LAT_TOK = DEC_BATCH * DEC_SEQ
N_TOK = N_CTX_TOK + N_LAT_TOK
N_SEG = 1 + DEC_BATCH

VMEM_LIMIT_BYTES = 48 * 1024 * 1024

ROW_TILE = 512
assert N_CTX_TOK % ROW_TILE == 0 and DEC_SEQ % ROW_TILE == 0


def _segment_of_row_block(i):
    n_ctx_blocks = N_CTX_TOK // ROW_TILE
    blocks_per_lat = DEC_SEQ // ROW_TILE
    return jnp.where(i < n_ctx_blocks, 0, 1 + (i - n_ctx_blocks) // blocks_per_lat)


def _norm_proj_kernel(x_ref, g_ref, mod_ref, w_ref, o_ref, h_ref, *, shift_idx, scale_idx):
    @pl.when(pl.program_id(1) == 0)
    def _():
        x = x_ref[...]
        y = x * lax.rsqrt(jnp.mean(x * x, axis=-1, keepdims=True) + RMS_EPS) * g_ref[...]
        h = y * (1.0 + mod_ref[0, scale_idx:scale_idx + 1, :]) + mod_ref[0, shift_idx:shift_idx + 1, :]
        h_ref[...] = h.astype(jnp.bfloat16)

    o_ref[...] = jnp.dot(h_ref[...], w_ref[...], preferred_element_type=jnp.float32)


def _norm_proj(x, g, mod, w_bf16, shift_idx, scale_idx, col_tile):
    n_rows, d = x.shape
    n_cols = w_bf16.shape[1]
    assert n_cols % col_tile == 0
    return pl.pallas_call(
        functools.partial(_norm_proj_kernel, shift_idx=shift_idx, scale_idx=scale_idx),
        grid=(n_rows // ROW_TILE, n_cols // col_tile),
        in_specs=[
            pl.BlockSpec((ROW_TILE, d), lambda i, j: (i, 0)),
            pl.BlockSpec((1, d), lambda i, j: (0, 0)),
            pl.BlockSpec((1, N_MOD, d), lambda i, j: (_segment_of_row_block(i), 0, 0)),
            pl.BlockSpec((d, col_tile), lambda i, j: (0, j)),
        ],
        out_specs=[pl.BlockSpec((ROW_TILE, col_tile), lambda i, j: (i, j)),
                   pl.BlockSpec((ROW_TILE, d), lambda i, j: (i, 0))],
        out_shape=[jax.ShapeDtypeStruct((n_rows, n_cols), jnp.float32),
                   jax.ShapeDtypeStruct((n_rows, d), jnp.bfloat16)],
        compiler_params=pltpu.CompilerParams(
            dimension_semantics=("arbitrary", "arbitrary"), vmem_limit_bytes=VMEM_LIMIT_BYTES),
        name="norm_proj",
    )(x, g, mod, w_bf16)


LANE = 128
ROUTE_TOK = 512
PEER_TOK = 512
PEER_EXP = 1024
NEG_INF = float("-inf")


def _take_top(vals, k):
    rows = lax.broadcasted_iota(jnp.int32, vals.shape, 0)
    taken = []
    for _ in range(k):
        m = jnp.max(vals, axis=0, keepdims=True)
        first = jnp.min(jnp.where(vals == m, rows, vals.shape[0]), axis=0, keepdims=True)
        vals = jnp.where(rows == first, NEG_INF, vals)
        taken.append(m)
    return jnp.concatenate(taken, axis=0), vals


def _peer_route_kernel(q_ref, sk_ref, s1m_ref, s2m_ref, thr_ref, c0_ref):
    def sub_tile(t, carry):
        tok = pl.ds(pl.multiple_of(t * LANE, LANE), LANE)
        masked, top_vals = [], []
        for z in range(2):
            qz = q_ref[tok, z * LANE:(z + 1) * LANE]
            s = lax.dot_general(sk_ref[0, z], qz, (((1,), (1,)), ((), ())),
                                preferred_element_type=jnp.float32)
            v, rest = _take_top(s, PEER_TOPK)
            masked.append(jnp.where(rest == NEG_INF, s, NEG_INF))
            top_vals.append(v)
        cand = jnp.concatenate([top_vals[0][i:i + 1] + top_vals[1] for i in range(PEER_TOPK)], axis=0)
        sv, _ = _take_top(cand, PEER_TOPK)
        z_sum = jnp.sum(jnp.exp(sv - sv[0:1]), axis=0, keepdims=True)
        s1m_ref[0, :, tok] = masked[0]
        s2m_ref[0, :, tok] = masked[1]
        thr_ref[0, :, tok] = sv[PEER_TOPK - 1:PEER_TOPK]
        c0_ref[0, :, tok] = -sv[0:1] - jnp.log(z_sum)
        return carry

    lax.fori_loop(0, ROUTE_TOK // LANE, sub_tile, 0)


def _peer_route(q, subkeys):
    n = q.shape[0]
    big = pl.BlockSpec((1, N_KEYS, ROUTE_TOK), lambda i, h: (h, 0, i))
    small = pl.BlockSpec((1, 1, ROUTE_TOK), lambda i, h: (h, 0, i))
    return pl.pallas_call(
        _peer_route_kernel,
        grid=(n // ROUTE_TOK, PEER_HEADS),
        in_specs=[
            pl.BlockSpec((ROUTE_TOK, PEER_KEY_DIM), lambda i, h: (i, h)),
            pl.BlockSpec((1, 2, N_KEYS, PEER_KEY_DIM // 2), lambda i, h: (h, 0, 0, 0)),
        ],
        out_specs=[big, big, small, small],
        out_shape=[jax.ShapeDtypeStruct((PEER_HEADS, N_KEYS, n), jnp.float32)] * 2
        + [jax.ShapeDtypeStruct((PEER_HEADS, 1, n), jnp.float32)] * 2,
        compiler_params=pltpu.CompilerParams(
            dimension_semantics=("arbitrary", "arbitrary"), vmem_limit_bytes=VMEM_LIMIT_BYTES),
        name="peer_route",
    )(q, subkeys)


def _gelu_tanh(x):
    return 0.5 * x * (1.0 + jnp.tanh(0.7978845608028654 * (x + 0.044715 * (x * x * x))))


def _peer_dense_kernel(h_ref, x_ref, mod_ref, s1m_ref, s2m_ref, thr_ref, c0_ref, u_ref, vt_ref, o_ref,
                       ht_scr, s2c_scr, coef_scr, acc_scr, *, gate_idx):
    e = pl.program_id(1)

    @pl.when(e == 0)
    def _():
        ht_scr[...] = h_ref[...].astype(jnp.float32).T.astype(jnp.bfloat16)
        s2c_scr[...] = s2m_ref[...] + c0_ref[...]
        acc_scr[...] = jnp.zeros_like(acc_scr)

    act = jnp.dot(u_ref[...], ht_scr[...], preferred_element_type=jnp.float32)
    a_per_step = PEER_EXP // N_KEYS
    a_rows = pl.ds(pl.multiple_of(e * a_per_step, a_per_step), a_per_step)
    for t in range(PEER_TOK // LANE):
        tok = slice(t * LANE, (t + 1) * LANE)
        s1_all = [s1m_ref[hh, a_rows, tok] for hh in range(PEER_HEADS)]
        for a in range(a_per_step):
            rows = slice(a * N_KEYS, (a + 1) * N_KEYS)
            w = jnp.zeros((N_KEYS, LANE), jnp.float32)
            for hh in range(PEER_HEADS):
                s1 = s1_all[hh][a:a + 1]
                picked = (s1 + s2m_ref[hh, :, tok]) >= thr_ref[hh, :, tok]
                w = w + jnp.where(picked, jnp.exp(s1 + s2c_scr[hh, :, tok]), 0.0)
            coef_scr[rows, tok] = (w * _gelu_tanh(act[rows, tok])).astype(jnp.bfloat16)
    acc_scr[...] += jnp.dot(vt_ref[...], coef_scr[...], preferred_element_type=jnp.float32)

    @pl.when(e == pl.num_programs(1) - 1)
    def _():
        o_ref[...] = x_ref[...] + mod_ref[0, gate_idx:gate_idx + 1, :] * acc_scr[...].T


def _peer_dense(h, x, mod, s1m, s2m, thr, c0, u_bf16, vt_bf16):
    n, d = x.shape
    assert PEER_TOK == ROW_TILE
    tok_row = lambda i, e: (i, 0)
    big = pl.BlockSpec((PEER_HEADS, N_KEYS, PEER_TOK), lambda i, e: (0, 0, i))
    small = pl.BlockSpec((PEER_HEADS, 1, PEER_TOK), lambda i, e: (0, 0, i))
    return pl.pallas_call(
        functools.partial(_peer_dense_kernel, gate_idx=5),
        grid=(n // PEER_TOK, N_EXPERTS // PEER_EXP),
        in_specs=[
            pl.BlockSpec((PEER_TOK, d), tok_row),
            pl.BlockSpec((PEER_TOK, d), tok_row),
            pl.BlockSpec((1, N_MOD, d), lambda i, e: (_segment_of_row_block(i), 0, 0)),
            big, big, small, small,
            pl.BlockSpec((PEER_EXP, d), lambda i, e: (e, 0)),
            pl.BlockSpec((d, PEER_EXP), lambda i, e: (0, e)),
        ],
        out_specs=pl.BlockSpec((PEER_TOK, d), tok_row),
        out_shape=jax.ShapeDtypeStruct((n, d), jnp.float32),
        scratch_shapes=[
            pltpu.VMEM((d, PEER_TOK), jnp.bfloat16),
            pltpu.VMEM((PEER_HEADS, N_KEYS, PEER_TOK), jnp.float32),
            pltpu.VMEM((PEER_EXP, PEER_TOK), jnp.bfloat16),
            pltpu.VMEM((d, PEER_TOK), jnp.float32),
        ],
        compiler_params=pltpu.CompilerParams(
            dimension_semantics=("arbitrary", "arbitrary"), vmem_limit_bytes=VMEM_LIMIT_BYTES),
        name="peer_dense",
    )(h, x, mod, s1m, s2m, thr, c0, u_bf16, vt_bf16)


def _merge_kernel(x_ref, ya_ref, yb_ref, yc_ref, ga_ref, gb_ref, gc_ref, mod_ref,
                  wa_ref, wb_ref, wc_ref, wo_ref, o_ref, *, gate_idx):
    def branch(y_ref, gate_ref, w_ref):
        proj = jnp.dot(y_ref[...].astype(jnp.bfloat16), w_ref[...], preferred_element_type=jnp.float32)
        return jax.nn.sigmoid(gate_ref[...]) * proj

    merged = branch(ya_ref, ga_ref, wa_ref) + branch(yb_ref, gb_ref, wb_ref) + branch(yc_ref, gc_ref, wc_ref)
    out = jnp.dot(merged.astype(jnp.bfloat16), wo_ref[...], preferred_element_type=jnp.float32)
    o_ref[...] = x_ref[...] + mod_ref[0, gate_idx:gate_idx + 1, :] * out


def _merge(x, ya, yb, yc, p, mod, wa, wb, wc, wo):
    n_rows, d = x.shape
    gate_col0 = 0
    row = lambda i: (i, 0)
    const = lambda i: (0, 0)
    y_spec = pl.BlockSpec((ROW_TILE, ya.shape[1]), row)
    w_spec = pl.BlockSpec(wa.shape, const)
    return pl.pallas_call(
        functools.partial(_merge_kernel, gate_idx=2),
        grid=(n_rows // ROW_TILE,),
        in_specs=[
            pl.BlockSpec((ROW_TILE, d), row), y_spec, y_spec, y_spec,
            pl.BlockSpec((ROW_TILE, d), lambda i: (i, gate_col0)),
            pl.BlockSpec((ROW_TILE, d), lambda i: (i, gate_col0 + 1)),
            pl.BlockSpec((ROW_TILE, d), lambda i: (i, gate_col0 + 2)),
            pl.BlockSpec((1, N_MOD, d), lambda i: (_segment_of_row_block(i), 0, 0)),
            w_spec, w_spec, w_spec, pl.BlockSpec(wo.shape, const),
        ],
        out_specs=pl.BlockSpec((ROW_TILE, d), row),
        out_shape=jax.ShapeDtypeStruct((n_rows, d), jnp.float32),
        compiler_params=pltpu.CompilerParams(
            dimension_semantics=("arbitrary",), vmem_limit_bytes=VMEM_LIMIT_BYTES),
        name="merge_out_proj",
    )(x, ya, yb, yc, p, p, p, mod, wa, wb, wc, wo)


HEAD_PAIRS = B_HEADS // 2
PREP_TOK = 256
SCAN_CHUNK = 128
assert PREP_TOK == SEQ and DEC_SEQ % PREP_TOK == 0 and 2 * HEAD_DIM == LANE
_HI = lax.Precision.HIGHEST


def _head_ones(n):
    r = lax.broadcasted_iota(jnp.int32, (n, n), 0) // HEAD_DIM
    c = lax.broadcasted_iota(jnp.int32, (n, n), 1) // HEAD_DIM
    return (r == c).astype(jnp.float32)


def _softplus(y):
    return jnp.maximum(y, 0.0) + jnp.log(1.0 + jnp.exp(-jnp.abs(y)))


def _rwkv_prep_kernel(pb_ref, prev_ref, next_ref, mu_ref, w0_ref, w2_ref, a0_ref, a2_ref, g2_ref,
                      kks_ref, ka_ref, rk_ref, ones_ref,
                      r_o, kk_o, v_o, w0_o, w1_o, k0_o, k1_o, nb0_o, nb1_o, g_o, bonus_o):
    i = pl.program_id(0)
    n_ctx = N_CTX_TOK // PREP_TOK
    per_lat = DEC_SEQ // PREP_TOK
    j = (i - n_ctx) % per_lat
    starts_seq = jnp.logical_or(i < n_ctx, j == 0)
    ends_seq = jnp.logical_or(i < n_ctx, j == per_lat - 1)

    pb = pb_ref[...]
    rows = lax.broadcasted_iota(jnp.int32, (PREP_TOK, 1), 0)
    before = jnp.where(starts_seq, 0.0, prev_ref[7:8, :])
    after = jnp.where(ends_seq, 0.0, next_ref[0:1, :])
    prev = jnp.where(rows == 0, before, pltpu.roll(pb, 1, 0))
    nxt = jnp.where(rows == PREP_TOK - 1, after, pltpu.roll(pb, PREP_TOK - 1, 0))
    xb = pb + mu_ref[...] * (0.5 * (prev + nxt) - pb)

    w = B_WIDTH
    r, k, v = xb[:, 0:w], xb[:, w:2 * w], xb[:, 2 * w:3 * w]
    wd = xb[:, 3 * w:3 * w + 2 * LORA_W]
    ad = xb[:, 3 * w + 2 * LORA_W:3 * w + 2 * LORA_W + 2 * LORA_A]
    gd = xb[:, 3 * w + 2 * LORA_W + 2 * LORA_A:]
    ones = ones_ref[...]

    def head_sum(x):
        return jnp.dot(x, ones, preferred_element_type=jnp.float32, precision=_HI)

    lw = jnp.dot(jnp.tanh(wd), w2_ref[...], preferred_element_type=jnp.float32, precision=_HI)
    la = jnp.dot(ad, a2_ref[...], preferred_element_type=jnp.float32, precision=_HI)
    kk = k * kks_ref[...]
    kk = kk * lax.rsqrt(head_sum(kk * kk) + 1e-12)
    kd_sum = jnp.zeros_like(k)
    for z, (w_o, k_o, nb_o) in enumerate(((w0_o, k0_o, nb0_o), (w1_o, k1_o, nb1_o))):
        wlog = -_softplus(-(w0_ref[z:z + 1, :] + lw[:, z * w:(z + 1) * w])) - 0.5
        a = jax.nn.sigmoid(a0_ref[z:z + 1, :] + la[:, z * w:(z + 1) * w])
        kd = k * (1.0 + (a - 1.0) * ka_ref[...])
        kd_sum = kd_sum + kd
        w_o[...] = jnp.exp(-jnp.exp(wlog))
        k_o[...] = kd
        nb_o[...] = -(kk * a)
    r_o[...] = r
    kk_o[...] = kk
    v_o[...] = v
    g_o[...] = jnp.dot(jax.nn.sigmoid(gd), g2_ref[...], preferred_element_type=jnp.float32, precision=_HI)
    bonus_o[...] = head_sum(r * kd_sum * rk_ref[...]) * v


def _rwkv_prep(p, mu, w0, w2_pad, a0, a2_pad, g2, kk_scale, ka, rk):
    n = p.shape[0]
    col_blk = (N_BRANCH * D_MODEL + A_COLS) // B_COLS
    assert col_blk * B_COLS == N_BRANCH * D_MODEL + A_COLS
    halo = 8
    per = PREP_TOK // halo
    full = lambda a: pl.BlockSpec(a.shape, lambda i: (0,) * a.ndim)
    ones = _head_ones(B_WIDTH)
    consts = (mu, w0, w2_pad, a0, a2_pad, g2, kk_scale, ka, rk, ones)
    out_spec = pl.BlockSpec((PREP_TOK, B_WIDTH), lambda i: (i, 0))
    return pl.pallas_call(
        _rwkv_prep_kernel,
        grid=(n // PREP_TOK,),
        in_specs=[
            pl.BlockSpec((PREP_TOK, B_COLS), lambda i: (i, col_blk)),
            pl.BlockSpec((halo, B_COLS), lambda i: (jnp.maximum(i * per - 1, 0), col_blk)),
            pl.BlockSpec((halo, B_COLS), lambda i: (jnp.minimum((i + 1) * per, n // halo - 1), col_blk)),
        ] + [full(a) for a in consts],
        out_specs=[out_spec] * 11,
        out_shape=[jax.ShapeDtypeStruct((n, B_WIDTH), jnp.float32)] * 11,
        compiler_params=pltpu.CompilerParams(
            dimension_semantics=("arbitrary",), vmem_limit_bytes=VMEM_LIMIT_BYTES),
        name="rwkv_prep",
    )(p, p, p, *consts)


SCAN_LANES = 2 * LANE
SCAN_PAIR_GROUP = 4


def _head_sums(head_ones, *terms):
    sums = []
    for x, passes in terms:
        hi = x.astype(jnp.bfloat16)
        out = jnp.dot(hi, head_ones, preferred_element_type=jnp.float32)
        if passes == 2:
            lo = (x - hi.astype(jnp.float32)).astype(jnp.bfloat16)
            out = out + jnp.dot(lo, head_ones, preferred_element_type=jnp.float32)
        sums.append(out)
    return sums


def _rwkv_scan_kernel(rf, kkf, vf, wf, kf, nbf, rb, kkb, vb, wb, kb, nbb, s0_ref, ones_ref, eye_ref,
                      yf_ref, yb_ref, sfin_ref, s_scr, *, n_seq):
    c = pl.program_id(1)

    @pl.when(c == 0)
    def _():
        s_scr[...] = s0_ref[...]

    head_ones = ones_ref[...]
    eye = eye_ref[...]
    fwd_refs = (rf, kkf, vf, wf, kf, nbf)
    bwd_refs = (rb, kkb, vb, wb, kb, nbb)
    R, KK, V, W, K, NB = range(6)

    def step(t, carry):
        t_f = pl.ds(t, 1)
        t_b = pl.ds(SCAN_CHUNK - 1 - t, 1)
        for g in range(n_seq):
            rows_f = [a[g, t_f, :] for a in fwd_refs]
            rows_b = [a[g, t_b, :] for a in bwd_refs]

            def both(which, p):
                lanes = slice(p * LANE, (p + 1) * LANE)
                return jnp.concatenate([rows_f[which][:, lanes], rows_b[which][:, lanes]], axis=1)

            y_rows = []
            for p0 in range(0, HEAD_PAIRS, SCAN_PAIR_GROUP):
                pairs = range(p0, p0 + SCAN_PAIR_GROUP)
                rows = {p: slice((p - p0) * HEAD_DIM, (p - p0 + 1) * HEAD_DIM) for p in pairs}
                s_old = {p: s_scr[g, p] for p in pairs}
                s_kk, v_col = _head_sums(
                    head_ones,
                    (jnp.concatenate([s_old[p] * both(KK, p) for p in pairs], axis=0), 2),
                    (jnp.concatenate([eye * both(V, p) for p in pairs], axis=0), 1))
                s_new = {}
                for p in pairs:
                    s = s_old[p] * both(W, p) + s_kk[rows[p]] * both(NB, p) + v_col[rows[p]] * both(K, p)
                    s_scr[g, p] = s
                    s_new[p] = s
                (y_all,) = _head_sums(
                    head_ones, (jnp.concatenate([s_new[p] * both(R, p) for p in pairs], axis=0), 1))
                y_rows += [jnp.sum(y_all[rows[p]] * eye, axis=0, keepdims=True) for p in pairs]
            yf_ref[g, t_f, :] = jnp.concatenate([y[:, :LANE] for y in y_rows], axis=1)
            yb_ref[g, t_b, :] = jnp.concatenate([y[:, LANE:] for y in y_rows], axis=1)
        return carry

    lax.fori_loop(0, SCAN_CHUNK, step, 0, unroll=2)

    @pl.when(c == pl.num_programs(1) - 1)
    def _():
        sfin_ref[...] = s_scr[...]


def _rwkv_scan(seqs, s0, seq_group):
    r, kk, v, w0, w1, k0, k1, nb0, nb1 = seqs
    n_seq, t_len, _ = r.shape
    n_chunks = t_len // SCAN_CHUNK
    assert n_seq % seq_group == 0 and n_chunks * SCAN_CHUNK == t_len
    blk = (seq_group, SCAN_CHUNK, B_WIDTH)
    fwd = pl.BlockSpec(blk, lambda i, c: (i, c, 0))
    bwd = pl.BlockSpec(blk, lambda i, c: (i, n_chunks - 1 - c, 0))
    state_blk = (seq_group, HEAD_PAIRS, HEAD_DIM, SCAN_LANES)
    state = pl.BlockSpec(state_blk, lambda i, c: (i, 0, 0, 0))
    head_ones = _head_ones(SCAN_LANES).astype(jnp.bfloat16)
    eye = (lax.broadcasted_iota(jnp.int32, (HEAD_DIM, SCAN_LANES), 0)
           == lax.broadcasted_iota(jnp.int32, (HEAD_DIM, SCAN_LANES), 1) % HEAD_DIM).astype(jnp.float32)
    const = lambda a: pl.BlockSpec(a.shape, lambda i, c: (0, 0))
    y_shape = jax.ShapeDtypeStruct((n_seq, t_len, B_WIDTH), jnp.float32)
    return pl.pallas_call(
        functools.partial(_rwkv_scan_kernel, n_seq=seq_group),
        grid=(n_seq // seq_group, n_chunks),
        in_specs=[fwd] * 6 + [bwd] * 6 + [state, const(head_ones), const(eye)],
        out_specs=[fwd, bwd, state],
        out_shape=[y_shape, y_shape, jax.ShapeDtypeStruct(s0.shape, jnp.float32)],
        scratch_shapes=[pltpu.VMEM(state_blk, jnp.float32)],
        compiler_params=pltpu.CompilerParams(
            dimension_semantics=("arbitrary", "arbitrary"), vmem_limit_bytes=VMEM_LIMIT_BYTES),
        name="rwkv_scan",
    )(r, kk, v, w0, k0, nb0, r, kk, v, w1, k1, nb1, s0, head_ones, eye)


def _rwkv_post_kernel(yf_ref, yb_ref, g_ref, bonus_ref, lng_ref, lnb_ref, ones_ref, o_ref):
    ones = ones_ref[...]
    y = yf_ref[...] + yb_ref[...]
    mean = jnp.dot(y, ones, preferred_element_type=jnp.float32, precision=_HI) * (1.0 / HEAD_DIM)
    d = y - mean
    var = jnp.dot(d * d, ones, preferred_element_type=jnp.float32, precision=_HI) * (1.0 / HEAD_DIM)
    y = d * lax.rsqrt(var + GN_EPS) * lng_ref[...] + lnb_ref[...]
    o_ref[...] = (y + bonus_ref[...]) * g_ref[...]


def _rwkv_post(yf, yb, g, bonus, lnx_g, lnx_b):
    n = yf.shape[0]
    row = pl.BlockSpec((PREP_TOK, B_WIDTH), lambda i: (i, 0))
    vec = pl.BlockSpec((1, B_WIDTH), lambda i: (0, 0))
    ones = _head_ones(B_WIDTH)
    return pl.pallas_call(
        _rwkv_post_kernel,
        grid=(n // PREP_TOK,),
        in_specs=[row, row, row, row, vec, vec, pl.BlockSpec(ones.shape, lambda i: (0, 0))],
        out_specs=row,
        out_shape=jax.ShapeDtypeStruct((n, B_WIDTH), jnp.float32),
        compiler_params=pltpu.CompilerParams(
            dimension_semantics=("arbitrary",), vmem_limit_bytes=VMEM_LIMIT_BYTES),
        name="rwkv_post",
    )(yf, yb, g, bonus, lnx_g, lnx_b, ones)


def _state_to_pairs(s):
    b = s.shape[0]
    s = s.reshape(b, 2, HEAD_PAIRS, 2, HEAD_DIM, HEAD_DIM)
    return jnp.transpose(s, (0, 2, 4, 1, 3, 5)).reshape(b, HEAD_PAIRS, HEAD_DIM, SCAN_LANES)


def _state_from_pairs(s):
    b = s.shape[0]
    s = s.reshape(b, HEAD_PAIRS, HEAD_DIM, 2, 2, HEAD_DIM)
    return jnp.transpose(s, (0, 3, 1, 4, 2, 5)).reshape(b, 2, B_HEADS, HEAD_DIM, HEAD_DIM)


def _rwkv_mixer(p, lp, state_lat):
    zeros = jnp.zeros((LORA_W, B_WIDTH), jnp.float32)
    w2_pad = jnp.concatenate([jnp.concatenate([lp["rw_w2"][0], zeros], axis=1),
                              jnp.concatenate([zeros, lp["rw_w2"][1]], axis=1)], axis=0)
    a2_pad = jnp.concatenate([jnp.concatenate([lp["rw_a2"][0], zeros], axis=1),
                              jnp.concatenate([zeros, lp["rw_a2"][1]], axis=1)], axis=0)
    row = lambda a: a.reshape(1, -1)
    prep = _rwkv_prep(p, row(lp["rw_mu"]), lp["rw_w0"], w2_pad, lp["rw_a0"], a2_pad, lp["rw_g2"],
                      row(lp["rw_kk"]), row(lp["rw_ka"]), row(lp["rw_rk"]))
    scan_in, (g, bonus) = prep[:9], prep[9:]
    ctx_in = [a[:N_CTX_TOK].reshape(BATCH, SEQ, B_WIDTH) for a in scan_in]
    lat_in = [a[N_CTX_TOK:].reshape(DEC_BATCH, DEC_SEQ, B_WIDTH) for a in scan_in]
    s0_ctx = jnp.zeros((BATCH, HEAD_PAIRS, HEAD_DIM, SCAN_LANES), jnp.float32)
    yf_c, yb_c, s_ctx = _rwkv_scan(ctx_in, s0_ctx, seq_group=4)
    yf_l, yb_l, _ = _rwkv_scan(lat_in, _state_to_pairs(state_lat), seq_group=DEC_BATCH)
    flat = lambda u, v: jnp.concatenate([u.reshape(N_CTX_TOK, B_WIDTH), v.reshape(N_LAT_TOK, B_WIDTH)], axis=0)
    y = _rwkv_post(flat(yf_c, yf_l), flat(yb_c, yb_l), g, bonus, row(lp["rw_lnx_g"]), row(lp["rw_lnx_b"]))
    return y, _state_from_pairs(s_ctx)


ATT_SCALE = HEAD_DIM ** -0.5
COL0_QA = N_BRANCH * D_MODEL
COL0_KA = COL0_QA + A_HEADS * HEAD_DIM
COL0_VA = COL0_KA + A_KV * HEAD_DIM
COL0_QC = COL0_QA + A_COLS + B_COLS
COL0_KC = COL0_QC + C_HEADS * HEAD_DIM
COL0_VC = COL0_KC + C_HEADS * HEAD_DIM
A_GROUP_COLS = A_GROUP * HEAD_DIM
assert A_KV * HEAD_DIM == LANE and all(c % LANE == 0 for c in (COL0_QA, COL0_KA, COL0_VA, COL0_QC, COL0_KC, COL0_VC))
assert COL0_QA % A_GROUP_COLS == 0
GRID_ROWS = DEC_SEQ // GRID_W
NA_WIN_ROWS = min(NA_ROWS, GRID_ROWS)


def _attend(q, parts, sink=None):
    scores = []
    for k, _, bias in parts:
        s = lax.dot_general(q, k, (((1,), (1,)), ((), ())), preferred_element_type=jnp.float32)
        scores.append(s if bias is None else s + bias)
    m = functools.reduce(jnp.maximum, [jnp.max(s, axis=-1, keepdims=True) for s in scores])
    den = 0.0
    if sink is not None:
        m = jnp.maximum(m, sink)
        den = jnp.exp(sink - m)
    out = 0.0
    for s, (_, v, _) in zip(scores, parts):
        prob = jnp.exp(s - m)
        den = den + jnp.sum(prob, axis=-1, keepdims=True)
        out = out + jnp.dot(prob.astype(jnp.bfloat16), v, preferred_element_type=jnp.float32)
    return out / den


def _scaled_bf16(q):
    return (q * ATT_SCALE).astype(jnp.bfloat16)


def _kv_half(x, kv):
    return jnp.where(kv == 0, x[:, :HEAD_DIM], x[:, HEAD_DIM:])


def _ctx_gqa_kernel(q_ref, k_ref, v_ref, sink_ref, o_ref):
    kv = pl.program_id(1)
    k = _kv_half(k_ref[...], kv).astype(jnp.bfloat16)
    v = _kv_half(v_ref[...], kv).astype(jnp.bfloat16)
    outs = []
    for g in range(A_GROUP):
        q = _scaled_bf16(q_ref[:, g * HEAD_DIM:(g + 1) * HEAD_DIM])
        sink = sink_ref[pl.ds(kv * A_GROUP + g, 1), :][:, 0:1]
        outs.append(_attend(q, [(k, v, None)], sink))
    o_ref[...] = jnp.concatenate(outs, axis=1)


def _ctx_gqa(p, sink_rows):
    q0, k0, v0 = COL0_QA // A_GROUP_COLS, COL0_KA // LANE, COL0_VA // LANE
    return pl.pallas_call(
        _ctx_gqa_kernel,
        grid=(BATCH, A_KV),
        in_specs=[
            pl.BlockSpec((SEQ, A_GROUP_COLS), lambda b, kv: (b, q0 + kv)),
            pl.BlockSpec((SEQ, LANE), lambda b, kv: (b, k0)),
            pl.BlockSpec((SEQ, LANE), lambda b, kv: (b, v0)),
            pl.BlockSpec(sink_rows.shape, lambda b, kv: (0, 0)),
        ],
        out_specs=pl.BlockSpec((SEQ, A_GROUP_COLS), lambda b, kv: (b, kv)),
        out_shape=jax.ShapeDtypeStruct((N_CTX_TOK, A_HEADS * HEAD_DIM), jnp.float32),
        compiler_params=pltpu.CompilerParams(
            dimension_semantics=("arbitrary", "arbitrary"), vmem_limit_bytes=VMEM_LIMIT_BYTES),
        name="ctx_gqa",
    )(p, p, p, sink_rows)


def _ctx_mha_kernel(q_ref, k_ref, v_ref, o_ref):
    outs = []
    for j in range(2):
        lanes = slice(j * HEAD_DIM, (j + 1) * HEAD_DIM)
        outs.append(_attend(_scaled_bf16(q_ref[:, lanes]),
                            [(k_ref[:, lanes].astype(jnp.bfloat16), v_ref[:, lanes].astype(jnp.bfloat16), None)]))
    o_ref[...] = jnp.concatenate(outs, axis=1)


def _ctx_mha(p):
    q0, k0, v0 = COL0_QC // LANE, COL0_KC // LANE, COL0_VC // LANE
    return pl.pallas_call(
        _ctx_mha_kernel,
        grid=(BATCH, C_HEADS // 2),
        in_specs=[
            pl.BlockSpec((SEQ, LANE), lambda b, hp: (b, q0 + hp)),
            pl.BlockSpec((SEQ, LANE), lambda b, hp: (b, k0 + hp)),
            pl.BlockSpec((SEQ, LANE), lambda b, hp: (b, v0 + hp)),
        ],
        out_specs=pl.BlockSpec((SEQ, LANE), lambda b, hp: (b, hp)),
        out_shape=jax.ShapeDtypeStruct((N_CTX_TOK, C_HEADS * HEAD_DIM), jnp.float32),
        compiler_params=pltpu.CompilerParams(
            dimension_semantics=("arbitrary", "arbitrary"), vmem_limit_bytes=VMEM_LIMIT_BYTES),
        name="ctx_mha",
    )(p, p, p)


def _rope(x, cos, sin):
    reps = x.shape[1] // LANE
    if reps > 1:
        cos = jnp.concatenate([cos] * reps, axis=1)
        sin = jnp.concatenate([sin] * reps, axis=1)
    lane = lax.broadcasted_iota(jnp.int32, x.shape, 1)
    first = (lane % (HEAD_DIM // 2)) < (HEAD_DIM // 4)
    quarter = HEAD_DIM // 4
    pieces = [x[:, i * LANE:(i + 1) * LANE] for i in range(reps)]
    up = jnp.concatenate([pltpu.roll(piece, LANE - quarter, 1) for piece in pieces], axis=1)
    down = jnp.concatenate([pltpu.roll(piece, quarter, 1) for piece in pieces], axis=1)
    return x * cos + jnp.where(first, -up, down) * sin


def _lat_window_kernel(q_ref, kp_ref, kc_ref, kn_ref, vp_ref, vc_ref, vn_ref, ck_ref, cv_ref,
                       cosq_ref, sinq_ref, cosp_ref, sinp_ref, cosn_ref, sinn_ref, sink_ref, o_ref):
    qb = pl.program_id(1)
    kv = pl.program_id(2)
    n_blocks = pl.num_programs(1)
    row = lax.broadcasted_iota(jnp.int32, (Q_BLOCK, Q_BLOCK), 0)
    col = lax.broadcasted_iota(jnp.int32, (Q_BLOCK, Q_BLOCK), 1)
    bias_prev = jnp.where(jnp.logical_and(col >= row, qb > 0), 0.0, NEG_INF)
    bias_next = jnp.where(jnp.logical_and(col <= row, qb < n_blocks - 1), 0.0, NEG_INF)
    assert WINDOW == Q_BLOCK

    def keys(k_ref, cos_ref, sin_ref):
        return _kv_half(_rope(k_ref[...], cos_ref[...], sin_ref[...]), kv).astype(jnp.bfloat16)

    def vals(v_ref):
        return _kv_half(v_ref[...], kv).astype(jnp.bfloat16)

    parts = [
        (keys(kp_ref, cosp_ref, sinp_ref), vals(vp_ref), bias_prev),
        (keys(kc_ref, cosq_ref, sinq_ref), vals(vc_ref), None),
        (keys(kn_ref, cosn_ref, sinn_ref), vals(vn_ref), bias_next),
        (_kv_half(ck_ref[0], kv).astype(jnp.bfloat16), _kv_half(cv_ref[0], kv).astype(jnp.bfloat16), None),
    ]
    q_all = _rope(q_ref[...], cosq_ref[...], sinq_ref[...])
    outs = []
    for g in range(A_GROUP):
        sink = sink_ref[pl.ds(kv * A_GROUP + g, 1), :][:, 0:1]
        outs.append(_attend(_scaled_bf16(q_all[:, g * HEAD_DIM:(g + 1) * HEAD_DIM]), parts, sink))
    o_ref[...] = jnp.concatenate(outs, axis=1)


def _lat_window(p, cache_k, cache_v, cos, sin, sink_rows):
    n_blocks = DEC_SEQ // Q_BLOCK
    base = N_CTX_TOK // Q_BLOCK
    q0, k0, v0 = COL0_QA // A_GROUP_COLS, COL0_KA // LANE, COL0_VA // LANE
    prev = lambda b, qb, kv: base + b * n_blocks + jnp.maximum(qb - 1, 0)
    here = lambda b, qb, kv: base + b * n_blocks + qb
    nxt = lambda b, qb, kv: base + b * n_blocks + jnp.minimum(qb + 1, n_blocks - 1)
    slab = lambda rows, c0: pl.BlockSpec((Q_BLOCK, LANE), lambda b, qb, kv: (rows(b, qb, kv), c0))
    table = lambda blk: pl.BlockSpec((Q_BLOCK, LANE), lambda b, qb, kv: (blk(b, qb, kv), 0))
    cache = pl.BlockSpec((1, PAST_LEN, LANE), lambda b, qb, kv: (b, 0, 0))
    return pl.pallas_call(
        _lat_window_kernel,
        grid=(DEC_BATCH, n_blocks, A_KV),
        in_specs=[
            pl.BlockSpec((Q_BLOCK, A_GROUP_COLS), lambda b, qb, kv: (here(b, qb, kv), q0 + kv)),
            slab(prev, k0), slab(here, k0), slab(nxt, k0),
            slab(prev, v0), slab(here, v0), slab(nxt, v0),
            cache, cache,
            table(lambda b, qb, kv: qb), table(lambda b, qb, kv: qb),
            table(lambda b, qb, kv: jnp.maximum(qb - 1, 0)), table(lambda b, qb, kv: jnp.maximum(qb - 1, 0)),
            table(lambda b, qb, kv: jnp.minimum(qb + 1, n_blocks - 1)),
            table(lambda b, qb, kv: jnp.minimum(qb + 1, n_blocks - 1)),
            pl.BlockSpec(sink_rows.shape, lambda b, qb, kv: (0, 0)),
        ],
        out_specs=pl.BlockSpec((Q_BLOCK, A_GROUP_COLS), lambda b, qb, kv: (b * n_blocks + qb, kv)),
        out_shape=jax.ShapeDtypeStruct((N_LAT_TOK, A_HEADS * HEAD_DIM), jnp.float32),
        compiler_params=pltpu.CompilerParams(
            dimension_semantics=("arbitrary", "arbitrary", "arbitrary"), vmem_limit_bytes=VMEM_LIMIT_BYTES),
        name="lat_window",
    )(p, p, p, p, p, p, p, cache_k, cache_v, cos, sin, cos, sin, cos, sin, sink_rows)


N_DC = 2 * NA_COLS - 1
N_DR = 2 * NA_ROWS - 1


def _na_bias_kernel(rpb_ref, o_ref):
    h = pl.program_id(0)
    shape = (GRID_W, 2 * GRID_W)
    qc = lax.broadcasted_iota(jnp.int32, shape, 0)
    kc = lax.broadcasted_iota(jnp.int32, shape, 1) % GRID_W
    second = lax.broadcasted_iota(jnp.int32, shape, 1) >= GRID_W
    start = jnp.clip(qc - NA_COLS // 2, 0, GRID_W - NA_COLS)
    inside = jnp.logical_and(kc >= start, kc < start + NA_COLS)
    dc = jnp.clip(kc - qc, -(NA_COLS - 1), NA_COLS - 1) + NA_COLS - 1
    for dr in range(N_DR - 1):
        bias = jnp.zeros(shape, jnp.float32)
        for d in range(N_DC):
            lo = rpb_ref[(h * N_DR + dr) * N_DC + d]
            hi = rpb_ref[(h * N_DR + dr + 1) * N_DC + d]
            bias = jnp.where(dc == d, jnp.where(second, hi, lo), bias)
        o_ref[0, dr] = jnp.where(inside, bias, NEG_INF)


def _na_bias(rpb):
    return pl.pallas_call(
        _na_bias_kernel,
        grid=(C_HEADS,),
        in_specs=[pl.BlockSpec(memory_space=pltpu.SMEM)],
        out_specs=pl.BlockSpec((1, N_DR - 1, GRID_W, 2 * GRID_W), lambda h: (h, 0, 0, 0)),
        out_shape=jax.ShapeDtypeStruct((C_HEADS, N_DR - 1, GRID_W, 2 * GRID_W), jnp.float32),
        compiler_params=pltpu.CompilerParams(dimension_semantics=("arbitrary",)),
        name="na_bias",
    )(rpb.reshape(-1))


def _lat_neighbourhood_kernel(q_ref, k_ref, v_ref, ck_ref, cv_ref, bias_ref, o_ref):
    hp = pl.program_id(1)
    r = pl.program_id(2)
    first_row = jnp.clip(r - NA_WIN_ROWS // 2, 0, GRID_ROWS - NA_WIN_ROWS)
    local = pl.ds(pl.multiple_of(first_row * GRID_W, GRID_W), NA_WIN_ROWS * GRID_W)
    dr0 = first_row - r + NA_ROWS - 1
    outs = []
    for j in range(2):
        lanes = slice(j * HEAD_DIM, (j + 1) * HEAD_DIM)
        bias = jnp.concatenate([bias_ref[hp * 2 + j, dr0 + a] for a in range(0, NA_WIN_ROWS, 2)], axis=1)
        parts = [
            (k_ref[local, :][:, lanes].astype(jnp.bfloat16), v_ref[local, :][:, lanes].astype(jnp.bfloat16), bias),
            (ck_ref[0, :, lanes].astype(jnp.bfloat16), cv_ref[0, :, lanes].astype(jnp.bfloat16), None),
        ]
        outs.append(_attend(_scaled_bf16(q_ref[:, lanes]), parts))
    o_ref[...] = jnp.concatenate(outs, axis=1)


def _lat_neighbourhood(p, cache_k, cache_v, bias_tiles):
    assert NA_WIN_ROWS % 2 == 0 and N_CTX_TOK % DEC_SEQ == 0
    q0, k0, v0 = COL0_QC // LANE, COL0_KC // LANE, COL0_VC // LANE
    row_base = N_CTX_TOK // GRID_W
    seq_base = N_CTX_TOK // DEC_SEQ
    whole = lambda c0: pl.BlockSpec((DEC_SEQ, LANE), lambda b, hp, r: (seq_base + b, c0 + hp))
    cache = pl.BlockSpec((1, PAST_LEN, LANE), lambda b, hp, r: (b, 0, hp))
    return pl.pallas_call(
        _lat_neighbourhood_kernel,
        grid=(DEC_BATCH, C_HEADS // 2, GRID_ROWS),
        in_specs=[
            pl.BlockSpec((GRID_W, LANE), lambda b, hp, r: (row_base + b * GRID_ROWS + r, q0 + hp)),
            whole(k0), whole(v0), cache, cache,
            pl.BlockSpec(bias_tiles.shape, lambda b, hp, r: (0, 0, 0, 0)),
        ],
        out_specs=pl.BlockSpec((GRID_W, LANE), lambda b, hp, r: (b * GRID_ROWS + r, hp)),
        out_shape=jax.ShapeDtypeStruct((N_LAT_TOK, C_HEADS * HEAD_DIM), jnp.float32),
        compiler_params=pltpu.CompilerParams(
            dimension_semantics=("arbitrary", "arbitrary", "arbitrary"), vmem_limit_bytes=VMEM_LIMIT_BYTES),
        name="lat_neighbourhood",
    )(p, p, p, cache_k, cache_v, bias_tiles)


def _rope_tables():
    quarter = HEAD_DIM // 4
    t = jnp.arange(DEC_SEQ)
    pos = jnp.stack([t // GRID_W, t % GRID_W], axis=-1).astype(jnp.float32)
    freqs = ROPE_BASE ** (-jnp.arange(quarter, dtype=jnp.float32) / quarter)
    ang = pos[:, :, None] * freqs
    ang = jnp.concatenate([ang, ang], axis=-1).reshape(DEC_SEQ, HEAD_DIM)
    ang = jnp.concatenate([ang] * (LANE // HEAD_DIM), axis=-1)
    return jnp.cos(ang), jnp.sin(ang)


def _final_norm_kernel(x_ref, g_ref, o_ref):
    x = x_ref[...]
    o_ref[...] = x * lax.rsqrt(jnp.mean(x * x, axis=-1, keepdims=True) + RMS_EPS) * g_ref[...]


def _final_norm(x, g):
    n, d = x.shape
    return pl.pallas_call(
        _final_norm_kernel,
        grid=(n // ROW_TILE,),
        in_specs=[pl.BlockSpec((ROW_TILE, d), lambda i: (i, 0)), pl.BlockSpec((1, d), lambda i: (0, 0))],
        out_specs=pl.BlockSpec((ROW_TILE, d), lambda i: (i, 0)),
        out_shape=jax.ShapeDtypeStruct((n, d), jnp.float32),
        compiler_params=pltpu.CompilerParams(dimension_semantics=("arbitrary",)),
        name="final_norm",
    )(x, g)


def kernel(x_prompt, x_sample, cache_a_k, cache_a_v, cache_c_k, cache_c_v, state_rwkv, c, c_ctx, ln1_g, ln2_g, lnf_g, ada_w, ada_b, w_in, a_sink, a_out, rw_mu, rw_w0, rw_w2, rw_a0, rw_a2, rw_g2, rw_kk, rw_ka, rw_rk, rw_lnx_g, rw_lnx_b, rw_out, na_rpb, na_out, w_o, pe_q, pe_subkeys, pe_u, pe_v):
    bf16 = jnp.bfloat16
    x = jnp.concatenate([x_prompt.reshape(N_CTX_TOK, D_MODEL), x_sample.reshape(N_LAT_TOK, D_MODEL)], axis=0)
    silu_all = jnp.concatenate([jax.nn.silu(c_ctx)[None, :], jax.nn.silu(c)], axis=0)
    n_mix = A_COLS + B_COLS + C_COLS
    rope_cos, rope_sin = _rope_tables()
    list_ak, list_av, list_ck, list_cv, list_st = [], [], [], [], []
    for l in range(DEPTH):
        lp = {
            "rw_mu": rw_mu[l], "rw_w0": rw_w0[l], "rw_w2": rw_w2[l], "rw_a0": rw_a0[l],
            "rw_a2": rw_a2[l], "rw_g2": rw_g2[l], "rw_kk": rw_kk[l], "rw_ka": rw_ka[l],
            "rw_rk": rw_rk[l], "rw_lnx_g": rw_lnx_g[l], "rw_lnx_b": rw_lnx_b[l],
        }
        mod = (silu_all @ ada_w[l] + ada_b[l]).reshape(N_SEG, N_MOD, D_MODEL)

        w_in_l = jnp.concatenate([w_in[l][:, n_mix:], w_in[l][:, :n_mix]], axis=1).astype(bf16)
        p, _ = _norm_proj(x, ln1_g[l][None, :], mod, w_in_l, 0, 1, col_tile=IN_COLS // 3)
        sink_rows = jnp.broadcast_to(a_sink[l][:, None], (A_HEADS, LANE))
        flat_cache = lambda cache: cache[:, l].reshape(DEC_BATCH, PAST_LEN, -1)
        ya = jnp.concatenate([
            _ctx_gqa(p, sink_rows),
            _lat_window(p, flat_cache(cache_a_k), flat_cache(cache_a_v), rope_cos, rope_sin, sink_rows)], axis=0)
        yc = jnp.concatenate([
            _ctx_mha(p),
            _lat_neighbourhood(p, flat_cache(cache_c_k), flat_cache(cache_c_v), _na_bias(na_rpb[l]))], axis=0)
        yb, st = _rwkv_mixer(p, lp, state_rwkv[:, l])
        new_cache = lambda col0, heads: p[:N_CTX_TOK, col0:col0 + heads * HEAD_DIM].reshape(BATCH, SEQ, heads, HEAD_DIM)
        list_ak.append(new_cache(COL0_KA, A_KV))
        list_av.append(new_cache(COL0_VA, A_KV))
        list_ck.append(new_cache(COL0_KC, C_HEADS))
        list_cv.append(new_cache(COL0_VC, C_HEADS))
        list_st.append(st)
        x = _merge(x, ya, yb, yc, p, mod,
                   a_out[l].astype(bf16), rw_out[l].astype(bf16), na_out[l].astype(bf16), w_o[l].astype(bf16))

        q, h2 = _norm_proj(x, ln2_g[l][None, :], mod, pe_q[l].astype(bf16), 3, 4,
                           col_tile=PEER_HEADS * PEER_KEY_DIM)
        s1m, s2m, thr, c0 = _peer_route(q, pe_subkeys[l])
        x = _peer_dense(h2, x, mod, s1m, s2m, thr, c0, pe_u[l].astype(bf16), pe_v[l].T.astype(bf16))

    y = _final_norm(x, lnf_g[None, :])
    y_prompt = y[:N_CTX_TOK].reshape(BATCH, SEQ, D_MODEL)
    y_sample = y[N_CTX_TOK:].reshape(DEC_BATCH, DEC_SEQ, D_MODEL)
    return (y_prompt, y_sample, jnp.stack(list_ak, axis=1), jnp.stack(list_av, axis=1),
            jnp.stack(list_ck, axis=1), jnp.stack(list_cv, axis=1), jnp.stack(list_st, axis=1))
```

```python
import functools

import jax
import jax.numpy as jnp
from jax import lax
from jax.experimental import pallas as pl
from jax.experimental.pallas import tpu as pltpu

D_MODEL = 1024
BATCH = 16
SEQ = 256
DEPTH = 4
DEC_BATCH = 2
DEC_SEQ = 2048
PAST_LEN = 512
GRID_W = 64
HEAD_DIM = 64
A_HEADS = 8
A_KV = 2
A_GROUP = A_HEADS // A_KV
WINDOW = 128
Q_BLOCK = 128
B_HEADS = 8
B_WIDTH = B_HEADS * HEAD_DIM
LORA_W = 64
LORA_A = 64
LORA_G = 128
GN_EPS = 64e-5
C_HEADS = 8
NA_ROWS = 8
NA_COLS = 16
N_BRANCH = 3
A_COLS = (A_HEADS + 2 * A_KV) * HEAD_DIM
B_COLS = 3 * B_WIDTH + 2 * LORA_W + 2 * LORA_A + LORA_G
C_COLS = 3 * C_HEADS * HEAD_DIM
IN_COLS = A_COLS + B_COLS + C_COLS + N_BRANCH * D_MODEL
PEER_HEADS = 8
N_KEYS = 128
N_EXPERTS = N_KEYS * N_KEYS
PEER_KEY_DIM = 256
PEER_TOPK = 16
PEER_CHUNK = 128
ROPE_BASE = 10000.0
RMS_EPS = 1e-6
N_MOD = 6

N_CTX_TOK = BATCH * SEQ
N_LAT_TOK = DEC_BATCH * DEC_SEQ
N_TOK = N_CTX_TOK + N_LAT_TOK
N_SEG = 1 + DEC_BATCH

VMEM_LIMIT_BYTES = 48 * 1024 * 1024

ROW_TILE = 512
assert N_CTX_TOK % ROW_TILE == 0 and DEC_SEQ % ROW_TILE == 0


def _segment_of_row_block(i):
    n_ctx_blocks = N_CTX_TOK // ROW_TILE
    blocks_per_lat = DEC_SEQ // ROW_TILE
    return jnp.where(i < n_ctx_blocks, 0, 1 + (i - n_ctx_blocks) // blocks_per_lat)


def _norm_proj_kernel(x_ref, g_ref, mod_ref, w_ref, o_ref, h_ref, *, shift_idx, scale_idx):
    @pl.when(pl.program_id(1) == 0)
    def _():
        x = x_ref[...]
        y = x * lax.rsqrt(jnp.mean(x * x, axis=-1, keepdims=True) + RMS_EPS) * g_ref[...]
        h = y * (1.0 + mod_ref[0, scale_idx:scale_idx + 1, :]) + mod_ref[0, shift_idx:shift_idx + 1, :]
        h_ref[...] = h.astype(jnp.bfloat16)

    o_ref[...] = jnp.dot(h_ref[...], w_ref[...], preferred_element_type=jnp.float32)


def _norm_proj(x, g, mod, w_bf16, shift_idx, scale_idx, col_tile):
    n_rows, d = x.shape
    n_cols = w_bf16.shape[1]
    assert n_cols % col_tile == 0
    return pl.pallas_call(
        functools.partial(_norm_proj_kernel, shift_idx=shift_idx, scale_idx=scale_idx),
        grid=(n_rows // ROW_TILE, n_cols // col_tile),
        in_specs=[
            pl.BlockSpec((ROW_TILE, d), lambda i, j: (i, 0)),
            pl.BlockSpec((1, d), lambda i, j: (0, 0)),
            pl.BlockSpec((1, N_MOD, d), lambda i, j: (_segment_of_row_block(i), 0, 0)),
            pl.BlockSpec((d, col_tile), lambda i, j: (0, j)),
        ],
        out_specs=[pl.BlockSpec((ROW_TILE, col_tile), lambda i, j: (i, j)),
                   pl.BlockSpec((ROW_TILE, d), lambda i, j: (i, 0))],
        out_shape=[jax.ShapeDtypeStruct((n_rows, n_cols), jnp.float32),
                   jax.ShapeDtypeStruct((n_rows, d), jnp.bfloat16)],
        compiler_params=pltpu.CompilerParams(
            dimension_semantics=("arbitrary", "arbitrary"), vmem_limit_bytes=VMEM_LIMIT_BYTES),
        name="norm_proj",
    )(x, g, mod, w_bf16)


LANE = 128
ROUTE_TOK = 512
PEER_TOK = 512
PEER_EXP = 1024
NEG_INF = float("-inf")
LOG2_E = 1.4426950408889634


def _take_top(vals, k):
    rows = lax.broadcasted_iota(jnp.int32, vals.shape, 0)
    taken = []
    for _ in range(k):
        m = jnp.max(vals, axis=0, keepdims=True)
        first = jnp.min(jnp.where(vals == m, rows, vals.shape[0]), axis=0, keepdims=True)
        vals = jnp.where(rows == first, NEG_INF, vals)
        taken.append(m)
    return jnp.concatenate(taken, axis=0), vals


def _peer_route_kernel(q_ref, sk_ref, s1m_ref, s2m_ref, thr_ref, c0_ref):
    def sub_tile(t, carry):
        tok = pl.ds(pl.multiple_of(t * LANE, LANE), LANE)
        masked, top_vals = [], []
        for z in range(2):
            qz = q_ref[tok, z * LANE:(z + 1) * LANE]
            s = lax.dot_general(sk_ref[0, z], qz, (((1,), (1,)), ((), ())),
                                preferred_element_type=jnp.float32)
            v, rest = _take_top(s, PEER_TOPK)
            masked.append(jnp.where(rest == NEG_INF, s, NEG_INF))
            top_vals.append(v)
        half = PEER_TOPK // 2
        v1, v2 = top_vals
        cand = jnp.concatenate(
            [v1[0:1] + v2] + [v1[i:i + 1] + v2[:half] for i in range(1, half)] + [v1[half:] + v2[0:1]], axis=0)
        sv, _ = _take_top(cand, PEER_TOPK)
        z_sum = jnp.sum(jnp.exp(sv - sv[0:1]), axis=0, keepdims=True)
        s1m_ref[0, :, tok] = masked[0]
        s2m_ref[0, :, tok] = masked[1]
        thr_ref[0, :, tok] = sv[PEER_TOPK - 1:PEER_TOPK]
        c0_ref[0, :, tok] = -sv[0:1] - jnp.log(z_sum)
        return carry

    lax.fori_loop(0, ROUTE_TOK // LANE, sub_tile, 0)


def _peer_route(q, subkeys):
    n = q.shape[0]
    big = pl.BlockSpec((1, N_KEYS, ROUTE_TOK), lambda i, h: (h, 0, i))
    small = pl.BlockSpec((1, 1, ROUTE_TOK), lambda i, h: (h, 0, i))
    return pl.pallas_call(
        _peer_route_kernel,
        grid=(n // ROUTE_TOK, PEER_HEADS),
        in_specs=[
            pl.BlockSpec((ROUTE_TOK, PEER_KEY_DIM), lambda i, h: (i, h)),
            pl.BlockSpec((1, 2, N_KEYS, PEER_KEY_DIM // 2), lambda i, h: (h, 0, 0, 0)),
        ],
        out_specs=[big, big, small, small],
        out_shape=[jax.ShapeDtypeStruct((PEER_HEADS, N_KEYS, n), jnp.float32)] * 2
        + [jax.ShapeDtypeStruct((PEER_HEADS, 1, n), jnp.float32)] * 2,
        compiler_params=pltpu.CompilerParams(
            dimension_semantics=("arbitrary", "arbitrary"), vmem_limit_bytes=VMEM_LIMIT_BYTES),
        name="peer_route",
    )(q, subkeys)


def _gelu_tanh(x):
    return 0.5 * x * (1.0 + jnp.tanh(0.7978845608028654 * (x + 0.044715 * (x * x * x))))


def _peer_dense_kernel(h_ref, x_ref, mod_ref, s1m_ref, s2m_ref, thr_ref, c0_ref, u_ref, vt_ref, o_ref,
                       ht_scr, s1l_scr, s2l_scr, coef_scr, acc_scr, *, gate_idx):
    e = pl.program_id(1)

    @pl.when(e == 0)
    def _():
        ht_scr[...] = h_ref[...].astype(jnp.float32).T.astype(jnp.bfloat16)
        s1l_scr[...] = s1m_ref[...] * LOG2_E
        s2l_scr[...] = (s2m_ref[...] + c0_ref[...]) * LOG2_E
        acc_scr[...] = jnp.zeros_like(acc_scr)

    act = jnp.dot(u_ref[...], ht_scr[...], preferred_element_type=jnp.float32)
    a_per_step = PEER_EXP // N_KEYS
    a_rows = pl.ds(pl.multiple_of(e * a_per_step, a_per_step), a_per_step)
    for t in range(PEER_TOK // LANE):
        tok = slice(t * LANE, (t + 1) * LANE)
        s1_all = [s1m_ref[hh, a_rows, tok] for hh in range(PEER_HEADS)]
        s1l_all = [s1l_scr[hh, a_rows, tok] for hh in range(PEER_HEADS)]
        for a in range(a_per_step):
            rows = slice(a * N_KEYS, (a + 1) * N_KEYS)
            w = jnp.zeros((N_KEYS, LANE), jnp.float32)
            for hh in range(PEER_HEADS):
                s1 = s1_all[hh][a:a + 1]
                picked = (s1 + s2m_ref[hh, :, tok]) >= thr_ref[hh, :, tok]
                w = w + jnp.where(picked, jnp.exp2(s1l_all[hh][a:a + 1] + s2l_scr[hh, :, tok]), 0.0)
            coef_scr[rows, tok] = (w * _gelu_tanh(act[rows, tok])).astype(jnp.bfloat16)
    acc_scr[...] += jnp.dot(vt_ref[...], coef_scr[...], preferred_element_type=jnp.float32)

    @pl.when(e == pl.num_programs(1) - 1)
    def _():
        o_ref[...] = x_ref[...] + mod_ref[0, gate_idx:gate_idx + 1, :] * acc_scr[...].T


def _peer_dense(h, x, mod, s1m, s2m, thr, c0, u_bf16, vt_bf16):
    n, d = x.shape
    assert PEER_TOK == ROW_TILE
    tok_row = lambda i, e: (i, 0)
    big = pl.BlockSpec((PEER_HEADS, N_KEYS, PEER_TOK), lambda i, e: (0, 0, i))
    small = pl.BlockSpec((PEER_HEADS, 1, PEER_TOK), lambda i, e: (0, 0, i))
    return pl.pallas_call(
        functools.partial(_peer_dense_kernel, gate_idx=5),
        grid=(n // PEER_TOK, N_EXPERTS // PEER_EXP),
        in_specs=[
            pl.BlockSpec((PEER_TOK, d), tok_row),
            pl.BlockSpec((PEER_TOK, d), tok_row),
            pl.BlockSpec((1, N_MOD, d), lambda i, e: (_segment_of_row_block(i), 0, 0)),
            big, big, small, small,
            pl.BlockSpec((PEER_EXP, d), lambda i, e: (e, 0)),
            pl.BlockSpec((d, PEER_EXP), lambda i, e: (0, e)),
        ],
        out_specs=pl.BlockSpec((PEER_TOK, d), tok_row),
        out_shape=jax.ShapeDtypeStruct((n, d), jnp.float32),
        scratch_shapes=[
            pltpu.VMEM((d, PEER_TOK), jnp.bfloat16),
            pltpu.VMEM((PEER_HEADS, N_KEYS, PEER_TOK), jnp.float32),
            pltpu.VMEM((PEER_HEADS, N_KEYS, PEER_TOK), jnp.float32),
            pltpu.VMEM((PEER_EXP, PEER_TOK), jnp.bfloat16),
            pltpu.VMEM((d, PEER_TOK), jnp.float32),
        ],
        compiler_params=pltpu.CompilerParams(
            dimension_semantics=("arbitrary", "arbitrary"), vmem_limit_bytes=VMEM_LIMIT_BYTES),
        name="peer_dense",
    )(h, x, mod, s1m, s2m, thr, c0, u_bf16, vt_bf16)


def _merge_kernel(x_ref, ya_ref, yb_ref, yc_ref, ga_ref, gb_ref, gc_ref, mod_ref,
                  wa_ref, wb_ref, wc_ref, wo_ref, o_ref, *, gate_idx):
    def branch(y_ref, gate_ref, w_ref):
        proj = jnp.dot(y_ref[...].astype(jnp.bfloat16), w_ref[...], preferred_element_type=jnp.float32)
        return jax.nn.sigmoid(gate_ref[...]) * proj

    merged = branch(ya_ref, ga_ref, wa_ref) + branch(yb_ref, gb_ref, wb_ref) + branch(yc_ref, gc_ref, wc_ref)
    out = jnp.dot(merged.astype(jnp.bfloat16), wo_ref[...], preferred_element_type=jnp.float32)
    o_ref[...] = x_ref[...] + mod_ref[0, gate_idx:gate_idx + 1, :] * out


def _merge(x, ya, yb, yc, p, mod, wa, wb, wc, wo):
    n_rows, d = x.shape
    gate_col0 = 0
    row = lambda i: (i, 0)
    const = lambda i: (0, 0)
    y_spec = pl.BlockSpec((ROW_TILE, ya.shape[1]), row)
    w_spec = pl.BlockSpec(wa.shape, const)
    return pl.pallas_call(
        functools.partial(_merge_kernel, gate_idx=2),
        grid=(n_rows // ROW_TILE,),
        in_specs=[
            pl.BlockSpec((ROW_TILE, d), row), y_spec, y_spec, y_spec,
            pl.BlockSpec((ROW_TILE, d), lambda i: (i, gate_col0)),
            pl.BlockSpec((ROW_TILE, d), lambda i: (i, gate_col0 + 1)),
            pl.BlockSpec((ROW_TILE, d), lambda i: (i, gate_col0 + 2)),
            pl.BlockSpec((1, N_MOD, d), lambda i: (_segment_of_row_block(i), 0, 0)),
            w_spec, w_spec, w_spec, pl.BlockSpec(wo.shape, const),
        ],
        out_specs=pl.BlockSpec((ROW_TILE, d), row),
        out_shape=jax.ShapeDtypeStruct((n_rows, d), jnp.float32),
        compiler_params=pltpu.CompilerParams(
            dimension_semantics=("arbitrary",), vmem_limit_bytes=VMEM_LIMIT_BYTES),
        name="merge_out_proj",
    )(x, ya, yb, yc, p, p, p, mod, wa, wb, wc, wo)


HEAD_PAIRS = B_HEADS // 2
PREP_TOK = 256
SCAN_CHUNK = 128
assert PREP_TOK == SEQ and DEC_SEQ % PREP_TOK == 0 and 2 * HEAD_DIM == LANE
_HI = lax.Precision.HIGHEST


def _head_ones(n):
    r = lax.broadcasted_iota(jnp.int32, (n, n), 0) // HEAD_DIM
    c = lax.broadcasted_iota(jnp.int32, (n, n), 1) // HEAD_DIM
    return (r == c).astype(jnp.float32)


def _softplus(y):
    return jnp.maximum(y, 0.0) + jnp.log(1.0 + jnp.exp(-jnp.abs(y)))


def _rwkv_prep_kernel(pb_ref, prev_ref, next_ref, mu_ref, w0_ref, w2_ref, a0_ref, a2_ref, g2_ref,
                      kks_ref, ka_ref, rk_ref, ones_ref,
                      r_o, kk_o, v_o, w0_o, w1_o, k0_o, k1_o, nb0_o, nb1_o, g_o, bonus_o):
    i = pl.program_id(0)
    n_ctx = N_CTX_TOK // PREP_TOK
    per_lat = DEC_SEQ // PREP_TOK
    j = (i - n_ctx) % per_lat
    starts_seq = jnp.logical_or(i < n_ctx, j == 0)
    ends_seq = jnp.logical_or(i < n_ctx, j == per_lat - 1)

    pb = pb_ref[...]
    rows = lax.broadcasted_iota(jnp.int32, (PREP_TOK, 1), 0)
    before = jnp.where(starts_seq, 0.0, prev_ref[7:8, :])
    after = jnp.where(ends_seq, 0.0, next_ref[0:1, :])
    prev = jnp.where(rows == 0, before, pltpu.roll(pb, 1, 0))
    nxt = jnp.where(rows == PREP_TOK - 1, after, pltpu.roll(pb, PREP_TOK - 1, 0))
    xb = pb + mu_ref[...] * (0.5 * (prev + nxt) - pb)

    w = B_WIDTH
    r, k, v = xb[:, 0:w], xb[:, w:2 * w], xb[:, 2 * w:3 * w]
    wd = xb[:, 3 * w:3 * w + 2 * LORA_W]
    ad = xb[:, 3 * w + 2 * LORA_W:3 * w + 2 * LORA_W + 2 * LORA_A]
    gd = xb[:, 3 * w + 2 * LORA_W + 2 * LORA_A:]
    ones = ones_ref[...]

    def head_sum(x):
        return jnp.dot(x, ones, preferred_element_type=jnp.float32, precision=_HI)

    lw = jnp.dot(jnp.tanh(wd), w2_ref[...], preferred_element_type=jnp.float32, precision=_HI)
    la = jnp.dot(ad, a2_ref[...], preferred_element_type=jnp.float32, precision=_HI)
    kk = k * kks_ref[...]
    kk = kk * lax.rsqrt(head_sum(kk * kk) + 1e-12)
    kd_sum = jnp.zeros_like(k)
    for z, (w_o, k_o, nb_o) in enumerate(((w0_o, k0_o, nb0_o), (w1_o, k1_o, nb1_o))):
        wlog = -_softplus(-(w0_ref[z:z + 1, :] + lw[:, z * w:(z + 1) * w])) - 0.5
        a = jax.nn.sigmoid(a0_ref[z:z + 1, :] + la[:, z * w:(z + 1) * w])
        kd = k * (1.0 + (a - 1.0) * ka_ref[...])
        kd_sum = kd_sum + kd
        w_o[...] = jnp.exp(-jnp.exp(wlog))
        k_o[...] = kd
        nb_o[...] = -(kk * a)
    r_o[...] = r
    kk_o[...] = kk
    v_o[...] = v
    g_o[...] = jnp.dot(jax.nn.sigmoid(gd), g2_ref[...], preferred_element_type=jnp.float32, precision=_HI)
    bonus_o[...] = head_sum(r * kd_sum * rk_ref[...]) * v


def _rwkv_prep(p, mu, w0, w2_pad, a0, a2_pad, g2, kk_scale, ka, rk):
    n = p.shape[0]
    col_blk = (N_BRANCH * D_MODEL + A_COLS) // B_COLS
    assert col_blk * B_COLS == N_BRANCH * D_MODEL + A_COLS
    halo = 8
    per = PREP_TOK // halo
    full = lambda a: pl.BlockSpec(a.shape, lambda i: (0,) * a.ndim)
    ones = _head_ones(B_WIDTH)
    consts = (mu, w0, w2_pad, a0, a2_pad, g2, kk_scale, ka, rk, ones)
    out_spec = pl.BlockSpec((PREP_TOK, B_WIDTH), lambda i: (i, 0))
    return pl.pallas_call(
        _rwkv_prep_kernel,
        grid=(n // PREP_TOK,),
        in_specs=[
            pl.BlockSpec((PREP_TOK, B_COLS), lambda i: (i, col_blk)),
            pl.BlockSpec((halo, B_COLS), lambda i: (jnp.maximum(i * per - 1, 0), col_blk)),
            pl.BlockSpec((halo, B_COLS), lambda i: (jnp.minimum((i + 1) * per, n // halo - 1), col_blk)),
        ] + [full(a) for a in consts],
        out_specs=[out_spec] * 11,
        out_shape=[jax.ShapeDtypeStruct((n, B_WIDTH), jnp.float32)] * 11,
        compiler_params=pltpu.CompilerParams(
            dimension_semantics=("arbitrary",), vmem_limit_bytes=VMEM_LIMIT_BYTES),
        name="rwkv_prep",
    )(p, p, p, *consts)


SCAN_LANES = 2 * LANE


def _head_sums(head_ones, *terms):
    sums = []
    for x, passes in terms:
        hi = x.astype(jnp.bfloat16)
        out = jnp.dot(hi, head_ones, preferred_element_type=jnp.float32)
        if passes == 2:
            lo = (x - hi.astype(jnp.float32)).astype(jnp.bfloat16)
            out = out + jnp.dot(lo, head_ones, preferred_element_type=jnp.float32)
        sums.append(out)
    return sums


def _rwkv_scan_kernel(rf, kkf, vf, wf, kf, nbf, rb, kkb, vb, wb, kb, nbb, s0_ref, ones_ref, eye_ref,
                      yf_ref, yb_ref, sfin_ref, s_scr, *, n_seq):
    c = pl.program_id(1)

    @pl.when(c == 0)
    def _():
        s_scr[...] = s0_ref[...]

    head_ones = ones_ref[...]
    eye = eye_ref[...]
    fwd_refs = (rf, kkf, vf, wf, kf, nbf)
    bwd_refs = (rb, kkb, vb, wb, kb, nbb)
    R, KK, V, W, K, NB = range(6)

    def step(t, carry):
        t_f = pl.ds(t, 1)
        t_b = pl.ds(SCAN_CHUNK - 1 - t, 1)
        rows_f = [[a[g, t_f, :] for a in fwd_refs] for g in range(n_seq)]
        rows_b = [[a[g, t_b, :] for a in bwd_refs] for g in range(n_seq)]

        def both(which, chain):
            g, p = chain
            lanes = slice(p * LANE, (p + 1) * LANE)
            return jnp.concatenate([rows_f[g][which][:, lanes], rows_b[g][which][:, lanes]], axis=1)

        chains = [(g, p) for g in range(n_seq) for p in range(HEAD_PAIRS)]
        rows = {ch: slice(i * HEAD_DIM, (i + 1) * HEAD_DIM) for i, ch in enumerate(chains)}
        s_old = {ch: s_scr[ch[0], ch[1]] for ch in chains}
        s_kk, v_col = _head_sums(
            head_ones,
            (jnp.concatenate([s_old[ch] * both(KK, ch) for ch in chains], axis=0), 1),
            (jnp.concatenate([eye * both(V, ch) for ch in chains], axis=0), 1))
        s_new = {}
        for ch in chains:
            s = s_old[ch] * both(W, ch) + s_kk[rows[ch]] * both(NB, ch) + v_col[rows[ch]] * both(K, ch)
            s_scr[ch[0], ch[1]] = s
            s_new[ch] = s
        (y_all,) = _head_sums(
            head_ones, (jnp.concatenate([s_new[ch] * both(R, ch) for ch in chains], axis=0), 1))
        for g in range(n_seq):
            y_rows = [jnp.sum(y_all[rows[(g, p)]] * eye, axis=0, keepdims=True) for p in range(HEAD_PAIRS)]
            yf_ref[g, t_f, :] = jnp.concatenate([y[:, :LANE] for y in y_rows], axis=1)
            yb_ref[g, t_b, :] = jnp.concatenate([y[:, LANE:] for y in y_rows], axis=1)
        return carry

    lax.fori_loop(0, SCAN_CHUNK, step, 0, unroll=2)

    @pl.when(c == pl.num_programs(1) - 1)
    def _():
        sfin_ref[...] = s_scr[...]


def _rwkv_scan(seqs, s0, seq_group):
    r, kk, v, w0, w1, k0, k1, nb0, nb1 = seqs
    n_seq, t_len, _ = r.shape
    n_chunks = t_len // SCAN_CHUNK
    assert n_seq % seq_group == 0 and n_chunks * SCAN_CHUNK == t_len
    blk = (seq_group, SCAN_CHUNK, B_WIDTH)
    fwd = pl.BlockSpec(blk, lambda i, c: (i, c, 0))
    bwd = pl.BlockSpec(blk, lambda i, c: (i, n_chunks - 1 - c, 0))
    state_blk = (seq_group, HEAD_PAIRS, HEAD_DIM, SCAN_LANES)
    state = pl.BlockSpec(state_blk, lambda i, c: (i, 0, 0, 0))
    head_ones = _head_ones(SCAN_LANES).astype(jnp.bfloat16)
    eye = (lax.broadcasted_iota(jnp.int32, (HEAD_DIM, SCAN_LANES), 0)
           == lax.broadcasted_iota(jnp.int32, (HEAD_DIM, SCAN_LANES), 1) % HEAD_DIM).astype(jnp.float32)
    const = lambda a: pl.BlockSpec(a.shape, lambda i, c: (0, 0))
    y_shape = jax.ShapeDtypeStruct((n_seq, t_len, B_WIDTH), jnp.float32)
    return pl.pallas_call(
        functools.partial(_rwkv_scan_kernel, n_seq=seq_group),
        grid=(n_seq // seq_group, n_chunks),
        in_specs=[fwd] * 6 + [bwd] * 6 + [state, const(head_ones), const(eye)],
        out_specs=[fwd, bwd, state],
        out_shape=[y_shape, y_shape, jax.ShapeDtypeStruct(s0.shape, jnp.float32)],
        scratch_shapes=[pltpu.VMEM(state_blk, jnp.float32)],
        compiler_params=pltpu.CompilerParams(
            dimension_semantics=("arbitrary", "arbitrary"), vmem_limit_bytes=VMEM_LIMIT_BYTES),
        name="rwkv_scan",
    )(r, kk, v, w0, k0, nb0, r, kk, v, w1, k1, nb1, s0, head_ones, eye)


def _rwkv_post_kernel(yf_ref, yb_ref, g_ref, bonus_ref, lng_ref, lnb_ref, ones_ref, o_ref):
    ones = ones_ref[...]
    y = yf_ref[...] + yb_ref[...]
    mean = jnp.dot(y, ones, preferred_element_type=jnp.float32, precision=_HI) * (1.0 / HEAD_DIM)
    d = y - mean
    var = jnp.dot(d * d, ones, preferred_element_type=jnp.float32, precision=_HI) * (1.0 / HEAD_DIM)
    y = d * lax.rsqrt(var + GN_EPS) * lng_ref[...] + lnb_ref[...]
    o_ref[...] = (y + bonus_ref[...]) * g_ref[...]


def _rwkv_post(yf, yb, g, bonus, lnx_g, lnx_b):
    n = yf.shape[0]
    row = pl.BlockSpec((PREP_TOK, B_WIDTH), lambda i: (i, 0))
    vec = pl.BlockSpec((1, B_WIDTH), lambda i: (0, 0))
    ones = _head_ones(B_WIDTH)
    return pl.pallas_call(
        _rwkv_post_kernel,
        grid=(n // PREP_TOK,),
        in_specs=[row, row, row, row, vec, vec, pl.BlockSpec(ones.shape, lambda i: (0, 0))],
        out_specs=row,
        out_shape=jax.ShapeDtypeStruct((n, B_WIDTH), jnp.float32),
        compiler_params=pltpu.CompilerParams(
            dimension_semantics=("arbitrary",), vmem_limit_bytes=VMEM_LIMIT_BYTES),
        name="rwkv_post",
    )(yf, yb, g, bonus, lnx_g, lnx_b, ones)


def _state_to_pairs(s):
    b = s.shape[0]
    s = s.reshape(b, 2, HEAD_PAIRS, 2, HEAD_DIM, HEAD_DIM)
    return jnp.transpose(s, (0, 2, 4, 1, 3, 5)).reshape(b, HEAD_PAIRS, HEAD_DIM, SCAN_LANES)


def _state_from_pairs(s):
    b = s.shape[0]
    s = s.reshape(b, HEAD_PAIRS, HEAD_DIM, 2, 2, HEAD_DIM)
    return jnp.transpose(s, (0, 3, 1, 4, 2, 5)).reshape(b, 2, B_HEADS, HEAD_DIM, HEAD_DIM)


def _rwkv_mixer(p, lp, state_lat):
    zeros = jnp.zeros((LORA_W, B_WIDTH), jnp.float32)
    w2_pad = jnp.concatenate([jnp.concatenate([lp["rw_w2"][0], zeros], axis=1),
                              jnp.concatenate([zeros, lp["rw_w2"][1]], axis=1)], axis=0)
    a2_pad = jnp.concatenate([jnp.concatenate([lp["rw_a2"][0], zeros], axis=1),
                              jnp.concatenate([zeros, lp["rw_a2"][1]], axis=1)], axis=0)
    row = lambda a: a.reshape(1, -1)
    prep = _rwkv_prep(p, row(lp["rw_mu"]), lp["rw_w0"], w2_pad, lp["rw_a0"], a2_pad, lp["rw_g2"],
                      row(lp["rw_kk"]), row(lp["rw_ka"]), row(lp["rw_rk"]))
    scan_in, (g, bonus) = prep[:9], prep[9:]
    ctx_in = [a[:N_CTX_TOK].reshape(BATCH, SEQ, B_WIDTH) for a in scan_in]
    lat_in = [a[N_CTX_TOK:].reshape(DEC_BATCH, DEC_SEQ, B_WIDTH) for a in scan_in]
    s0_ctx = jnp.zeros((BATCH, HEAD_PAIRS, HEAD_DIM, SCAN_LANES), jnp.float32)
    yf_c, yb_c, s_ctx = _rwkv_scan(ctx_in, s0_ctx, seq_group=4)
    yf_l, yb_l, _ = _rwkv_scan(lat_in, _state_to_pairs(state_lat), seq_group=DEC_BATCH)
    flat = lambda u, v: jnp.concatenate([u.reshape(N_CTX_TOK, B_WIDTH), v.reshape(N_LAT_TOK, B_WIDTH)], axis=0)
    y = _rwkv_post(flat(yf_c, yf_l), flat(yb_c, yb_l), g, bonus, row(lp["rw_lnx_g"]), row(lp["rw_lnx_b"]))
    return y, _state_from_pairs(s_ctx)


ATT_SCALE = HEAD_DIM ** -0.5
COL0_QA = N_BRANCH * D_MODEL
COL0_KA = COL0_QA + A_HEADS * HEAD_DIM
COL0_VA = COL0_KA + A_KV * HEAD_DIM
COL0_QC = COL0_QA + A_COLS + B_COLS
COL0_KC = COL0_QC + C_HEADS * HEAD_DIM
COL0_VC = COL0_KC + C_HEADS * HEAD_DIM
A_GROUP_COLS = A_GROUP * HEAD_DIM
assert A_KV * HEAD_DIM == LANE and all(c % LANE == 0 for c in (COL0_QA, COL0_KA, COL0_VA, COL0_QC, COL0_KC, COL0_VC))
assert COL0_QA % A_GROUP_COLS == 0
GRID_ROWS = DEC_SEQ // GRID_W
NA_WIN_ROWS = min(NA_ROWS, GRID_ROWS)


def _attend(q, parts, sink=None):
    scores = []
    for k, _, bias in parts:
        s = lax.dot_general(q, k, (((1,), (1,)), ((), ())), preferred_element_type=jnp.float32)
        scores.append(s if bias is None else s + bias)
    m = functools.reduce(jnp.maximum, [jnp.max(s, axis=-1, keepdims=True) for s in scores])
    den = 0.0
    if sink is not None:
        m = jnp.maximum(m, sink)
        den = jnp.exp(sink - m)
    out = 0.0
    for s, (_, v, _) in zip(scores, parts):
        prob = jnp.exp(s - m)
        den = den + jnp.sum(prob, axis=-1, keepdims=True)
        out = out + jnp.dot(prob.astype(jnp.bfloat16), v, preferred_element_type=jnp.float32)
    return out / den


def _scaled_bf16(q):
    return (q * ATT_SCALE).astype(jnp.bfloat16)


def _kv_half(x, kv):
    return jnp.where(kv == 0, x[:, :HEAD_DIM], x[:, HEAD_DIM:])


def _ctx_gqa_kernel(q_ref, k_ref, v_ref, sink_ref, o_ref):
    kv = pl.program_id(1)
    k = _kv_half(k_ref[...], kv).astype(jnp.bfloat16)
    v = _kv_half(v_ref[...], kv).astype(jnp.bfloat16)
    outs = []
    for g in range(A_GROUP):
        q = _scaled_bf16(q_ref[:, g * HEAD_DIM:(g + 1) * HEAD_DIM])
        sink = sink_ref[pl.ds(kv * A_GROUP + g, 1), :][:, 0:1]
        outs.append(_attend(q, [(k, v, None)], sink))
    o_ref[...] = jnp.concatenate(outs, axis=1)


def _ctx_gqa(p, sink_rows):
    q0, k0, v0 = COL0_QA // A_GROUP_COLS, COL0_KA // LANE, COL0_VA // LANE
    return pl.pallas_call(
        _ctx_gqa_kernel,
        grid=(BATCH, A_KV),
        in_specs=[
            pl.BlockSpec((SEQ, A_GROUP_COLS), lambda b, kv: (b, q0 + kv)),
            pl.BlockSpec((SEQ, LANE), lambda b, kv: (b, k0)),
            pl.BlockSpec((SEQ, LANE), lambda b, kv: (b, v0)),
            pl.BlockSpec(sink_rows.shape, lambda b, kv: (0, 0)),
        ],
        out_specs=pl.BlockSpec((SEQ, A_GROUP_COLS), lambda b, kv: (b, kv)),
        out_shape=jax.ShapeDtypeStruct((N_CTX_TOK, A_HEADS * HEAD_DIM), jnp.float32),
        compiler_params=pltpu.CompilerParams(
            dimension_semantics=("arbitrary", "arbitrary"), vmem_limit_bytes=VMEM_LIMIT_BYTES),
        name="ctx_gqa",
    )(p, p, p, sink_rows)


def _ctx_mha_kernel(q_ref, k_ref, v_ref, o_ref):
    outs = []
    for j in range(2):
        lanes = slice(j * HEAD_DIM, (j + 1) * HEAD_DIM)
        outs.append(_attend(_scaled_bf16(q_ref[:, lanes]),
                            [(k_ref[:, lanes].astype(jnp.bfloat16), v_ref[:, lanes].astype(jnp.bfloat16), None)]))
    o_ref[...] = jnp.concatenate(outs, axis=1)


def _ctx_mha(p):
    q0, k0, v0 = COL0_QC // LANE, COL0_KC // LANE, COL0_VC // LANE
    return pl.pallas_call(
        _ctx_mha_kernel,
        grid=(BATCH, C_HEADS // 2),
        in_specs=[
            pl.BlockSpec((SEQ, LANE), lambda b, hp: (b, q0 + hp)),
            pl.BlockSpec((SEQ, LANE), lambda b, hp: (b, k0 + hp)),
            pl.BlockSpec((SEQ, LANE), lambda b, hp: (b, v0 + hp)),
        ],
        out_specs=pl.BlockSpec((SEQ, LANE), lambda b, hp: (b, hp)),
        out_shape=jax.ShapeDtypeStruct((N_CTX_TOK, C_HEADS * HEAD_DIM), jnp.float32),
        compiler_params=pltpu.CompilerParams(
            dimension_semantics=("arbitrary", "arbitrary"), vmem_limit_bytes=VMEM_LIMIT_BYTES),
        name="ctx_mha",
    )(p, p, p)


def _rope(x, cos, sin):
    reps = x.shape[1] // LANE
    if reps > 1:
        cos = jnp.concatenate([cos] * reps, axis=1)
        sin = jnp.concatenate([sin] * reps, axis=1)
    lane = lax.broadcasted_iota(jnp.int32, x.shape, 1)
    first = (lane % (HEAD_DIM // 2)) < (HEAD_DIM // 4)
    quarter = HEAD_DIM // 4
    pieces = [x[:, i * LANE:(i + 1) * LANE] for i in range(reps)]
    up = jnp.concatenate([pltpu.roll(piece, LANE - quarter, 1) for piece in pieces], axis=1)
    down = jnp.concatenate([pltpu.roll(piece, quarter, 1) for piece in pieces], axis=1)
    return x * cos + jnp.where(first, -up, down) * sin


def _lat_window_kernel(q_ref, kp_ref, kc_ref, kn_ref, vp_ref, vc_ref, vn_ref, ck_ref, cv_ref,
                       cosq_ref, sinq_ref, cosp_ref, sinp_ref, cosn_ref, sinn_ref, sink_ref, o_ref):
    qb = pl.program_id(1)
    kv = pl.program_id(2)
    n_blocks = pl.num_programs(1)
    row = lax.broadcasted_iota(jnp.int32, (Q_BLOCK, Q_BLOCK), 0)
    col = lax.broadcasted_iota(jnp.int32, (Q_BLOCK, Q_BLOCK), 1)
    bias_prev = jnp.where(jnp.logical_and(col >= row, qb > 0), 0.0, NEG_INF)
    bias_next = jnp.where(jnp.logical_and(col <= row, qb < n_blocks - 1), 0.0, NEG_INF)
    assert WINDOW == Q_BLOCK

    def keys(k_ref, cos_ref, sin_ref):
        return _kv_half(_rope(k_ref[...], cos_ref[...], sin_ref[...]), kv).astype(jnp.bfloat16)

    def vals(v_ref):
        return _kv_half(v_ref[...], kv).astype(jnp.bfloat16)

    parts = [
        (keys(kp_ref, cosp_ref, sinp_ref), vals(vp_ref), bias_prev),
        (keys(kc_ref, cosq_ref, sinq_ref), vals(vc_ref), None),
        (keys(kn_ref, cosn_ref, sinn_ref), vals(vn_ref), bias_next),
        (_kv_half(ck_ref[0], kv).astype(jnp.bfloat16), _kv_half(cv_ref[0], kv).astype(jnp.bfloat16), None),
    ]
    q_all = _rope(q_ref[...], cosq_ref[...], sinq_ref[...])
    outs = []
    for g in range(A_GROUP):
        sink = sink_ref[pl.ds(kv * A_GROUP + g, 1), :][:, 0:1]
        outs.append(_attend(_scaled_bf16(q_all[:, g * HEAD_DIM:(g + 1) * HEAD_DIM]), parts, sink))
    o_ref[...] = jnp.concatenate(outs, axis=1)


def _lat_window(p, cache_k, cache_v, cos, sin, sink_rows):
    n_blocks = DEC_SEQ // Q_BLOCK
    base = N_CTX_TOK // Q_BLOCK
    q0, k0, v0 = COL0_QA // A_GROUP_COLS, COL0_KA // LANE, COL0_VA // LANE
    prev = lambda b, qb, kv: base + b * n_blocks + jnp.maximum(qb - 1, 0)
    here = lambda b, qb, kv: base + b * n_blocks + qb
    nxt = lambda b, qb, kv: base + b * n_blocks + jnp.minimum(qb + 1, n_blocks - 1)
    slab = lambda rows, c0: pl.BlockSpec((Q_BLOCK, LANE), lambda b, qb, kv: (rows(b, qb, kv), c0))
    table = lambda blk: pl.BlockSpec((Q_BLOCK, LANE), lambda b, qb, kv: (blk(b, qb, kv), 0))
    cache = pl.BlockSpec((1, PAST_LEN, LANE), lambda b, qb, kv: (b, 0, 0))
    return pl.pallas_call(
        _lat_window_kernel,
        grid=(DEC_BATCH, n_blocks, A_KV),
        in_specs=[
            pl.BlockSpec((Q_BLOCK, A_GROUP_COLS), lambda b, qb, kv: (here(b, qb, kv), q0 + kv)),
            slab(prev, k0), slab(here, k0), slab(nxt, k0),
            slab(prev, v0), slab(here, v0), slab(nxt, v0),
            cache, cache,
            table(lambda b, qb, kv: qb), table(lambda b, qb, kv: qb),
            table(lambda b, qb, kv: jnp.maximum(qb - 1, 0)), table(lambda b, qb, kv: jnp.maximum(qb - 1, 0)),
            table(lambda b, qb, kv: jnp.minimum(qb + 1, n_blocks - 1)),
            table(lambda b, qb, kv: jnp.minimum(qb + 1, n_blocks - 1)),
            pl.BlockSpec(sink_rows.shape, lambda b, qb, kv: (0, 0)),
        ],
        out_specs=pl.BlockSpec((Q_BLOCK, A_GROUP_COLS), lambda b, qb, kv: (b * n_blocks + qb, kv)),
        out_shape=jax.ShapeDtypeStruct((N_LAT_TOK, A_HEADS * HEAD_DIM), jnp.float32),
        compiler_params=pltpu.CompilerParams(
            dimension_semantics=("arbitrary", "arbitrary", "arbitrary"), vmem_limit_bytes=VMEM_LIMIT_BYTES),
        name="lat_window",
    )(p, p, p, p, p, p, p, cache_k, cache_v, cos, sin, cos, sin, cos, sin, sink_rows)


N_DC = 2 * NA_COLS - 1
N_DR = 2 * NA_ROWS - 1


def _na_bias_kernel(rpb_ref, o_ref):
    h = pl.program_id(0)
    shape = (GRID_W, 2 * GRID_W)
    qc = lax.broadcasted_iota(jnp.int32, shape, 0)
    kc = lax.broadcasted_iota(jnp.int32, shape, 1) % GRID_W
    second = lax.broadcasted_iota(jnp.int32, shape, 1) >= GRID_W
    start = jnp.clip(qc - NA_COLS // 2, 0, GRID_W - NA_COLS)
    inside = jnp.logical_and(kc >= start, kc < start + NA_COLS)
    dc = jnp.clip(kc - qc, -(NA_COLS - 1), NA_COLS - 1) + NA_COLS - 1
    for dr in range(N_DR - 1):
        bias = jnp.zeros(shape, jnp.float32)
        for d in range(N_DC):
            lo = rpb_ref[(h * N_DR + dr) * N_DC + d]
            hi = rpb_ref[(h * N_DR + dr + 1) * N_DC + d]
            bias = jnp.where(dc == d, jnp.where(second, hi, lo), bias)
        o_ref[0, dr] = jnp.where(inside, bias, NEG_INF)


def _na_bias(rpb):
    return pl.pallas_call(
        _na_bias_kernel,
        grid=(C_HEADS,),
        in_specs=[pl.BlockSpec(memory_space=pltpu.SMEM)],
        out_specs=pl.BlockSpec((1, N_DR - 1, GRID_W, 2 * GRID_W), lambda h: (h, 0, 0, 0)),
        out_shape=jax.ShapeDtypeStruct((C_HEADS, N_DR - 1, GRID_W, 2 * GRID_W), jnp.float32),
        compiler_params=pltpu.CompilerParams(dimension_semantics=("arbitrary",)),
        name="na_bias",
    )(rpb.reshape(-1))


def _lat_neighbourhood_kernel(q_ref, k_ref, v_ref, ck_ref, cv_ref, bias_ref, o_ref):
    hp = pl.program_id(1)
    r = pl.program_id(2)
    first_row = jnp.clip(r - NA_WIN_ROWS // 2, 0, GRID_ROWS - NA_WIN_ROWS)
    local = pl.ds(pl.multiple_of(first_row * GRID_W, GRID_W), NA_WIN_ROWS * GRID_W)
    dr0 = first_row - r + NA_ROWS - 1
    outs = []
    for j in range(2):
        lanes = slice(j * HEAD_DIM, (j + 1) * HEAD_DIM)
        bias = jnp.concatenate([bias_ref[hp * 2 + j, dr0 + a] for a in range(0, NA_WIN_ROWS, 2)], axis=1)
        parts = [
            (k_ref[local, :][:, lanes].astype(jnp.bfloat16), v_ref[local, :][:, lanes].astype(jnp.bfloat16), bias),
            (ck_ref[0, :, lanes].astype(jnp.bfloat16), cv_ref[0, :, lanes].astype(jnp.bfloat16), None),
        ]
        outs.append(_attend(_scaled_bf16(q_ref[:, lanes]), parts))
    o_ref[...] = jnp.concatenate(outs, axis=1)


def _lat_neighbourhood(p, cache_k, cache_v, bias_tiles):
    assert NA_WIN_ROWS % 2 == 0 and N_CTX_TOK % DEC_SEQ == 0
    q0, k0, v0 = COL0_QC // LANE, COL0_KC // LANE, COL0_VC // LANE
    row_base = N_CTX_TOK // GRID_W
    seq_base = N_CTX_TOK // DEC_SEQ
    whole = lambda c0: pl.BlockSpec((DEC_SEQ, LANE), lambda b, hp, r: (seq_base + b, c0 + hp))
    cache = pl.BlockSpec((1, PAST_LEN, LANE), lambda b, hp, r: (b, 0, hp))
    return pl.pallas_call(
        _lat_neighbourhood_kernel,
        grid=(DEC_BATCH, C_HEADS // 2, GRID_ROWS),
        in_specs=[
            pl.BlockSpec((GRID_W, LANE), lambda b, hp, r: (row_base + b * GRID_ROWS + r, q0 + hp)),
            whole(k0), whole(v0), cache, cache,
            pl.BlockSpec(bias_tiles.shape, lambda b, hp, r: (0, 0, 0, 0)),
        ],
        out_specs=pl.BlockSpec((GRID_W, LANE), lambda b, hp, r: (b * GRID_ROWS + r, hp)),
        out_shape=jax.ShapeDtypeStruct((N_LAT_TOK, C_HEADS * HEAD_DIM), jnp.float32),
        compiler_params=pltpu.CompilerParams(
            dimension_semantics=("arbitrary", "arbitrary", "arbitrary"), vmem_limit_bytes=VMEM_LIMIT_BYTES),
        name="lat_neighbourhood",
    )(p, p, p, cache_k, cache_v, bias_tiles)


def _rope_tables():
    quarter = HEAD_DIM // 4
    t = jnp.arange(DEC_SEQ)
    pos = jnp.stack([t // GRID_W, t % GRID_W], axis=-1).astype(jnp.float32)
    freqs = ROPE_BASE ** (-jnp.arange(quarter, dtype=jnp.float32) / quarter)
    ang = pos[:, :, None] * freqs
    ang = jnp.concatenate([ang, ang], axis=-1).reshape(DEC_SEQ, HEAD_DIM)
    ang = jnp.concatenate([ang] * (LANE // HEAD_DIM), axis=-1)
    return jnp.cos(ang), jnp.sin(ang)


def _final_norm_kernel(x_ref, g_ref, o_ref):
    x = x_ref[...]
    o_ref[...] = x * lax.rsqrt(jnp.mean(x * x, axis=-1, keepdims=True) + RMS_EPS) * g_ref[...]


def _final_norm(x, g):
    n, d = x.shape
    return pl.pallas_call(
        _final_norm_kernel,
        grid=(n // ROW_TILE,),
        in_specs=[pl.BlockSpec((ROW_TILE, d), lambda i: (i, 0)), pl.BlockSpec((1, d), lambda i: (0, 0))],
        out_specs=pl.BlockSpec((ROW_TILE, d), lambda i: (i, 0)),
        out_shape=jax.ShapeDtypeStruct((n, d), jnp.float32),
        compiler_params=pltpu.CompilerParams(dimension_semantics=("arbitrary",)),
        name="final_norm",
    )(x, g)


def kernel(x_prompt, x_sample, cache_a_k, cache_a_v, cache_c_k, cache_c_v, state_rwkv, c, c_ctx, ln1_g, ln2_g, lnf_g, ada_w, ada_b, w_in, a_sink, a_out, rw_mu, rw_w0, rw_w2, rw_a0, rw_a2, rw_g2, rw_kk, rw_ka, rw_rk, rw_lnx_g, rw_lnx_b, rw_out, na_rpb, na_out, w_o, pe_q, pe_subkeys, pe_u, pe_v):
    bf16 = jnp.bfloat16
    x = jnp.concatenate([x_prompt.reshape(N_CTX_TOK, D_MODEL), x_sample.reshape(N_LAT_TOK, D_MODEL)], axis=0)
    silu_all = jnp.concatenate([jax.nn.silu(c_ctx)[None, :], jax.nn.silu(c)], axis=0)
    n_mix = A_COLS + B_COLS + C_COLS
    rope_cos, rope_sin = _rope_tables()
    list_ak, list_av, list_ck, list_cv, list_st = [], [], [], [], []
    for l in range(DEPTH):
        lp = {
            "rw_mu": rw_mu[l], "rw_w0": rw_w0[l], "rw_w2": rw_w2[l], "rw_a0": rw_a0[l],
            "rw_a2": rw_a2[l], "rw_g2": rw_g2[l], "rw_kk": rw_kk[l], "rw_ka": rw_ka[l],
            "rw_rk": rw_rk[l], "rw_lnx_g": rw_lnx_g[l], "rw_lnx_b": rw_lnx_b[l],
        }
        mod = (silu_all @ ada_w[l] + ada_b[l]).reshape(N_SEG, N_MOD, D_MODEL)

        w_in_l = jnp.concatenate([w_in[l][:, n_mix:], w_in[l][:, :n_mix]], axis=1).astype(bf16)
        p, _ = _norm_proj(x, ln1_g[l][None, :], mod, w_in_l, 0, 1, col_tile=IN_COLS // 3)
        sink_rows = jnp.broadcast_to(a_sink[l][:, None], (A_HEADS, LANE))
        flat_cache = lambda cache: cache[:, l].reshape(DEC_BATCH, PAST_LEN, -1)
        ya = jnp.concatenate([
            _ctx_gqa(p, sink_rows),
            _lat_window(p, flat_cache(cache_a_k), flat_cache(cache_a_v), rope_cos, rope_sin, sink_rows)], axis=0)
        yc = jnp.concatenate([
            _ctx_mha(p),
            _lat_neighbourhood(p, flat_cache(cache_c_k), flat_cache(cache_c_v), _na_bias(na_rpb[l]))], axis=0)
        yb, st = _rwkv_mixer(p, lp, state_rwkv[:, l])
        new_cache = lambda col0, heads: p[:N_CTX_TOK, col0:col0 + heads * HEAD_DIM].reshape(BATCH, SEQ, heads, HEAD_DIM)
        list_ak.append(new_cache(COL0_KA, A_KV))
        list_av.append(new_cache(COL0_VA, A_KV))
        list_ck.append(new_cache(COL0_KC, C_HEADS))
        list_cv.append(new_cache(COL0_VC, C_HEADS))
        list_st.append(st)
        x = _merge(x, ya, yb, yc, p, mod,
                   a_out[l].astype(bf16), rw_out[l].astype(bf16), na_out[l].astype(bf16), w_o[l].astype(bf16))

        q, h2 = _norm_proj(x, ln2_g[l][None, :], mod, pe_q[l].astype(bf16), 3, 4,
                           col_tile=PEER_HEADS * PEER_KEY_DIM)
        s1m, s2m, thr, c0 = _peer_route(q, pe_subkeys[l])
        x = _peer_dense(h2, x, mod, s1m, s2m, thr, c0, pe_u[l].astype(bf16), pe_v[l].T.astype(bf16))

    y = _final_norm(x, lnf_g[None, :])
    y_prompt = y[:N_CTX_TOK].reshape(BATCH, SEQ, D_MODEL)
    y_sample = y[N_CTX_TOK:].reshape(DEC_BATCH, DEC_SEQ, D_MODEL)
    return (y_prompt, y_sample, jnp.stack(list_ak, axis=1), jnp.stack(list_av, axis=1),
            jnp.stack(list_ck, axis=1), jnp.stack(list_cv, axis=1), jnp.stack(list_st, axis=1))
```

```python
import functools

import jax
import jax.numpy as jnp
from jax import lax
from jax.experimental import pallas as pl
from jax.experimental.pallas import tpu as pltpu

D_MODEL = 1024
BATCH = 16
SEQ = 256
DEPTH = 4
DEC_BATCH = 2
DEC_SEQ = 2048
PAST_LEN = 512
GRID_W = 64
HEAD_DIM = 64
A_HEADS = 8
A_KV = 2
A_GROUP = A_HEADS // A_KV
WINDOW = 128
Q_BLOCK = 128
B_HEADS = 8
B_WIDTH = B_HEADS * HEAD_DIM
LORA_W = 64
LORA_A = 64
LORA_G = 128
GN_EPS = 64e-5
C_HEADS = 8
NA_ROWS = 8
NA_COLS = 16
N_BRANCH = 3
A_COLS = (A_HEADS + 2 * A_KV) * HEAD_DIM
B_COLS = 3 * B_WIDTH + 2 * LORA_W + 2 * LORA_A + LORA_G
C_COLS = 3 * C_HEADS * HEAD_DIM
IN_COLS = A_COLS + B_COLS + C_COLS + N_BRANCH * D_MODEL
PEER_HEADS = 8
N_KEYS = 128
N_EXPERTS = N_KEYS * N_KEYS
PEER_KEY_DIM = 256
PEER_TOPK = 16
PEER_CHUNK = 128
ROPE_BASE = 10000.0
RMS_EPS = 1e-6
N_MOD = 6

N_CTX_TOK = BATCH * SEQ
N_LAT_TOK = DEC_BATCH * DEC_SEQ
N_TOK = N_CTX_TOK + N_LAT_TOK
N_SEG = 1 + DEC_BATCH

VMEM_LIMIT_BYTES = 48 * 1024 * 1024

ROW_TILE = 512
assert N_CTX_TOK % ROW_TILE == 0 and DEC_SEQ % ROW_TILE == 0


def _segment_of_row_block(i):
    n_ctx_blocks = N_CTX_TOK // ROW_TILE
    blocks_per_lat = DEC_SEQ // ROW_TILE
    return jnp.where(i < n_ctx_blocks, 0, 1 + (i - n_ctx_blocks) // blocks_per_lat)


def _norm_proj_kernel(x_ref, g_ref, mod_ref, w_ref, o_ref, h_ref, *, shift_idx, scale_idx):
    @pl.when(pl.program_id(1) == 0)
    def _():
        x = x_ref[...]
        y = x * lax.rsqrt(jnp.mean(x * x, axis=-1, keepdims=True) + RMS_EPS) * g_ref[...]
        h = y * (1.0 + mod_ref[0, scale_idx:scale_idx + 1, :]) + mod_ref[0, shift_idx:shift_idx + 1, :]
        h_ref[...] = h.astype(jnp.bfloat16)

    o_ref[...] = jnp.dot(h_ref[...], w_ref[...], preferred_element_type=jnp.float32)


def _norm_proj(x, g, mod, w_bf16, shift_idx, scale_idx, col_tile):
    n_rows, d = x.shape
    n_cols = w_bf16.shape[1]
    assert n_cols % col_tile == 0
    return pl.pallas_call(
        functools.partial(_norm_proj_kernel, shift_idx=shift_idx, scale_idx=scale_idx),
        grid=(n_rows // ROW_TILE, n_cols // col_tile),
        in_specs=[
            pl.BlockSpec((ROW_TILE, d), lambda i, j: (i, 0)),
            pl.BlockSpec((1, d), lambda i, j: (0, 0)),
            pl.BlockSpec((1, N_MOD, d), lambda i, j: (_segment_of_row_block(i), 0, 0)),
            pl.BlockSpec((d, col_tile), lambda i, j: (0, j)),
        ],
        out_specs=[pl.BlockSpec((ROW_TILE, col_tile), lambda i, j: (i, j)),
                   pl.BlockSpec((ROW_TILE, d), lambda i, j: (i, 0))],
        out_shape=[jax.ShapeDtypeStruct((n_rows, n_cols), jnp.float32),
                   jax.ShapeDtypeStruct((n_rows, d), jnp.bfloat16)],
        compiler_params=pltpu.CompilerParams(
            dimension_semantics=("arbitrary", "arbitrary"), vmem_limit_bytes=VMEM_LIMIT_BYTES),
        name="norm_proj",
    )(x, g, mod, w_bf16)


LANE = 128
ROUTE_TOK = 512
PEER_TOK = 512
PEER_EXP = 1024
NEG_INF = float("-inf")
LOG2_E = 1.4426950408889634


def _take_top(vals, k):
    rows = lax.broadcasted_iota(jnp.int32, vals.shape, 0)
    taken = []
    for _ in range(k):
        m = jnp.max(vals, axis=0, keepdims=True)
        first = jnp.min(jnp.where(vals == m, rows, vals.shape[0]), axis=0, keepdims=True)
        vals = jnp.where(rows == first, NEG_INF, vals)
        taken.append(m)
    return jnp.concatenate(taken, axis=0), vals


def _take_top_distinct(vals, k):
    taken = []
    for _ in range(k):
        m = jnp.max(vals, axis=0, keepdims=True)
        vals = jnp.where(vals == m, NEG_INF, vals)
        taken.append(m)
    removed = jnp.sum((vals == NEG_INF).astype(jnp.int32), axis=0, keepdims=True)
    return jnp.concatenate(taken, axis=0), vals, removed


def _peer_route_kernel(q_ref, sk_ref, s1m_ref, s2m_ref, thr_ref, c0_ref):
    def route(tok, take_top):
        masked, top_vals, removed = [], [], []
        for z in range(2):
            qz = q_ref[tok, z * LANE:(z + 1) * LANE]
            s = lax.dot_general(sk_ref[0, z], qz, (((1,), (1,)), ((), ())),
                                preferred_element_type=jnp.float32)
            v, rest, n = take_top(s, PEER_TOPK)
            masked.append(jnp.where(rest == NEG_INF, s, NEG_INF))
            top_vals.append(v)
            removed.append(n)
        half = PEER_TOPK // 2
        v1, v2 = top_vals
        cand = jnp.concatenate(
            [v1[0:1] + v2] + [v1[i:i + 1] + v2[:half] for i in range(1, half)] + [v1[half:] + v2[0:1]], axis=0)
        sv, _, n = take_top(cand, PEER_TOPK)
        removed.append(n)
        z_sum = jnp.sum(jnp.exp(sv - sv[0:1]), axis=0, keepdims=True)
        s1m_ref[0, :, tok] = masked[0]
        s2m_ref[0, :, tok] = masked[1]
        thr_ref[0, :, tok] = sv[PEER_TOPK - 1:PEER_TOPK]
        c0_ref[0, :, tok] = -sv[0:1] - jnp.log(z_sum)
        return jnp.max(functools.reduce(jnp.maximum, removed))

    def exact_take_top(vals, k):
        v, rest = _take_top(vals, k)
        return v, rest, jnp.full((1, vals.shape[1]), k, jnp.int32)

    def tile_pair(t, carry):
        toks = [pl.ds(pl.multiple_of((2 * t + i) * LANE, LANE), LANE) for i in range(2)]
        worst = [route(tok, _take_top_distinct) for tok in toks]
        for tok, n in zip(toks, worst):
            @pl.when(n > PEER_TOPK)
            def _():
                route(tok, exact_take_top)

        return carry

    lax.fori_loop(0, ROUTE_TOK // (2 * LANE), tile_pair, 0)


def _peer_route(q, subkeys):
    n = q.shape[0]
    big = pl.BlockSpec((1, N_KEYS, ROUTE_TOK), lambda i, h: (h, 0, i))
    small = pl.BlockSpec((1, 1, ROUTE_TOK), lambda i, h: (h, 0, i))
    return pl.pallas_call(
        _peer_route_kernel,
        grid=(n // ROUTE_TOK, PEER_HEADS),
        in_specs=[
            pl.BlockSpec((ROUTE_TOK, PEER_KEY_DIM), lambda i, h: (i, h)),
            pl.BlockSpec((1, 2, N_KEYS, PEER_KEY_DIM // 2), lambda i, h: (h, 0, 0, 0)),
        ],
        out_specs=[big, big, small, small],
        out_shape=[jax.ShapeDtypeStruct((PEER_HEADS, N_KEYS, n), jnp.float32)] * 2
        + [jax.ShapeDtypeStruct((PEER_HEADS, 1, n), jnp.float32)] * 2,
        compiler_params=pltpu.CompilerParams(
            dimension_semantics=("arbitrary", "arbitrary"), vmem_limit_bytes=VMEM_LIMIT_BYTES),
        name="peer_route",
    )(q, subkeys)


def _gelu_tanh(x):
    return 0.5 * x * (1.0 + jnp.tanh(0.7978845608028654 * (x + 0.044715 * (x * x * x))))


def _peer_dense_kernel(h_ref, x_ref, mod_ref, s1m_ref, s2m_ref, thr_ref, c0_ref, u_ref, vt_ref, o_ref,
                       ht_scr, s1l_scr, s2l_scr, coef_scr, acc_scr, *, gate_idx):
    e = pl.program_id(1)

    @pl.when(e == 0)
    def _():
        ht_scr[...] = h_ref[...].astype(jnp.float32).T.astype(jnp.bfloat16)
        s1l_scr[...] = s1m_ref[...] * LOG2_E
        s2l_scr[...] = (s2m_ref[...] + c0_ref[...]) * LOG2_E
        acc_scr[...] = jnp.zeros_like(acc_scr)

    act = jnp.dot(u_ref[...], ht_scr[...], preferred_element_type=jnp.float32)
    a_per_step = PEER_EXP // N_KEYS
    a_rows = pl.ds(pl.multiple_of(e * a_per_step, a_per_step), a_per_step)
    for t in range(PEER_TOK // LANE):
        tok = slice(t * LANE, (t + 1) * LANE)
        s1_all = [s1m_ref[hh, a_rows, tok] for hh in range(PEER_HEADS)]
        s1l_all = [s1l_scr[hh, a_rows, tok] for hh in range(PEER_HEADS)]
        for a in range(a_per_step):
            rows = slice(a * N_KEYS, (a + 1) * N_KEYS)
            w = jnp.zeros((N_KEYS, LANE), jnp.float32)
            for hh in range(PEER_HEADS):
                s1 = s1_all[hh][a:a + 1]
                picked = (s1 + s2m_ref[hh, :, tok]) >= thr_ref[hh, :, tok]
                w = w + jnp.where(picked, jnp.exp2(s1l_all[hh][a:a + 1] + s2l_scr[hh, :, tok]), 0.0)
            coef_scr[rows, tok] = (w * _gelu_tanh(act[rows, tok])).astype(jnp.bfloat16)
    acc_scr[...] += jnp.dot(vt_ref[...], coef_scr[...], preferred_element_type=jnp.float32)

    @pl.when(e == pl.num_programs(1) - 1)
    def _():
        o_ref[...] = x_ref[...] + mod_ref[0, gate_idx:gate_idx + 1, :] * acc_scr[...].T


def _peer_dense(h, x, mod, s1m, s2m, thr, c0, u_bf16, vt_bf16):
    n, d = x.shape
    assert PEER_TOK == ROW_TILE
    tok_row = lambda i, e: (i, 0)
    big = pl.BlockSpec((PEER_HEADS, N_KEYS, PEER_TOK), lambda i, e: (0, 0, i))
    small = pl.BlockSpec((PEER_HEADS, 1, PEER_TOK), lambda i, e: (0, 0, i))
    return pl.pallas_call(
        functools.partial(_peer_dense_kernel, gate_idx=5),
        grid=(n // PEER_TOK, N_EXPERTS // PEER_EXP),
        in_specs=[
            pl.BlockSpec((PEER_TOK, d), tok_row),
            pl.BlockSpec((PEER_TOK, d), tok_row),
            pl.BlockSpec((1, N_MOD, d), lambda i, e: (_segment_of_row_block(i), 0, 0)),
            big, big, small, small,
            pl.BlockSpec((PEER_EXP, d), lambda i, e: (e, 0)),
            pl.BlockSpec((d, PEER_EXP), lambda i, e: (0, e)),
        ],
        out_specs=pl.BlockSpec((PEER_TOK, d), tok_row),
        out_shape=jax.ShapeDtypeStruct((n, d), jnp.float32),
        scratch_shapes=[
            pltpu.VMEM((d, PEER_TOK), jnp.bfloat16),
            pltpu.VMEM((PEER_HEADS, N_KEYS, PEER_TOK), jnp.float32),
            pltpu.VMEM((PEER_HEADS, N_KEYS, PEER_TOK), jnp.float32),
            pltpu.VMEM((PEER_EXP, PEER_TOK), jnp.bfloat16),
            pltpu.VMEM((d, PEER_TOK), jnp.float32),
        ],
        compiler_params=pltpu.CompilerParams(
            dimension_semantics=("arbitrary", "arbitrary"), vmem_limit_bytes=VMEM_LIMIT_BYTES),
        name="peer_dense",
    )(h, x, mod, s1m, s2m, thr, c0, u_bf16, vt_bf16)


def _merge_kernel(x_ref, ya_ref, yb_ref, yc_ref, ga_ref, gb_ref, gc_ref, mod_ref,
                  wa_ref, wb_ref, wc_ref, wo_ref, o_ref, *, gate_idx):
    def branch(y_ref, gate_ref, w_ref):
        proj = jnp.dot(y_ref[...].astype(jnp.bfloat16), w_ref[...], preferred_element_type=jnp.float32)
        return jax.nn.sigmoid(gate_ref[...]) * proj

    merged = branch(ya_ref, ga_ref, wa_ref) + branch(yb_ref, gb_ref, wb_ref) + branch(yc_ref, gc_ref, wc_ref)
    out = jnp.dot(merged.astype(jnp.bfloat16), wo_ref[...], preferred_element_type=jnp.float32)
    o_ref[...] = x_ref[...] + mod_ref[0, gate_idx:gate_idx + 1, :] * out


def _merge(x, ya, yb, yc, p, mod, wa, wb, wc, wo):
    n_rows, d = x.shape
    gate_col0 = 0
    row = lambda i: (i, 0)
    const = lambda i: (0, 0)
    y_spec = pl.BlockSpec((ROW_TILE, ya.shape[1]), row)
    w_spec = pl.BlockSpec(wa.shape, const)
    return pl.pallas_call(
        functools.partial(_merge_kernel, gate_idx=2),
        grid=(n_rows // ROW_TILE,),
        in_specs=[
            pl.BlockSpec((ROW_TILE, d), row), y_spec, y_spec, y_spec,
            pl.BlockSpec((ROW_TILE, d), lambda i: (i, gate_col0)),
            pl.BlockSpec((ROW_TILE, d), lambda i: (i, gate_col0 + 1)),
            pl.BlockSpec((ROW_TILE, d), lambda i: (i, gate_col0 + 2)),
            pl.BlockSpec((1, N_MOD, d), lambda i: (_segment_of_row_block(i), 0, 0)),
            w_spec, w_spec, w_spec, pl.BlockSpec(wo.shape, const),
        ],
        out_specs=pl.BlockSpec((ROW_TILE, d), row),
        out_shape=jax.ShapeDtypeStruct((n_rows, d), jnp.float32),
        compiler_params=pltpu.CompilerParams(
            dimension_semantics=("arbitrary",), vmem_limit_bytes=VMEM_LIMIT_BYTES),
        name="merge_out_proj",
    )(x, ya, yb, yc, p, p, p, mod, wa, wb, wc, wo)


HEAD_PAIRS = B_HEADS // 2
PREP_TOK = 256
SCAN_CHUNK = 128
assert PREP_TOK == SEQ and DEC_SEQ % PREP_TOK == 0 and 2 * HEAD_DIM == LANE
_HI = lax.Precision.HIGHEST


def _head_ones(n):
    r = lax.broadcasted_iota(jnp.int32, (n, n), 0) // HEAD_DIM
    c = lax.broadcasted_iota(jnp.int32, (n, n), 1) // HEAD_DIM
    return (r == c).astype(jnp.float32)


def _softplus(y):
    return jnp.maximum(y, 0.0) + jnp.log(1.0 + jnp.exp(-jnp.abs(y)))


def _rwkv_prep_kernel(pb_ref, prev_ref, next_ref, mu_ref, w0_ref, w2_ref, a0_ref, a2_ref, g2_ref,
                      kks_ref, ka_ref, rk_ref, ones_ref,
                      r_o, kk_o, v_o, w0_o, w1_o, k0_o, k1_o, nb0_o, nb1_o, g_o, bonus_o):
    i = pl.program_id(0)
    n_ctx = N_CTX_TOK // PREP_TOK
    per_lat = DEC_SEQ // PREP_TOK
    j = (i - n_ctx) % per_lat
    starts_seq = jnp.logical_or(i < n_ctx, j == 0)
    ends_seq = jnp.logical_or(i < n_ctx, j == per_lat - 1)

    pb = pb_ref[...]
    rows = lax.broadcasted_iota(jnp.int32, (PREP_TOK, 1), 0)
    before = jnp.where(starts_seq, 0.0, prev_ref[7:8, :])
    after = jnp.where(ends_seq, 0.0, next_ref[0:1, :])
    prev = jnp.where(rows == 0, before, pltpu.roll(pb, 1, 0))
    nxt = jnp.where(rows == PREP_TOK - 1, after, pltpu.roll(pb, PREP_TOK - 1, 0))
    xb = pb + mu_ref[...] * (0.5 * (prev + nxt) - pb)

    w = B_WIDTH
    r, k, v = xb[:, 0:w], xb[:, w:2 * w], xb[:, 2 * w:3 * w]
    wd = xb[:, 3 * w:3 * w + 2 * LORA_W]
    ad = xb[:, 3 * w + 2 * LORA_W:3 * w + 2 * LORA_W + 2 * LORA_A]
    gd = xb[:, 3 * w + 2 * LORA_W + 2 * LORA_A:]
    ones = ones_ref[...]

    def head_sum(x):
        return jnp.dot(x, ones, preferred_element_type=jnp.float32, precision=_HI)

    lw = jnp.dot(jnp.tanh(wd), w2_ref[...], preferred_element_type=jnp.float32, precision=_HI)
    la = jnp.dot(ad, a2_ref[...], preferred_element_type=jnp.float32, precision=_HI)
    kk = k * kks_ref[...]
    kk = kk * lax.rsqrt(head_sum(kk * kk) + 1e-12)
    kd_sum = jnp.zeros_like(k)
    for z, (w_o, k_o, nb_o) in enumerate(((w0_o, k0_o, nb0_o), (w1_o, k1_o, nb1_o))):
        wlog = -_softplus(-(w0_ref[z:z + 1, :] + lw[:, z * w:(z + 1) * w])) - 0.5
        a = jax.nn.sigmoid(a0_ref[z:z + 1, :] + la[:, z * w:(z + 1) * w])
        kd = k * (1.0 + (a - 1.0) * ka_ref[...])
        kd_sum = kd_sum + kd
        w_o[...] = jnp.exp(-jnp.exp(wlog))
        k_o[...] = kd
        nb_o[...] = -(kk * a)
    r_o[...] = r
    kk_o[...] = kk
    v_o[...] = v
    g_o[...] = jnp.dot(jax.nn.sigmoid(gd), g2_ref[...], preferred_element_type=jnp.float32, precision=_HI)
    bonus_o[...] = head_sum(r * kd_sum * rk_ref[...]) * v


def _rwkv_prep(p, mu, w0, w2_pad, a0, a2_pad, g2, kk_scale, ka, rk):
    n = p.shape[0]
    col_blk = (N_BRANCH * D_MODEL + A_COLS) // B_COLS
    assert col_blk * B_COLS == N_BRANCH * D_MODEL + A_COLS
    halo = 8
    per = PREP_TOK // halo
    full = lambda a: pl.BlockSpec(a.shape, lambda i: (0,) * a.ndim)
    ones = _head_ones(B_WIDTH)
    consts = (mu, w0, w2_pad, a0, a2_pad, g2, kk_scale, ka, rk, ones)
    out_spec = pl.BlockSpec((PREP_TOK, B_WIDTH), lambda i: (i, 0))
    return pl.pallas_call(
        _rwkv_prep_kernel,
        grid=(n // PREP_TOK,),
        in_specs=[
            pl.BlockSpec((PREP_TOK, B_COLS), lambda i: (i, col_blk)),
            pl.BlockSpec((halo, B_COLS), lambda i: (jnp.maximum(i * per - 1, 0), col_blk)),
            pl.BlockSpec((halo, B_COLS), lambda i: (jnp.minimum((i + 1) * per, n // halo - 1), col_blk)),
        ] + [full(a) for a in consts],
        out_specs=[out_spec] * 11,
        out_shape=[jax.ShapeDtypeStruct((n, B_WIDTH), jnp.float32)] * 11,
        compiler_params=pltpu.CompilerParams(
            dimension_semantics=("arbitrary",), vmem_limit_bytes=VMEM_LIMIT_BYTES),
        name="rwkv_prep",
    )(p, p, p, *consts)


SCAN_LANES = 2 * LANE


def _head_sums(head_ones, *terms):
    sums = []
    for x, passes in terms:
        hi = x.astype(jnp.bfloat16)
        out = jnp.dot(hi, head_ones, preferred_element_type=jnp.float32)
        if passes == 2:
            lo = (x - hi.astype(jnp.float32)).astype(jnp.bfloat16)
            out = out + jnp.dot(lo, head_ones, preferred_element_type=jnp.float32)
        sums.append(out)
    return sums


def _rwkv_scan_kernel(rf, kkf, vf, wf, kf, nbf, rb, kkb, vb, wb, kb, nbb, s0_ref, ones_ref, eye_ref,
                      yf_ref, yb_ref, sfin_ref, s_scr, *, n_seq):
    c = pl.program_id(1)

    @pl.when(c == 0)
    def _():
        s_scr[...] = s0_ref[...]

    head_ones = ones_ref[...]
    eye = eye_ref[...]
    fwd_refs = (rf, kkf, vf, wf, kf, nbf)
    bwd_refs = (rb, kkb, vb, wb, kb, nbb)
    R, KK, V, W, K, NB = range(6)

    def step(t, carry):
        t_f = pl.ds(t, 1)
        t_b = pl.ds(SCAN_CHUNK - 1 - t, 1)
        rows_f = [[a[g, t_f, :] for a in fwd_refs] for g in range(n_seq)]
        rows_b = [[a[g, t_b, :] for a in bwd_refs] for g in range(n_seq)]

        def both(which, chain):
            g, p = chain
            lanes = slice(p * LANE, (p + 1) * LANE)
            return jnp.concatenate([rows_f[g][which][:, lanes], rows_b[g][which][:, lanes]], axis=1)

        chains = [(g, p) for g in range(n_seq) for p in range(HEAD_PAIRS)]
        rows = {ch: slice(i * HEAD_DIM, (i + 1) * HEAD_DIM) for i, ch in enumerate(chains)}
        s_old = {ch: s_scr[ch[0], ch[1]] for ch in chains}
        s_kk, v_col = _head_sums(
            head_ones,
            (jnp.concatenate([s_old[ch] * both(KK, ch) for ch in chains], axis=0), 1),
            (jnp.concatenate([eye * both(V, ch) for ch in chains], axis=0), 1))
        s_new = {}
        for ch in chains:
            s = s_old[ch] * both(W, ch) + s_kk[rows[ch]] * both(NB, ch) + v_col[rows[ch]] * both(K, ch)
            s_scr[ch[0], ch[1]] = s
            s_new[ch] = s
        (y_all,) = _head_sums(
            head_ones, (jnp.concatenate([s_new[ch] * both(R, ch) for ch in chains], axis=0), 1))
        for g in range(n_seq):
            y_rows = [jnp.sum(y_all[rows[(g, p)]] * eye, axis=0, keepdims=True) for p in range(HEAD_PAIRS)]
            yf_ref[g, t_f, :] = jnp.concatenate([y[:, :LANE] for y in y_rows], axis=1)
            yb_ref[g, t_b, :] = jnp.concatenate([y[:, LANE:] for y in y_rows], axis=1)
        return carry

    lax.fori_loop(0, SCAN_CHUNK, step, 0, unroll=2)

    @pl.when(c == pl.num_programs(1) - 1)
    def _():
        sfin_ref[...] = s_scr[...]


def _rwkv_scan(flat, s0, t_len, first_seq, seq_group):
    r, kk, v, w0, w1, k0, k1, nb0, nb1 = (a.reshape(-1, t_len, B_WIDTH) for a in flat)
    n_seq = s0.shape[0]
    n_chunks = t_len // SCAN_CHUNK
    assert n_seq % seq_group == 0 and first_seq % seq_group == 0 and n_chunks * SCAN_CHUNK == t_len
    first_blk = first_seq // seq_group
    blk = (seq_group, SCAN_CHUNK, B_WIDTH)
    fwd_in = pl.BlockSpec(blk, lambda i, c: (first_blk + i, c, 0))
    bwd_in = pl.BlockSpec(blk, lambda i, c: (first_blk + i, n_chunks - 1 - c, 0))
    fwd = pl.BlockSpec(blk, lambda i, c: (i, c, 0))
    bwd = pl.BlockSpec(blk, lambda i, c: (i, n_chunks - 1 - c, 0))
    state_blk = (seq_group, HEAD_PAIRS, HEAD_DIM, SCAN_LANES)
    state = pl.BlockSpec(state_blk, lambda i, c: (i, 0, 0, 0))
    head_ones = _head_ones(SCAN_LANES).astype(jnp.bfloat16)
    eye = (lax.broadcasted_iota(jnp.int32, (HEAD_DIM, SCAN_LANES), 0)
           == lax.broadcasted_iota(jnp.int32, (HEAD_DIM, SCAN_LANES), 1) % HEAD_DIM).astype(jnp.float32)
    const = lambda a: pl.BlockSpec(a.shape, lambda i, c: (0, 0))
    y_shape = jax.ShapeDtypeStruct((n_seq, t_len, B_WIDTH), jnp.float32)
    return pl.pallas_call(
        functools.partial(_rwkv_scan_kernel, n_seq=seq_group),
        grid=(n_seq // seq_group, n_chunks),
        in_specs=[fwd_in] * 6 + [bwd_in] * 6 + [state, const(head_ones), const(eye)],
        out_specs=[fwd, bwd, state],
        out_shape=[y_shape, y_shape, jax.ShapeDtypeStruct(s0.shape, jnp.float32)],
        scratch_shapes=[pltpu.VMEM(state_blk, jnp.float32)],
        compiler_params=pltpu.CompilerParams(
            dimension_semantics=("arbitrary", "arbitrary"), vmem_limit_bytes=VMEM_LIMIT_BYTES),
        name="rwkv_scan",
    )(r, kk, v, w0, k0, nb0, r, kk, v, w1, k1, nb1, s0, head_ones, eye)


def _rwkv_post_kernel(yfc_ref, ybc_ref, yfl_ref, ybl_ref, g_ref, bonus_ref, lng_ref, lnb_ref, ones_ref, o_ref):
    ones = ones_ref[...]
    is_ctx = pl.program_id(0) < N_CTX_TOK // PREP_TOK
    y = jnp.where(is_ctx, yfc_ref[...] + ybc_ref[...], yfl_ref[...] + ybl_ref[...])
    mean = jnp.dot(y, ones, preferred_element_type=jnp.float32, precision=_HI) * (1.0 / HEAD_DIM)
    d = y - mean
    var = jnp.dot(d * d, ones, preferred_element_type=jnp.float32, precision=_HI) * (1.0 / HEAD_DIM)
    y = d * lax.rsqrt(var + GN_EPS) * lng_ref[...] + lnb_ref[...]
    o_ref[...] = (y + bonus_ref[...]) * g_ref[...]


def _rwkv_post(yf_ctx, yb_ctx, yf_lat, yb_lat, g, bonus, lnx_g, lnx_b):
    n = g.shape[0]
    n_ctx = N_CTX_TOK // PREP_TOK
    n_lat = N_LAT_TOK // PREP_TOK
    row = pl.BlockSpec((PREP_TOK, B_WIDTH), lambda i: (i, 0))
    ctx_row = pl.BlockSpec((PREP_TOK, B_WIDTH), lambda i: (jnp.minimum(i, n_ctx - 1), 0))
    lat_row = pl.BlockSpec((PREP_TOK, B_WIDTH), lambda i: (jnp.clip(i - n_ctx, 0, n_lat - 1), 0))
    vec = pl.BlockSpec((1, B_WIDTH), lambda i: (0, 0))
    ones = _head_ones(B_WIDTH)
    return pl.pallas_call(
        _rwkv_post_kernel,
        grid=(n // PREP_TOK,),
        in_specs=[ctx_row, ctx_row, lat_row, lat_row, row, row, vec, vec, pl.BlockSpec(ones.shape, lambda i: (0, 0))],
        out_specs=row,
        out_shape=jax.ShapeDtypeStruct((n, B_WIDTH), jnp.float32),
        compiler_params=pltpu.CompilerParams(
            dimension_semantics=("arbitrary",), vmem_limit_bytes=VMEM_LIMIT_BYTES),
        name="rwkv_post",
    )(yf_ctx, yb_ctx, yf_lat, yb_lat, g, bonus, lnx_g, lnx_b, ones)


def _state_to_pairs(s):
    b = s.shape[0]
    s = s.reshape(b, 2, HEAD_PAIRS, 2, HEAD_DIM, HEAD_DIM)
    return jnp.transpose(s, (0, 2, 4, 1, 3, 5)).reshape(b, HEAD_PAIRS, HEAD_DIM, SCAN_LANES)


def _state_from_pairs(s):
    b = s.shape[0]
    s = s.reshape(b, HEAD_PAIRS, HEAD_DIM, 2, 2, HEAD_DIM)
    return jnp.transpose(s, (0, 3, 1, 4, 2, 5)).reshape(b, 2, B_HEADS, HEAD_DIM, HEAD_DIM)


def _rwkv_mixer(p, lp, state_lat):
    zeros = jnp.zeros((LORA_W, B_WIDTH), jnp.float32)
    w2_pad = jnp.concatenate([jnp.concatenate([lp["rw_w2"][0], zeros], axis=1),
                              jnp.concatenate([zeros, lp["rw_w2"][1]], axis=1)], axis=0)
    a2_pad = jnp.concatenate([jnp.concatenate([lp["rw_a2"][0], zeros], axis=1),
                              jnp.concatenate([zeros, lp["rw_a2"][1]], axis=1)], axis=0)
    row = lambda a: a.reshape(1, -1)
    prep = _rwkv_prep(p, row(lp["rw_mu"]), lp["rw_w0"], w2_pad, lp["rw_a0"], a2_pad, lp["rw_g2"],
                      row(lp["rw_kk"]), row(lp["rw_ka"]), row(lp["rw_rk"]))
    scan_in, (g, bonus) = prep[:9], prep[9:]
    s0_ctx = jnp.zeros((BATCH, HEAD_PAIRS, HEAD_DIM, SCAN_LANES), jnp.float32)
    yf_c, yb_c, s_ctx = _rwkv_scan(scan_in, s0_ctx, t_len=SEQ, first_seq=0, seq_group=4)
    yf_l, yb_l, _ = _rwkv_scan(scan_in, _state_to_pairs(state_lat), t_len=DEC_SEQ,
                               first_seq=N_CTX_TOK // DEC_SEQ, seq_group=DEC_BATCH)
    flat = lambda a: a.reshape(-1, B_WIDTH)
    y = _rwkv_post(flat(yf_c), flat(yb_c), flat(yf_l), flat(yb_l), g, bonus,
                   row(lp["rw_lnx_g"]), row(lp["rw_lnx_b"]))
    return y, _state_from_pairs(s_ctx)


ATT_SCALE = HEAD_DIM ** -0.5
COL0_QA = N_BRANCH * D_MODEL
COL0_KA = COL0_QA + A_HEADS * HEAD_DIM
COL0_VA = COL0_KA + A_KV * HEAD_DIM
COL0_QC = COL0_QA + A_COLS + B_COLS
COL0_KC = COL0_QC + C_HEADS * HEAD_DIM
COL0_VC = COL0_KC + C_HEADS * HEAD_DIM
A_GROUP_COLS = A_GROUP * HEAD_DIM
assert A_KV * HEAD_DIM == LANE and all(c % LANE == 0 for c in (COL0_QA, COL0_KA, COL0_VA, COL0_QC, COL0_KC, COL0_VC))
assert COL0_QA % A_GROUP_COLS == 0
GRID_ROWS = DEC_SEQ // GRID_W
NA_WIN_ROWS = min(NA_ROWS, GRID_ROWS)


def _attend(q, parts, sink=None):
    scores = []
    for k, _, bias in parts:
        s = lax.dot_general(q, k, (((1,), (1,)), ((), ())), preferred_element_type=jnp.float32)
        scores.append(s if bias is None else s + bias)
    m = functools.reduce(jnp.maximum, [jnp.max(s, axis=-1, keepdims=True) for s in scores])
    den = 0.0
    if sink is not None:
        m = jnp.maximum(m, sink)
        den = jnp.exp(sink - m)
    out = 0.0
    for s, (_, v, _) in zip(scores, parts):
        prob = jnp.exp(s - m)
        den = den + jnp.sum(prob, axis=-1, keepdims=True)
        out = out + jnp.dot(prob.astype(jnp.bfloat16), v, preferred_element_type=jnp.float32)
    return out / den


def _scaled_bf16(q):
    return (q * ATT_SCALE).astype(jnp.bfloat16)


def _kv_half(x, kv):
    return jnp.where(kv == 0, x[:, :HEAD_DIM], x[:, HEAD_DIM:])


def _ctx_gqa_kernel(q_ref, k_ref, v_ref, sink_ref, o_ref):
    kv = pl.program_id(1)
    k = _kv_half(k_ref[...], kv).astype(jnp.bfloat16)
    v = _kv_half(v_ref[...], kv).astype(jnp.bfloat16)
    outs = []
    for g in range(A_GROUP):
        q = _scaled_bf16(q_ref[:, g * HEAD_DIM:(g + 1) * HEAD_DIM])
        sink = sink_ref[pl.ds(kv * A_GROUP + g, 1), :][:, 0:1]
        outs.append(_attend(q, [(k, v, None)], sink))
    o_ref[...] = jnp.concatenate(outs, axis=1)


def _ctx_gqa(p, sink_rows):
    q0, k0, v0 = COL0_QA // A_GROUP_COLS, COL0_KA // LANE, COL0_VA // LANE
    return pl.pallas_call(
        _ctx_gqa_kernel,
        grid=(BATCH, A_KV),
        in_specs=[
            pl.BlockSpec((SEQ, A_GROUP_COLS), lambda b, kv: (b, q0 + kv)),
            pl.BlockSpec((SEQ, LANE), lambda b, kv: (b, k0)),
            pl.BlockSpec((SEQ, LANE), lambda b, kv: (b, v0)),
            pl.BlockSpec(sink_rows.shape, lambda b, kv: (0, 0)),
        ],
        out_specs=pl.BlockSpec((SEQ, A_GROUP_COLS), lambda b, kv: (b, kv)),
        out_shape=jax.ShapeDtypeStruct((N_CTX_TOK, A_HEADS * HEAD_DIM), jnp.float32),
        compiler_params=pltpu.CompilerParams(
            dimension_semantics=("arbitrary", "arbitrary"), vmem_limit_bytes=VMEM_LIMIT_BYTES),
        name="ctx_gqa",
    )(p, p, p, sink_rows)


def _ctx_mha_kernel(q_ref, k_ref, v_ref, o_ref):
    outs = []
    for j in range(2):
        lanes = slice(j * HEAD_DIM, (j + 1) * HEAD_DIM)
        outs.append(_attend(_scaled_bf16(q_ref[:, lanes]),
                            [(k_ref[:, lanes].astype(jnp.bfloat16), v_ref[:, lanes].astype(jnp.bfloat16), None)]))
    o_ref[...] = jnp.concatenate(outs, axis=1)


def _ctx_mha(p):
    q0, k0, v0 = COL0_QC // LANE, COL0_KC // LANE, COL0_VC // LANE
    return pl.pallas_call(
        _ctx_mha_kernel,
        grid=(BATCH, C_HEADS // 2),
        in_specs=[
            pl.BlockSpec((SEQ, LANE), lambda b, hp: (b, q0 + hp)),
            pl.BlockSpec((SEQ, LANE), lambda b, hp: (b, k0 + hp)),
            pl.BlockSpec((SEQ, LANE), lambda b, hp: (b, v0 + hp)),
        ],
        out_specs=pl.BlockSpec((SEQ, LANE), lambda b, hp: (b, hp)),
        out_shape=jax.ShapeDtypeStruct((N_CTX_TOK, C_HEADS * HEAD_DIM), jnp.float32),
        compiler_params=pltpu.CompilerParams(
            dimension_semantics=("arbitrary", "arbitrary"), vmem_limit_bytes=VMEM_LIMIT_BYTES),
        name="ctx_mha",
    )(p, p, p)


def _rope(x, cos, sin):
    reps = x.shape[1] // LANE
    if reps > 1:
        cos = jnp.concatenate([cos] * reps, axis=1)
        sin = jnp.concatenate([sin] * reps, axis=1)
    lane = lax.broadcasted_iota(jnp.int32, x.shape, 1)
    first = (lane % (HEAD_DIM // 2)) < (HEAD_DIM // 4)
    quarter = HEAD_DIM // 4
    pieces = [x[:, i * LANE:(i + 1) * LANE] for i in range(reps)]
    up = jnp.concatenate([pltpu.roll(piece, LANE - quarter, 1) for piece in pieces], axis=1)
    down = jnp.concatenate([pltpu.roll(piece, quarter, 1) for piece in pieces], axis=1)
    return x * cos + jnp.where(first, -up, down) * sin


def _lat_window_kernel(q_ref, kp_ref, kc_ref, kn_ref, vp_ref, vc_ref, vn_ref, ck_ref, cv_ref,
                       cosq_ref, sinq_ref, cosp_ref, sinp_ref, cosn_ref, sinn_ref, sink_ref, o_ref):
    qb = pl.program_id(1)
    kv = pl.program_id(2)
    n_blocks = pl.num_programs(1)
    row = lax.broadcasted_iota(jnp.int32, (Q_BLOCK, Q_BLOCK), 0)
    col = lax.broadcasted_iota(jnp.int32, (Q_BLOCK, Q_BLOCK), 1)
    bias_prev = jnp.where(jnp.logical_and(col >= row, qb > 0), 0.0, NEG_INF)
    bias_next = jnp.where(jnp.logical_and(col <= row, qb < n_blocks - 1), 0.0, NEG_INF)
    assert WINDOW == Q_BLOCK

    def keys(k_ref, cos_ref, sin_ref):
        return _kv_half(_rope(k_ref[...], cos_ref[...], sin_ref[...]), kv).astype(jnp.bfloat16)

    def vals(v_ref):
        return _kv_half(v_ref[...], kv).astype(jnp.bfloat16)

    parts = [
        (keys(kp_ref, cosp_ref, sinp_ref), vals(vp_ref), bias_prev),
        (keys(kc_ref, cosq_ref, sinq_ref), vals(vc_ref), None),
        (keys(kn_ref, cosn_ref, sinn_ref), vals(vn_ref), bias_next),
        (_kv_half(ck_ref[0], kv).astype(jnp.bfloat16), _kv_half(cv_ref[0], kv).astype(jnp.bfloat16), None),
    ]
    q_all = _rope(q_ref[...], cosq_ref[...], sinq_ref[...])
    outs = []
    for g in range(A_GROUP):
        sink = sink_ref[pl.ds(kv * A_GROUP + g, 1), :][:, 0:1]
        outs.append(_attend(_scaled_bf16(q_all[:, g * HEAD_DIM:(g + 1) * HEAD_DIM]), parts, sink))
    o_ref[...] = jnp.concatenate(outs, axis=1)


def _lat_window(p, cache_k, cache_v, cos, sin, sink_rows):
    n_blocks = DEC_SEQ // Q_BLOCK
    base = N_CTX_TOK // Q_BLOCK
    q0, k0, v0 = COL0_QA // A_GROUP_COLS, COL0_KA // LANE, COL0_VA // LANE
    prev = lambda b, qb, kv: base + b * n_blocks + jnp.maximum(qb - 1, 0)
    here = lambda b, qb, kv: base + b * n_blocks + qb
    nxt = lambda b, qb, kv: base + b * n_blocks + jnp.minimum(qb + 1, n_blocks - 1)
    slab = lambda rows, c0: pl.BlockSpec((Q_BLOCK, LANE), lambda b, qb, kv: (rows(b, qb, kv), c0))
    table = lambda blk: pl.BlockSpec((Q_BLOCK, LANE), lambda b, qb, kv: (blk(b, qb, kv), 0))
    cache = pl.BlockSpec((1, PAST_LEN, LANE), lambda b, qb, kv: (b, 0, 0))
    return pl.pallas_call(
        _lat_window_kernel,
        grid=(DEC_BATCH, n_blocks, A_KV),
        in_specs=[
            pl.BlockSpec((Q_BLOCK, A_GROUP_COLS), lambda b, qb, kv: (here(b, qb, kv), q0 + kv)),
            slab(prev, k0), slab(here, k0), slab(nxt, k0),
            slab(prev, v0), slab(here, v0), slab(nxt, v0),
            cache, cache,
            table(lambda b, qb, kv: qb), table(lambda b, qb, kv: qb),
            table(lambda b, qb, kv: jnp.maximum(qb - 1, 0)), table(lambda b, qb, kv: jnp.maximum(qb - 1, 0)),
            table(lambda b, qb, kv: jnp.minimum(qb + 1, n_blocks - 1)),
            table(lambda b, qb, kv: jnp.minimum(qb + 1, n_blocks - 1)),
            pl.BlockSpec(sink_rows.shape, lambda b, qb, kv: (0, 0)),
        ],
        out_specs=pl.BlockSpec((Q_BLOCK, A_GROUP_COLS), lambda b, qb, kv: (b * n_blocks + qb, kv)),
        out_shape=jax.ShapeDtypeStruct((N_LAT_TOK, A_HEADS * HEAD_DIM), jnp.float32),
        compiler_params=pltpu.CompilerParams(
            dimension_semantics=("arbitrary", "arbitrary", "arbitrary"), vmem_limit_bytes=VMEM_LIMIT_BYTES),
        name="lat_window",
    )(p, p, p, p, p, p, p, cache_k, cache_v, cos, sin, cos, sin, cos, sin, sink_rows)


N_DC = 2 * NA_COLS - 1
N_DR = 2 * NA_ROWS - 1


def _na_bias_kernel(rpb_ref, o_ref):
    h = pl.program_id(0)
    shape = (GRID_W, 2 * GRID_W)
    qc = lax.broadcasted_iota(jnp.int32, shape, 0)
    kc = lax.broadcasted_iota(jnp.int32, shape, 1) % GRID_W
    second = lax.broadcasted_iota(jnp.int32, shape, 1) >= GRID_W
    start = jnp.clip(qc - NA_COLS // 2, 0, GRID_W - NA_COLS)
    inside = jnp.logical_and(kc >= start, kc < start + NA_COLS)
    dc = jnp.clip(kc - qc, -(NA_COLS - 1), NA_COLS - 1) + NA_COLS - 1
    for dr in range(N_DR - 1):
        bias = jnp.zeros(shape, jnp.float32)
        for d in range(N_DC):
            lo = rpb_ref[(h * N_DR + dr) * N_DC + d]
            hi = rpb_ref[(h * N_DR + dr + 1) * N_DC + d]
            bias = jnp.where(dc == d, jnp.where(second, hi, lo), bias)
        o_ref[0, dr] = jnp.where(inside, bias, NEG_INF)


def _na_bias(rpb):
    return pl.pallas_call(
        _na_bias_kernel,
        grid=(C_HEADS,),
        in_specs=[pl.BlockSpec(memory_space=pltpu.SMEM)],
        out_specs=pl.BlockSpec((1, N_DR - 1, GRID_W, 2 * GRID_W), lambda h: (h, 0, 0, 0)),
        out_shape=jax.ShapeDtypeStruct((C_HEADS, N_DR - 1, GRID_W, 2 * GRID_W), jnp.float32),
        compiler_params=pltpu.CompilerParams(dimension_semantics=("arbitrary",)),
        name="na_bias",
    )(rpb.reshape(-1))


def _lat_neighbourhood_kernel(q_ref, k_ref, v_ref, ck_ref, cv_ref, bias_ref, o_ref):
    hp = pl.program_id(1)
    r = pl.program_id(2)
    first_row = jnp.clip(r - NA_WIN_ROWS // 2, 0, GRID_ROWS - NA_WIN_ROWS)
    local = pl.ds(pl.multiple_of(first_row * GRID_W, GRID_W), NA_WIN_ROWS * GRID_W)
    dr0 = first_row - r + NA_ROWS - 1
    outs = []
    for j in range(2):
        lanes = slice(j * HEAD_DIM, (j + 1) * HEAD_DIM)
        bias = jnp.concatenate([bias_ref[hp * 2 + j, dr0 + a] for a in range(0, NA_WIN_ROWS, 2)], axis=1)
        parts = [
            (k_ref[local, :][:, lanes].astype(jnp.bfloat16), v_ref[local, :][:, lanes].astype(jnp.bfloat16), bias),
            (ck_ref[0, :, lanes].astype(jnp.bfloat16), cv_ref[0, :, lanes].astype(jnp.bfloat16), None),
        ]
        outs.append(_attend(_scaled_bf16(q_ref[:, lanes]), parts))
    o_ref[...] = jnp.concatenate(outs, axis=1)


def _lat_neighbourhood(p, cache_k, cache_v, bias_tiles):
    assert NA_WIN_ROWS % 2 == 0 and N_CTX_TOK % DEC_SEQ == 0
    q0, k0, v0 = COL0_QC // LANE, COL0_KC // LANE, COL0_VC // LANE
    row_base = N_CTX_TOK // GRID_W
    seq_base = N_CTX_TOK // DEC_SEQ
    whole = lambda c0: pl.BlockSpec((DEC_SEQ, LANE), lambda b, hp, r: (seq_base + b, c0 + hp))
    cache = pl.BlockSpec((1, PAST_LEN, LANE), lambda b, hp, r: (b, 0, hp))
    return pl.pallas_call(
        _lat_neighbourhood_kernel,
        grid=(DEC_BATCH, C_HEADS // 2, GRID_ROWS),
        in_specs=[
            pl.BlockSpec((GRID_W, LANE), lambda b, hp, r: (row_base + b * GRID_ROWS + r, q0 + hp)),
            whole(k0), whole(v0), cache, cache,
            pl.BlockSpec(bias_tiles.shape, lambda b, hp, r: (0, 0, 0, 0)),
        ],
        out_specs=pl.BlockSpec((GRID_W, LANE), lambda b, hp, r: (b * GRID_ROWS + r, hp)),
        out_shape=jax.ShapeDtypeStruct((N_LAT_TOK, C_HEADS * HEAD_DIM), jnp.float32),
        compiler_params=pltpu.CompilerParams(
            dimension_semantics=("arbitrary", "arbitrary", "arbitrary"), vmem_limit_bytes=VMEM_LIMIT_BYTES),
        name="lat_neighbourhood",
    )(p, p, p, cache_k, cache_v, bias_tiles)


def _rope_tables():
    quarter = HEAD_DIM // 4
    t = jnp.arange(DEC_SEQ)
    pos = jnp.stack([t // GRID_W, t % GRID_W], axis=-1).astype(jnp.float32)
    freqs = ROPE_BASE ** (-jnp.arange(quarter, dtype=jnp.float32) / quarter)
    ang = pos[:, :, None] * freqs
    ang = jnp.concatenate([ang, ang], axis=-1).reshape(DEC_SEQ, HEAD_DIM)
    ang = jnp.concatenate([ang] * (LANE // HEAD_DIM), axis=-1)
    return jnp.cos(ang), jnp.sin(ang)


def _final_norm_kernel(x_ref, g_ref, o_ref):
    x = x_ref[...]
    o_ref[...] = x * lax.rsqrt(jnp.mean(x * x, axis=-1, keepdims=True) + RMS_EPS) * g_ref[...]


def _final_norm(x, g):
    n, d = x.shape
    return pl.pallas_call(
        _final_norm_kernel,
        grid=(n // ROW_TILE,),
        in_specs=[pl.BlockSpec((ROW_TILE, d), lambda i: (i, 0)), pl.BlockSpec((1, d), lambda i: (0, 0))],
        out_specs=pl.BlockSpec((ROW_TILE, d), lambda i: (i, 0)),
        out_shape=jax.ShapeDtypeStruct((n, d), jnp.float32),
        compiler_params=pltpu.CompilerParams(dimension_semantics=("arbitrary",)),
        name="final_norm",
    )(x, g)


def kernel(x_prompt, x_sample, cache_a_k, cache_a_v, cache_c_k, cache_c_v, state_rwkv, c, c_ctx, ln1_g, ln2_g, lnf_g, ada_w, ada_b, w_in, a_sink, a_out, rw_mu, rw_w0, rw_w2, rw_a0, rw_a2, rw_g2, rw_kk, rw_ka, rw_rk, rw_lnx_g, rw_lnx_b, rw_out, na_rpb, na_out, w_o, pe_q, pe_subkeys, pe_u, pe_v):
    bf16 = jnp.bfloat16
    x = jnp.concatenate([x_prompt.reshape(N_CTX_TOK, D_MODEL), x_sample.reshape(N_LAT_TOK, D_MODEL)], axis=0)
    silu_all = jnp.concatenate([jax.nn.silu(c_ctx)[None, :], jax.nn.silu(c)], axis=0)
    n_mix = A_COLS + B_COLS + C_COLS
    rope_cos, rope_sin = _rope_tables()
    list_ak, list_av, list_ck, list_cv, list_st = [], [], [], [], []
    for l in range(DEPTH):
        lp = {
            "rw_mu": rw_mu[l], "rw_w0": rw_w0[l], "rw_w2": rw_w2[l], "rw_a0": rw_a0[l],
            "rw_a2": rw_a2[l], "rw_g2": rw_g2[l], "rw_kk": rw_kk[l], "rw_ka": rw_ka[l],
            "rw_rk": rw_rk[l], "rw_lnx_g": rw_lnx_g[l], "rw_lnx_b": rw_lnx_b[l],
        }
        mod = (silu_all @ ada_w[l] + ada_b[l]).reshape(N_SEG, N_MOD, D_MODEL)

        w_in_l = jnp.concatenate([w_in[l][:, n_mix:], w_in[l][:, :n_mix]], axis=1).astype(bf16)
        p, _ = _norm_proj(x, ln1_g[l][None, :], mod, w_in_l, 0, 1, col_tile=IN_COLS // 3)
        sink_rows = jnp.broadcast_to(a_sink[l][:, None], (A_HEADS, LANE))
        flat_cache = lambda cache: cache[:, l].reshape(DEC_BATCH, PAST_LEN, -1)
        ya = jnp.concatenate([
            _ctx_gqa(p, sink_rows),
            _lat_window(p, flat_cache(cache_a_k), flat_cache(cache_a_v), rope_cos, rope_sin, sink_rows)], axis=0)
        yc = jnp.concatenate([
            _ctx_mha(p),
            _lat_neighbourhood(p, flat_cache(cache_c_k), flat_cache(cache_c_v), _na_bias(na_rpb[l]))], axis=0)
        yb, st = _rwkv_mixer(p, lp, state_rwkv[:, l])
        new_cache = lambda col0, heads: p[:N_CTX_TOK, col0:col0 + heads * HEAD_DIM].reshape(BATCH, SEQ, heads, HEAD_DIM)
        list_ak.append(new_cache(COL0_KA, A_KV))
        list_av.append(new_cache(COL0_VA, A_KV))
        list_ck.append(new_cache(COL0_KC, C_HEADS))
        list_cv.append(new_cache(COL0_VC, C_HEADS))
        list_st.append(st)
        x = _merge(x, ya, yb, yc, p, mod,
                   a_out[l].astype(bf16), rw_out[l].astype(bf16), na_out[l].astype(bf16), w_o[l].astype(bf16))

        q, h2 = _norm_proj(x, ln2_g[l][None, :], mod, pe_q[l].astype(bf16), 3, 4,
                           col_tile=PEER_HEADS * PEER_KEY_DIM)
        s1m, s2m, thr, c0 = _peer_route(q, pe_subkeys[l])
        x = _peer_dense(h2, x, mod, s1m, s2m, thr, c0, pe_u[l].astype(bf16), pe_v[l].T.astype(bf16))

    y = _final_norm(x, lnf_g[None, :])
    y_prompt = y[:N_CTX_TOK].reshape(BATCH, SEQ, D_MODEL)
    y_sample = y[N_CTX_TOK:].reshape(DEC_BATCH, DEC_SEQ, D_MODEL)
    return (y_prompt, y_sample, jnp.stack(list_ak, axis=1), jnp.stack(list_av, axis=1),
            jnp.stack(list_ck, axis=1), jnp.stack(list_cv, axis=1), jnp.stack(list_st, axis=1))
```

```python
import functools

import jax
import jax.numpy as jnp
from jax import lax
from jax.experimental import pallas as pl
from jax.experimental.pallas import tpu as pltpu

D_MODEL = 1024
BATCH = 16
SEQ = 256
DEPTH = 4
DEC_BATCH = 2
DEC_SEQ = 2048
PAST_LEN = 512
GRID_W = 64
HEAD_DIM = 64
A_HEADS = 8
A_KV = 2
A_GROUP = A_HEADS // A_KV
WINDOW = 128
Q_BLOCK = 128
B_HEADS = 8
B_WIDTH = B_HEADS * HEAD_DIM
LORA_W = 64
LORA_A = 64
LORA_G = 128
GN_EPS = 64e-5
C_HEADS = 8
NA_ROWS = 8
NA_COLS = 16
N_BRANCH = 3
A_COLS = (A_HEADS + 2 * A_KV) * HEAD_DIM
B_COLS = 3 * B_WIDTH + 2 * LORA_W + 2 * LORA_A + LORA_G
C_COLS = 3 * C_HEADS * HEAD_DIM
IN_COLS = A_COLS + B_COLS + C_COLS + N_BRANCH * D_MODEL
PEER_HEADS = 8
N_KEYS = 128
N_EXPERTS = N_KEYS * N_KEYS
PEER_KEY_DIM = 256
PEER_TOPK = 16
PEER_CHUNK = 128
ROPE_BASE = 10000.0
RMS_EPS = 1e-6
N_MOD = 6

N_CTX_TOK = BATCH * SEQ
N_LAT_TOK = DEC_BATCH * DEC_SEQ
N_TOK = N_CTX_TOK + N_LAT_TOK
N_SEG = 1 + DEC_BATCH

VMEM_LIMIT_BYTES = 48 * 1024 * 1024

ROW_TILE = 512
assert N_CTX_TOK % ROW_TILE == 0 and DEC_SEQ % ROW_TILE == 0


def _segment_of_row_block(i):
    n_ctx_blocks = N_CTX_TOK // ROW_TILE
    blocks_per_lat = DEC_SEQ // ROW_TILE
    return jnp.where(i < n_ctx_blocks, 0, 1 + (i - n_ctx_blocks) // blocks_per_lat)


def _norm_proj_kernel(x_ref, g_ref, mod_ref, w_ref, o_ref, h_ref, *, shift_idx, scale_idx):
    @pl.when(pl.program_id(1) == 0)
    def _():
        x = x_ref[...]
        y = x * lax.rsqrt(jnp.mean(x * x, axis=-1, keepdims=True) + RMS_EPS) * g_ref[...]
        h = y * (1.0 + mod_ref[0, scale_idx:scale_idx + 1, :]) + mod_ref[0, shift_idx:shift_idx + 1, :]
        h_ref[...] = h.astype(jnp.bfloat16)

    o_ref[...] = jnp.dot(h_ref[...], w_ref[...], preferred_element_type=jnp.float32)


def _norm_proj(x, g, mod, w_bf16, shift_idx, scale_idx, col_tile):
    n_rows, d = x.shape
    n_cols = w_bf16.shape[1]
    assert n_cols % col_tile == 0
    return pl.pallas_call(
        functools.partial(_norm_proj_kernel, shift_idx=shift_idx, scale_idx=scale_idx),
        grid=(n_rows // ROW_TILE, n_cols // col_tile),
        in_specs=[
            pl.BlockSpec((ROW_TILE, d), lambda i, j: (i, 0)),
            pl.BlockSpec((1, d), lambda i, j: (0, 0)),
            pl.BlockSpec((1, N_MOD, d), lambda i, j: (_segment_of_row_block(i), 0, 0)),
            pl.BlockSpec((d, col_tile), lambda i, j: (0, j)),
        ],
        out_specs=[pl.BlockSpec((ROW_TILE, col_tile), lambda i, j: (i, j)),
                   pl.BlockSpec((ROW_TILE, d), lambda i, j: (i, 0))],
        out_shape=[jax.ShapeDtypeStruct((n_rows, n_cols), jnp.float32),
                   jax.ShapeDtypeStruct((n_rows, d), jnp.bfloat16)],
        compiler_params=pltpu.CompilerParams(
            dimension_semantics=("arbitrary", "arbitrary"), vmem_limit_bytes=VMEM_LIMIT_BYTES),
        name="norm_proj",
    )(x, g, mod, w_bf16)


LANE = 128
ROUTE_TOK = 512
PEER_TOK = 512
PEER_EXP = 1024
NEG_INF = float("-inf")
LOG2_E = 1.4426950408889634


def _take_top(vals, k):
    rows = lax.broadcasted_iota(jnp.int32, vals.shape, 0)
    taken = []
    for _ in range(k):
        m = jnp.max(vals, axis=0, keepdims=True)
        first = jnp.min(jnp.where(vals == m, rows, vals.shape[0]), axis=0, keepdims=True)
        vals = jnp.where(rows == first, NEG_INF, vals)
        taken.append(m)
    return jnp.concatenate(taken, axis=0), vals


def _take_top_distinct(vals, k):
    taken = []
    for _ in range(k):
        m = jnp.max(vals, axis=0, keepdims=True)
        vals = jnp.where(vals == m, NEG_INF, vals)
        taken.append(m)
    removed = jnp.sum((vals == NEG_INF).astype(jnp.int32), axis=0, keepdims=True)
    return jnp.concatenate(taken, axis=0), vals, removed


def _peer_route_kernel(q_ref, sk_ref, s1m_ref, s2m_ref, thr_ref, c0_ref):
    def route(tok, take_top):
        masked, top_vals, removed = [], [], []
        for z in range(2):
            qz = q_ref[tok, z * LANE:(z + 1) * LANE]
            s = lax.dot_general(sk_ref[0, z], qz, (((1,), (1,)), ((), ())),
                                preferred_element_type=jnp.float32)
            v, rest, n = take_top(s, PEER_TOPK)
            masked.append(jnp.where(rest == NEG_INF, s, NEG_INF))
            top_vals.append(v)
            removed.append(n)
        half = PEER_TOPK // 2
        v1, v2 = top_vals
        cand = jnp.concatenate(
            [v1[0:1] + v2] + [v1[i:i + 1] + v2[:half] for i in range(1, half)] + [v1[half:] + v2[0:1]], axis=0)
        sv, _, n = take_top(cand, PEER_TOPK)
        removed.append(n)
        z_sum = jnp.sum(jnp.exp(sv - sv[0:1]), axis=0, keepdims=True)
        s1m_ref[0, :, tok] = masked[0]
        s2m_ref[0, :, tok] = masked[1]
        thr_ref[0, :, tok] = sv[PEER_TOPK - 1:PEER_TOPK]
        c0_ref[0, :, tok] = -sv[0:1] - jnp.log(z_sum)
        return jnp.max(functools.reduce(jnp.maximum, removed))

    def exact_take_top(vals, k):
        v, rest = _take_top(vals, k)
        return v, rest, jnp.full((1, vals.shape[1]), k, jnp.int32)

    def tile_pair(t, carry):
        toks = [pl.ds(pl.multiple_of((2 * t + i) * LANE, LANE), LANE) for i in range(2)]
        worst = [route(tok, _take_top_distinct) for tok in toks]
        for tok, n in zip(toks, worst):
            @pl.when(n > PEER_TOPK)
            def _():
                route(tok, exact_take_top)

        return carry

    lax.fori_loop(0, ROUTE_TOK // (2 * LANE), tile_pair, 0)


def _peer_route(q, subkeys):
    n = q.shape[0]
    big = pl.BlockSpec((1, N_KEYS, ROUTE_TOK), lambda i, h: (h, 0, i))
    small = pl.BlockSpec((1, 1, ROUTE_TOK), lambda i, h: (h, 0, i))
    return pl.pallas_call(
        _peer_route_kernel,
        grid=(n // ROUTE_TOK, PEER_HEADS),
        in_specs=[
            pl.BlockSpec((ROUTE_TOK, PEER_KEY_DIM), lambda i, h: (i, h)),
            pl.BlockSpec((1, 2, N_KEYS, PEER_KEY_DIM // 2), lambda i, h: (h, 0, 0, 0)),
        ],
        out_specs=[big, big, small, small],
        out_shape=[jax.ShapeDtypeStruct((PEER_HEADS, N_KEYS, n), jnp.float32)] * 2
        + [jax.ShapeDtypeStruct((PEER_HEADS, 1, n), jnp.float32)] * 2,
        compiler_params=pltpu.CompilerParams(
            dimension_semantics=("arbitrary", "arbitrary"), vmem_limit_bytes=VMEM_LIMIT_BYTES),
        name="peer_route",
    )(q, subkeys)


def _gelu_tanh(x):
    return 0.5 * x * (1.0 + jnp.tanh(0.7978845608028654 * (x + 0.044715 * (x * x * x))))


def _peer_dense_kernel(h_ref, x_ref, mod_ref, s1m_ref, s2m_ref, thr_ref, c0_ref, u_ref, vt_ref, o_ref,
                       ht_scr, s1l_scr, s2l_scr, coef_scr, acc_scr, *, gate_idx):
    e = pl.program_id(1)

    @pl.when(e == 0)
    def _():
        ht_scr[...] = h_ref[...].astype(jnp.float32).T.astype(jnp.bfloat16)
        s1l_scr[...] = s1m_ref[...] * LOG2_E
        s2l_scr[...] = (s2m_ref[...] + c0_ref[...]) * LOG2_E
        acc_scr[...] = jnp.zeros_like(acc_scr)

    act = jnp.dot(u_ref[...], ht_scr[...], preferred_element_type=jnp.float32)
    a_per_step = PEER_EXP // N_KEYS
    a_rows = pl.ds(pl.multiple_of(e * a_per_step, a_per_step), a_per_step)
    for t in range(PEER_TOK // LANE):
        tok = slice(t * LANE, (t + 1) * LANE)
        s1_all = [s1m_ref[hh, a_rows, tok] for hh in range(PEER_HEADS)]
        s1l_all = [s1l_scr[hh, a_rows, tok] for hh in range(PEER_HEADS)]
        for a in range(a_per_step):
            rows = slice(a * N_KEYS, (a + 1) * N_KEYS)
            w = jnp.zeros((N_KEYS, LANE), jnp.float32)
            for hh in range(PEER_HEADS):
                s1 = s1_all[hh][a:a + 1]
                picked = (s1 + s2m_ref[hh, :, tok]) >= thr_ref[hh, :, tok]
                w = w + jnp.where(picked, jnp.exp2(s1l_all[hh][a:a + 1] + s2l_scr[hh, :, tok]), 0.0)
            coef_scr[rows, tok] = (w * _gelu_tanh(act[rows, tok])).astype(jnp.bfloat16)
    acc_scr[...] += jnp.dot(vt_ref[...], coef_scr[...], preferred_element_type=jnp.float32)

    @pl.when(e == pl.num_programs(1) - 1)
    def _():
        o_ref[...] = x_ref[...] + mod_ref[0, gate_idx:gate_idx + 1, :] * acc_scr[...].T


def _peer_dense(h, x, mod, s1m, s2m, thr, c0, u_bf16, vt_bf16):
    n, d = x.shape
    assert PEER_TOK == ROW_TILE
    tok_row = lambda i, e: (i, 0)
    big = pl.BlockSpec((PEER_HEADS, N_KEYS, PEER_TOK), lambda i, e: (0, 0, i))
    small = pl.BlockSpec((PEER_HEADS, 1, PEER_TOK), lambda i, e: (0, 0, i))
    return pl.pallas_call(
        functools.partial(_peer_dense_kernel, gate_idx=5),
        grid=(n // PEER_TOK, N_EXPERTS // PEER_EXP),
        in_specs=[
            pl.BlockSpec((PEER_TOK, d), tok_row),
            pl.BlockSpec((PEER_TOK, d), tok_row),
            pl.BlockSpec((1, N_MOD, d), lambda i, e: (_segment_of_row_block(i), 0, 0)),
            big, big, small, small,
            pl.BlockSpec((PEER_EXP, d), lambda i, e: (e, 0)),
            pl.BlockSpec((d, PEER_EXP), lambda i, e: (0, e)),
        ],
        out_specs=pl.BlockSpec((PEER_TOK, d), tok_row),
        out_shape=jax.ShapeDtypeStruct((n, d), jnp.float32),
        scratch_shapes=[
            pltpu.VMEM((d, PEER_TOK), jnp.bfloat16),
            pltpu.VMEM((PEER_HEADS, N_KEYS, PEER_TOK), jnp.float32),
            pltpu.VMEM((PEER_HEADS, N_KEYS, PEER_TOK), jnp.float32),
            pltpu.VMEM((PEER_EXP, PEER_TOK), jnp.bfloat16),
            pltpu.VMEM((d, PEER_TOK), jnp.float32),
        ],
        compiler_params=pltpu.CompilerParams(
            dimension_semantics=("arbitrary", "arbitrary"), vmem_limit_bytes=VMEM_LIMIT_BYTES),
        name="peer_dense",
    )(h, x, mod, s1m, s2m, thr, c0, u_bf16, vt_bf16)


def _merge_kernel(x_ref, ya_ref, yb_ref, yc_ref, ga_ref, gb_ref, gc_ref, mod_ref,
                  wa_ref, wb_ref, wc_ref, wo_ref, o_ref, *, gate_idx):
    def branch(y_ref, gate_ref, w_ref):
        proj = jnp.dot(y_ref[...].astype(jnp.bfloat16), w_ref[...], preferred_element_type=jnp.float32)
        return jax.nn.sigmoid(gate_ref[...]) * proj

    merged = branch(ya_ref, ga_ref, wa_ref) + branch(yb_ref, gb_ref, wb_ref) + branch(yc_ref, gc_ref, wc_ref)
    out = jnp.dot(merged.astype(jnp.bfloat16), wo_ref[...], preferred_element_type=jnp.float32)
    o_ref[...] = x_ref[...] + mod_ref[0, gate_idx:gate_idx + 1, :] * out


def _merge(x, ya, yb, yc, p, mod, wa, wb, wc, wo):
    n_rows, d = x.shape
    gate_col0 = 0
    row = lambda i: (i, 0)
    const = lambda i: (0, 0)
    y_spec = pl.BlockSpec((ROW_TILE, ya.shape[1]), row)
    w_spec = pl.BlockSpec(wa.shape, const)
    return pl.pallas_call(
        functools.partial(_merge_kernel, gate_idx=2),
        grid=(n_rows // ROW_TILE,),
        in_specs=[
            pl.BlockSpec((ROW_TILE, d), row), y_spec, y_spec, y_spec,
            pl.BlockSpec((ROW_TILE, d), lambda i: (i, gate_col0)),
            pl.BlockSpec((ROW_TILE, d), lambda i: (i, gate_col0 + 1)),
            pl.BlockSpec((ROW_TILE, d), lambda i: (i, gate_col0 + 2)),
            pl.BlockSpec((1, N_MOD, d), lambda i: (_segment_of_row_block(i), 0, 0)),
            w_spec, w_spec, w_spec, pl.BlockSpec(wo.shape, const),
        ],
        out_specs=pl.BlockSpec((ROW_TILE, d), row),
        out_shape=jax.ShapeDtypeStruct((n_rows, d), jnp.float32),
        compiler_params=pltpu.CompilerParams(
            dimension_semantics=("arbitrary",), vmem_limit_bytes=VMEM_LIMIT_BYTES),
        name="merge_out_proj",
    )(x, ya, yb, yc, p, p, p, mod, wa, wb, wc, wo)


HEAD_PAIRS = B_HEADS // 2
PREP_TOK = 256
SCAN_CHUNK = 128
assert PREP_TOK == SEQ and DEC_SEQ % PREP_TOK == 0 and 2 * HEAD_DIM == LANE
_HI = lax.Precision.HIGHEST


def _head_ones(n):
    r = lax.broadcasted_iota(jnp.int32, (n, n), 0) // HEAD_DIM
    c = lax.broadcasted_iota(jnp.int32, (n, n), 1) // HEAD_DIM
    return (r == c).astype(jnp.float32)


def _softplus(y):
    return jnp.maximum(y, 0.0) + jnp.log(1.0 + jnp.exp(-jnp.abs(y)))


def _rwkv_prep_kernel(pb_ref, prev_ref, next_ref, mu_ref, w0_ref, w2_ref, a0_ref, a2_ref, g2_ref,
                      kks_ref, ka_ref, rk_ref, ones_ref,
                      r_o, kk_o, v_o, w0_o, w1_o, k0_o, k1_o, nb0_o, nb1_o, g_o, bonus_o):
    i = pl.program_id(0)
    n_ctx = N_CTX_TOK // PREP_TOK
    per_lat = DEC_SEQ // PREP_TOK
    j = (i - n_ctx) % per_lat
    starts_seq = jnp.logical_or(i < n_ctx, j == 0)
    ends_seq = jnp.logical_or(i < n_ctx, j == per_lat - 1)

    pb = pb_ref[...]
    rows = lax.broadcasted_iota(jnp.int32, (PREP_TOK, 1), 0)
    before = jnp.where(starts_seq, 0.0, prev_ref[7:8, :])
    after = jnp.where(ends_seq, 0.0, next_ref[0:1, :])
    prev = jnp.where(rows == 0, before, pltpu.roll(pb, 1, 0))
    nxt = jnp.where(rows == PREP_TOK - 1, after, pltpu.roll(pb, PREP_TOK - 1, 0))
    xb = pb + mu_ref[...] * (0.5 * (prev + nxt) - pb)

    w = B_WIDTH
    r, k, v = xb[:, 0:w], xb[:, w:2 * w], xb[:, 2 * w:3 * w]
    wd = xb[:, 3 * w:3 * w + 2 * LORA_W]
    ad = xb[:, 3 * w + 2 * LORA_W:3 * w + 2 * LORA_W + 2 * LORA_A]
    gd = xb[:, 3 * w + 2 * LORA_W + 2 * LORA_A:]
    ones = ones_ref[...]

    def head_sum(x):
        return jnp.dot(x, ones, preferred_element_type=jnp.float32, precision=_HI)

    lw = jnp.dot(jnp.tanh(wd), w2_ref[...], preferred_element_type=jnp.float32, precision=_HI)
    la = jnp.dot(ad, a2_ref[...], preferred_element_type=jnp.float32, precision=_HI)
    kk = k * kks_ref[...]
    kk = kk * lax.rsqrt(head_sum(kk * kk) + 1e-12)
    kd_sum = jnp.zeros_like(k)
    for z, (w_o, k_o, nb_o) in enumerate(((w0_o, k0_o, nb0_o), (w1_o, k1_o, nb1_o))):
        wlog = -_softplus(-(w0_ref[z:z + 1, :] + lw[:, z * w:(z + 1) * w])) - 0.5
        a = jax.nn.sigmoid(a0_ref[z:z + 1, :] + la[:, z * w:(z + 1) * w])
        kd = k * (1.0 + (a - 1.0) * ka_ref[...])
        kd_sum = kd_sum + kd
        w_o[...] = jnp.exp(-jnp.exp(wlog))
        k_o[...] = kd
        nb_o[...] = -(kk * a)
    r_o[...] = r
    kk_o[...] = kk
    v_o[...] = v
    g_o[...] = jnp.dot(jax.nn.sigmoid(gd), g2_ref[...], preferred_element_type=jnp.float32, precision=_HI)
    bonus_o[...] = head_sum(r * kd_sum * rk_ref[...]) * v


def _rwkv_prep(p, mu, w0, w2_pad, a0, a2_pad, g2, kk_scale, ka, rk):
    n = p.shape[0]
    col_blk = (N_BRANCH * D_MODEL + A_COLS) // B_COLS
    assert col_blk * B_COLS == N_BRANCH * D_MODEL + A_COLS
    halo = 8
    per = PREP_TOK // halo
    full = lambda a: pl.BlockSpec(a.shape, lambda i: (0,) * a.ndim)
    ones = _head_ones(B_WIDTH)
    consts = (mu, w0, w2_pad, a0, a2_pad, g2, kk_scale, ka, rk, ones)
    out_spec = pl.BlockSpec((PREP_TOK, B_WIDTH), lambda i: (i, 0))
    return pl.pallas_call(
        _rwkv_prep_kernel,
        grid=(n // PREP_TOK,),
        in_specs=[
            pl.BlockSpec((PREP_TOK, B_COLS), lambda i: (i, col_blk)),
            pl.BlockSpec((halo, B_COLS), lambda i: (jnp.maximum(i * per - 1, 0), col_blk)),
            pl.BlockSpec((halo, B_COLS), lambda i: (jnp.minimum((i + 1) * per, n // halo - 1), col_blk)),
        ] + [full(a) for a in consts],
        out_specs=[out_spec] * 11,
        out_shape=[jax.ShapeDtypeStruct((n, B_WIDTH), jnp.float32)] * 11,
        compiler_params=pltpu.CompilerParams(
            dimension_semantics=("arbitrary",), vmem_limit_bytes=VMEM_LIMIT_BYTES),
        name="rwkv_prep",
    )(p, p, p, *consts)


SCAN_LANES = 2 * LANE


def _head_sums(head_ones, *terms):
    sums = []
    for x, passes in terms:
        hi = x.astype(jnp.bfloat16)
        out = jnp.dot(hi, head_ones, preferred_element_type=jnp.float32)
        if passes == 2:
            lo = (x - hi.astype(jnp.float32)).astype(jnp.bfloat16)
            out = out + jnp.dot(lo, head_ones, preferred_element_type=jnp.float32)
        sums.append(out)
    return sums


def _rwkv_scan_kernel(rf, kkf, vf, wf, kf, nbf, rb, kkb, vb, wb, kb, nbb, s0_ref, ones_ref, eye_ref,
                      yf_ref, yb_ref, sfin_ref, s_scr, *, n_seq):
    c = pl.program_id(1)

    @pl.when(c == 0)
    def _():
        s_scr[...] = s0_ref[...]

    head_ones = ones_ref[...]
    eye = eye_ref[...]
    fwd_refs = (rf, kkf, vf, wf, kf, nbf)
    bwd_refs = (rb, kkb, vb, wb, kb, nbb)
    R, KK, V, W, K, NB = range(6)

    def step(t, carry):
        t_f = pl.ds(t, 1)
        t_b = pl.ds(SCAN_CHUNK - 1 - t, 1)
        rows_f = [[a[g, t_f, :] for a in fwd_refs] for g in range(n_seq)]
        rows_b = [[a[g, t_b, :] for a in bwd_refs] for g in range(n_seq)]

        def both(which, chain):
            g, p = chain
            lanes = slice(p * LANE, (p + 1) * LANE)
            return jnp.concatenate([rows_f[g][which][:, lanes], rows_b[g][which][:, lanes]], axis=1)

        chains = [(g, p) for g in range(n_seq) for p in range(HEAD_PAIRS)]
        rows = {ch: slice(i * HEAD_DIM, (i + 1) * HEAD_DIM) for i, ch in enumerate(chains)}
        s_old = {ch: s_scr[ch[0], ch[1]] for ch in chains}
        s_kk, v_col = _head_sums(
            head_ones,
            (jnp.concatenate([s_old[ch] * both(KK, ch) for ch in chains], axis=0), 1),
            (jnp.concatenate([eye * both(V, ch) for ch in chains], axis=0), 1))
        s_new = {}
        for ch in chains:
            s = s_old[ch] * both(W, ch) + s_kk[rows[ch]] * both(NB, ch) + v_col[rows[ch]] * both(K, ch)
            s_scr[ch[0], ch[1]] = s
            s_new[ch] = s
        (y_all,) = _head_sums(
            head_ones, (jnp.concatenate([s_new[ch] * both(R, ch) for ch in chains], axis=0), 1))
        for g in range(n_seq):
            y_rows = [jnp.sum(y_all[rows[(g, p)]] * eye, axis=0, keepdims=True) for p in range(HEAD_PAIRS)]
            yf_ref[g, t_f, :] = jnp.concatenate([y[:, :LANE] for y in y_rows], axis=1)
            yb_ref[g, t_b, :] = jnp.concatenate([y[:, LANE:] for y in y_rows], axis=1)
        return carry

    lax.fori_loop(0, SCAN_CHUNK, step, 0, unroll=16)

    @pl.when(c == pl.num_programs(1) - 1)
    def _():
        sfin_ref[...] = s_scr[...]


def _rwkv_scan(flat, s0, t_len, first_seq, seq_group):
    r, kk, v, w0, w1, k0, k1, nb0, nb1 = (a.reshape(-1, t_len, B_WIDTH) for a in flat)
    n_seq = s0.shape[0]
    n_chunks = t_len // SCAN_CHUNK
    assert n_seq % seq_group == 0 and first_seq % seq_group == 0 and n_chunks * SCAN_CHUNK == t_len
    first_blk = first_seq // seq_group
    blk = (seq_group, SCAN_CHUNK, B_WIDTH)
    fwd_in = pl.BlockSpec(blk, lambda i, c: (first_blk + i, c, 0))
    bwd_in = pl.BlockSpec(blk, lambda i, c: (first_blk + i, n_chunks - 1 - c, 0))
    fwd = pl.BlockSpec(blk, lambda i, c: (i, c, 0))
    bwd = pl.BlockSpec(blk, lambda i, c: (i, n_chunks - 1 - c, 0))
    state_blk = (seq_group, HEAD_PAIRS, HEAD_DIM, SCAN_LANES)
    state = pl.BlockSpec(state_blk, lambda i, c: (i, 0, 0, 0))
    head_ones = _head_ones(SCAN_LANES).astype(jnp.bfloat16)
    eye = (lax.broadcasted_iota(jnp.int32, (HEAD_DIM, SCAN_LANES), 0)
           == lax.broadcasted_iota(jnp.int32, (HEAD_DIM, SCAN_LANES), 1) % HEAD_DIM).astype(jnp.float32)
    const = lambda a: pl.BlockSpec(a.shape, lambda i, c: (0, 0))
    y_shape = jax.ShapeDtypeStruct((n_seq, t_len, B_WIDTH), jnp.float32)
    return pl.pallas_call(
        functools.partial(_rwkv_scan_kernel, n_seq=seq_group),
        grid=(n_seq // seq_group, n_chunks),
        in_specs=[fwd_in] * 6 + [bwd_in] * 6 + [state, const(head_ones), const(eye)],
        out_specs=[fwd, bwd, state],
        out_shape=[y_shape, y_shape, jax.ShapeDtypeStruct(s0.shape, jnp.float32)],
        scratch_shapes=[pltpu.VMEM(state_blk, jnp.float32)],
        compiler_params=pltpu.CompilerParams(
            dimension_semantics=("arbitrary", "arbitrary"), vmem_limit_bytes=VMEM_LIMIT_BYTES),
        name="rwkv_scan",
    )(r, kk, v, w0, k0, nb0, r, kk, v, w1, k1, nb1, s0, head_ones, eye)


def _rwkv_post_kernel(yfc_ref, ybc_ref, yfl_ref, ybl_ref, g_ref, bonus_ref, lng_ref, lnb_ref, ones_ref, o_ref):
    ones = ones_ref[...]
    is_ctx = pl.program_id(0) < N_CTX_TOK // PREP_TOK
    y = jnp.where(is_ctx, yfc_ref[...] + ybc_ref[...], yfl_ref[...] + ybl_ref[...])
    mean = jnp.dot(y, ones, preferred_element_type=jnp.float32, precision=_HI) * (1.0 / HEAD_DIM)
    d = y - mean
    var = jnp.dot(d * d, ones, preferred_element_type=jnp.float32, precision=_HI) * (1.0 / HEAD_DIM)
    y = d * lax.rsqrt(var + GN_EPS) * lng_ref[...] + lnb_ref[...]
    o_ref[...] = (y + bonus_ref[...]) * g_ref[...]


def _rwkv_post(yf_ctx, yb_ctx, yf_lat, yb_lat, g, bonus, lnx_g, lnx_b):
    n = g.shape[0]
    n_ctx = N_CTX_TOK // PREP_TOK
    n_lat = N_LAT_TOK // PREP_TOK
    row = pl.BlockSpec((PREP_TOK, B_WIDTH), lambda i: (i, 0))
    ctx_row = pl.BlockSpec((PREP_TOK, B_WIDTH), lambda i: (jnp.minimum(i, n_ctx - 1), 0))
    lat_row = pl.BlockSpec((PREP_TOK, B_WIDTH), lambda i: (jnp.clip(i - n_ctx, 0, n_lat - 1), 0))
    vec = pl.BlockSpec((1, B_WIDTH), lambda i: (0, 0))
    ones = _head_ones(B_WIDTH)
    return pl.pallas_call(
        _rwkv_post_kernel,
        grid=(n // PREP_TOK,),
        in_specs=[ctx_row, ctx_row, lat_row, lat_row, row, row, vec, vec, pl.BlockSpec(ones.shape, lambda i: (0, 0))],
        out_specs=row,
        out_shape=jax.ShapeDtypeStruct((n, B_WIDTH), jnp.float32),
        compiler_params=pltpu.CompilerParams(
            dimension_semantics=("arbitrary",), vmem_limit_bytes=VMEM_LIMIT_BYTES),
        name="rwkv_post",
    )(yf_ctx, yb_ctx, yf_lat, yb_lat, g, bonus, lnx_g, lnx_b, ones)


def _state_to_pairs(s):
    b = s.shape[0]
    s = s.reshape(b, 2, HEAD_PAIRS, 2, HEAD_DIM, HEAD_DIM)
    return jnp.transpose(s, (0, 2, 4, 1, 3, 5)).reshape(b, HEAD_PAIRS, HEAD_DIM, SCAN_LANES)


def _state_from_pairs(s):
    b = s.shape[0]
    s = s.reshape(b, HEAD_PAIRS, HEAD_DIM, 2, 2, HEAD_DIM)
    return jnp.transpose(s, (0, 3, 1, 4, 2, 5)).reshape(b, 2, B_HEADS, HEAD_DIM, HEAD_DIM)


def _rwkv_mixer(p, lp, state_lat):
    zeros = jnp.zeros((LORA_W, B_WIDTH), jnp.float32)
    w2_pad = jnp.concatenate([jnp.concatenate([lp["rw_w2"][0], zeros], axis=1),
                              jnp.concatenate([zeros, lp["rw_w2"][1]], axis=1)], axis=0)
    a2_pad = jnp.concatenate([jnp.concatenate([lp["rw_a2"][0], zeros], axis=1),
                              jnp.concatenate([zeros, lp["rw_a2"][1]], axis=1)], axis=0)
    row = lambda a: a.reshape(1, -1)
    prep = _rwkv_prep(p, row(lp["rw_mu"]), lp["rw_w0"], w2_pad, lp["rw_a0"], a2_pad, lp["rw_g2"],
                      row(lp["rw_kk"]), row(lp["rw_ka"]), row(lp["rw_rk"]))
    scan_in, (g, bonus) = prep[:9], prep[9:]
    s0_ctx = jnp.zeros((BATCH, HEAD_PAIRS, HEAD_DIM, SCAN_LANES), jnp.float32)
    yf_c, yb_c, s_ctx = _rwkv_scan(scan_in, s0_ctx, t_len=SEQ, first_seq=0, seq_group=4)
    yf_l, yb_l, _ = _rwkv_scan(scan_in, _state_to_pairs(state_lat), t_len=DEC_SEQ,
                               first_seq=N_CTX_TOK // DEC_SEQ, seq_group=DEC_BATCH)
    flat = lambda a: a.reshape(-1, B_WIDTH)
    y = _rwkv_post(flat(yf_c), flat(yb_c), flat(yf_l), flat(yb_l), g, bonus,
                   row(lp["rw_lnx_g"]), row(lp["rw_lnx_b"]))
    return y, _state_from_pairs(s_ctx)


ATT_SCALE = HEAD_DIM ** -0.5
COL0_QA = N_BRANCH * D_MODEL
COL0_KA = COL0_QA + A_HEADS * HEAD_DIM
COL0_VA = COL0_KA + A_KV * HEAD_DIM
COL0_QC = COL0_QA + A_COLS + B_COLS
COL0_KC = COL0_QC + C_HEADS * HEAD_DIM
COL0_VC = COL0_KC + C_HEADS * HEAD_DIM
A_GROUP_COLS = A_GROUP * HEAD_DIM
assert A_KV * HEAD_DIM == LANE and all(c % LANE == 0 for c in (COL0_QA, COL0_KA, COL0_VA, COL0_QC, COL0_KC, COL0_VC))
assert COL0_QA % A_GROUP_COLS == 0
GRID_ROWS = DEC_SEQ // GRID_W
NA_WIN_ROWS = min(NA_ROWS, GRID_ROWS)


def _attend(q, parts, sink=None):
    scores = []
    for k, _, bias in parts:
        s = lax.dot_general(q, k, (((1,), (1,)), ((), ())), preferred_element_type=jnp.float32)
        scores.append(s if bias is None else s + bias)
    m = functools.reduce(jnp.maximum, [jnp.max(s, axis=-1, keepdims=True) for s in scores])
    den = 0.0
    if sink is not None:
        m = jnp.maximum(m, sink)
        den = jnp.exp(sink - m)
    out = 0.0
    for s, (_, v, _) in zip(scores, parts):
        prob = jnp.exp(s - m)
        den = den + jnp.sum(prob, axis=-1, keepdims=True)
        out = out + jnp.dot(prob.astype(jnp.bfloat16), v, preferred_element_type=jnp.float32)
    return out / den


def _scaled_bf16(q):
    return (q * ATT_SCALE).astype(jnp.bfloat16)


def _kv_half(x, kv):
    return jnp.where(kv == 0, x[:, :HEAD_DIM], x[:, HEAD_DIM:])


def _ctx_gqa_kernel(q_ref, k_ref, v_ref, sink_ref, o_ref):
    kv = pl.program_id(1)
    k = _kv_half(k_ref[...], kv).astype(jnp.bfloat16)
    v = _kv_half(v_ref[...], kv).astype(jnp.bfloat16)
    outs = []
    for g in range(A_GROUP):
        q = _scaled_bf16(q_ref[:, g * HEAD_DIM:(g + 1) * HEAD_DIM])
        sink = sink_ref[pl.ds(kv * A_GROUP + g, 1), :][:, 0:1]
        outs.append(_attend(q, [(k, v, None)], sink))
    o_ref[...] = jnp.concatenate(outs, axis=1)


def _ctx_gqa(p, sink_rows):
    q0, k0, v0 = COL0_QA // A_GROUP_COLS, COL0_KA // LANE, COL0_VA // LANE
    return pl.pallas_call(
        _ctx_gqa_kernel,
        grid=(BATCH, A_KV),
        in_specs=[
            pl.BlockSpec((SEQ, A_GROUP_COLS), lambda b, kv: (b, q0 + kv)),
            pl.BlockSpec((SEQ, LANE), lambda b, kv: (b, k0)),
            pl.BlockSpec((SEQ, LANE), lambda b, kv: (b, v0)),
            pl.BlockSpec(sink_rows.shape, lambda b, kv: (0, 0)),
        ],
        out_specs=pl.BlockSpec((SEQ, A_GROUP_COLS), lambda b, kv: (b, kv)),
        out_shape=jax.ShapeDtypeStruct((N_CTX_TOK, A_HEADS * HEAD_DIM), jnp.float32),
        compiler_params=pltpu.CompilerParams(
            dimension_semantics=("arbitrary", "arbitrary"), vmem_limit_bytes=VMEM_LIMIT_BYTES),
        name="ctx_gqa",
    )(p, p, p, sink_rows)


def _ctx_mha_kernel(q_ref, k_ref, v_ref, o_ref):
    outs = []
    for j in range(2):
        lanes = slice(j * HEAD_DIM, (j + 1) * HEAD_DIM)
        outs.append(_attend(_scaled_bf16(q_ref[:, lanes]),
                            [(k_ref[:, lanes].astype(jnp.bfloat16), v_ref[:, lanes].astype(jnp.bfloat16), None)]))
    o_ref[...] = jnp.concatenate(outs, axis=1)


def _ctx_mha(p):
    q0, k0, v0 = COL0_QC // LANE, COL0_KC // LANE, COL0_VC // LANE
    return pl.pallas_call(
        _ctx_mha_kernel,
        grid=(BATCH, C_HEADS // 2),
        in_specs=[
            pl.BlockSpec((SEQ, LANE), lambda b, hp: (b, q0 + hp)),
            pl.BlockSpec((SEQ, LANE), lambda b, hp: (b, k0 + hp)),
            pl.BlockSpec((SEQ, LANE), lambda b, hp: (b, v0 + hp)),
        ],
        out_specs=pl.BlockSpec((SEQ, LANE), lambda b, hp: (b, hp)),
        out_shape=jax.ShapeDtypeStruct((N_CTX_TOK, C_HEADS * HEAD_DIM), jnp.float32),
        compiler_params=pltpu.CompilerParams(
            dimension_semantics=("arbitrary", "arbitrary"), vmem_limit_bytes=VMEM_LIMIT_BYTES),
        name="ctx_mha",
    )(p, p, p)


def _rope(x, cos, sin):
    reps = x.shape[1] // LANE
    if reps > 1:
        cos = jnp.concatenate([cos] * reps, axis=1)
        sin = jnp.concatenate([sin] * reps, axis=1)
    lane = lax.broadcasted_iota(jnp.int32, x.shape, 1)
    first = (lane % (HEAD_DIM // 2)) < (HEAD_DIM // 4)
    quarter = HEAD_DIM // 4
    pieces = [x[:, i * LANE:(i + 1) * LANE] for i in range(reps)]
    up = jnp.concatenate([pltpu.roll(piece, LANE - quarter, 1) for piece in pieces], axis=1)
    down = jnp.concatenate([pltpu.roll(piece, quarter, 1) for piece in pieces], axis=1)
    return x * cos + jnp.where(first, -up, down) * sin


def _lat_window_kernel(q_ref, kp_ref, kc_ref, kn_ref, vp_ref, vc_ref, vn_ref, ck_ref, cv_ref,
                       cosq_ref, sinq_ref, cosp_ref, sinp_ref, cosn_ref, sinn_ref, sink_ref, o_ref):
    qb = pl.program_id(1)
    kv = pl.program_id(2)
    n_blocks = pl.num_programs(1)
    row = lax.broadcasted_iota(jnp.int32, (Q_BLOCK, Q_BLOCK), 0)
    col = lax.broadcasted_iota(jnp.int32, (Q_BLOCK, Q_BLOCK), 1)
    bias_prev = jnp.where(jnp.logical_and(col >= row, qb > 0), 0.0, NEG_INF)
    bias_next = jnp.where(jnp.logical_and(col <= row, qb < n_blocks - 1), 0.0, NEG_INF)
    assert WINDOW == Q_BLOCK

    def keys(k_ref, cos_ref, sin_ref):
        return _kv_half(_rope(k_ref[...], cos_ref[...], sin_ref[...]), kv).astype(jnp.bfloat16)

    def vals(v_ref):
        return _kv_half(v_ref[...], kv).astype(jnp.bfloat16)

    parts = [
        (keys(kp_ref, cosp_ref, sinp_ref), vals(vp_ref), bias_prev),
        (keys(kc_ref, cosq_ref, sinq_ref), vals(vc_ref), None),
        (keys(kn_ref, cosn_ref, sinn_ref), vals(vn_ref), bias_next),
        (_kv_half(ck_ref[0], kv).astype(jnp.bfloat16), _kv_half(cv_ref[0], kv).astype(jnp.bfloat16), None),
    ]
    q_all = _rope(q_ref[...], cosq_ref[...], sinq_ref[...])
    outs = []
    for g in range(A_GROUP):
        sink = sink_ref[pl.ds(kv * A_GROUP + g, 1), :][:, 0:1]
        outs.append(_attend(_scaled_bf16(q_all[:, g * HEAD_DIM:(g + 1) * HEAD_DIM]), parts, sink))
    o_ref[...] = jnp.concatenate(outs, axis=1)


def _lat_window(p, cache_k, cache_v, cos, sin, sink_rows):
    n_blocks = DEC_SEQ // Q_BLOCK
    base = N_CTX_TOK // Q_BLOCK
    q0, k0, v0 = COL0_QA // A_GROUP_COLS, COL0_KA // LANE, COL0_VA // LANE
    prev = lambda b, qb, kv: base + b * n_blocks + jnp.maximum(qb - 1, 0)
    here = lambda b, qb, kv: base + b * n_blocks + qb
    nxt = lambda b, qb, kv: base + b * n_blocks + jnp.minimum(qb + 1, n_blocks - 1)
    slab = lambda rows, c0: pl.BlockSpec((Q_BLOCK, LANE), lambda b, qb, kv: (rows(b, qb, kv), c0))
    table = lambda blk: pl.BlockSpec((Q_BLOCK, LANE), lambda b, qb, kv: (blk(b, qb, kv), 0))
    cache = pl.BlockSpec((1, PAST_LEN, LANE), lambda b, qb, kv: (b, 0, 0))
    return pl.pallas_call(
        _lat_window_kernel,
        grid=(DEC_BATCH, n_blocks, A_KV),
        in_specs=[
            pl.BlockSpec((Q_BLOCK, A_GROUP_COLS), lambda b, qb, kv: (here(b, qb, kv), q0 + kv)),
            slab(prev, k0), slab(here, k0), slab(nxt, k0),
            slab(prev, v0), slab(here, v0), slab(nxt, v0),
            cache, cache,
            table(lambda b, qb, kv: qb), table(lambda b, qb, kv: qb),
            table(lambda b, qb, kv: jnp.maximum(qb - 1, 0)), table(lambda b, qb, kv: jnp.maximum(qb - 1, 0)),
            table(lambda b, qb, kv: jnp.minimum(qb + 1, n_blocks - 1)),
            table(lambda b, qb, kv: jnp.minimum(qb + 1, n_blocks - 1)),
            pl.BlockSpec(sink_rows.shape, lambda b, qb, kv: (0, 0)),
        ],
        out_specs=pl.BlockSpec((Q_BLOCK, A_GROUP_COLS), lambda b, qb, kv: (b * n_blocks + qb, kv)),
        out_shape=jax.ShapeDtypeStruct((N_LAT_TOK, A_HEADS * HEAD_DIM), jnp.float32),
        compiler_params=pltpu.CompilerParams(
            dimension_semantics=("arbitrary", "arbitrary", "arbitrary"), vmem_limit_bytes=VMEM_LIMIT_BYTES),
        name="lat_window",
    )(p, p, p, p, p, p, p, cache_k, cache_v, cos, sin, cos, sin, cos, sin, sink_rows)


N_DC = 2 * NA_COLS - 1
N_DR = 2 * NA_ROWS - 1


def _na_bias_kernel(rpb_ref, o_ref):
    h = pl.program_id(0)
    shape = (GRID_W, 2 * GRID_W)
    qc = lax.broadcasted_iota(jnp.int32, shape, 0)
    kc = lax.broadcasted_iota(jnp.int32, shape, 1) % GRID_W
    second = lax.broadcasted_iota(jnp.int32, shape, 1) >= GRID_W
    start = jnp.clip(qc - NA_COLS // 2, 0, GRID_W - NA_COLS)
    inside = jnp.logical_and(kc >= start, kc < start + NA_COLS)
    dc = jnp.clip(kc - qc, -(NA_COLS - 1), NA_COLS - 1) + NA_COLS - 1
    for dr in range(N_DR - 1):
        bias = jnp.zeros(shape, jnp.float32)
        for d in range(N_DC):
            lo = rpb_ref[(h * N_DR + dr) * N_DC + d]
            hi = rpb_ref[(h * N_DR + dr + 1) * N_DC + d]
            bias = jnp.where(dc == d, jnp.where(second, hi, lo), bias)
        o_ref[0, dr] = jnp.where(inside, bias, NEG_INF)


def _na_bias(rpb):
    return pl.pallas_call(
        _na_bias_kernel,
        grid=(C_HEADS,),
        in_specs=[pl.BlockSpec(memory_space=pltpu.SMEM)],
        out_specs=pl.BlockSpec((1, N_DR - 1, GRID_W, 2 * GRID_W), lambda h: (h, 0, 0, 0)),
        out_shape=jax.ShapeDtypeStruct((C_HEADS, N_DR - 1, GRID_W, 2 * GRID_W), jnp.float32),
        compiler_params=pltpu.CompilerParams(dimension_semantics=("arbitrary",)),
        name="na_bias",
    )(rpb.reshape(-1))


def _lat_neighbourhood_kernel(q_ref, k_ref, v_ref, ck_ref, cv_ref, bias_ref, o_ref):
    hp = pl.program_id(1)
    r = pl.program_id(2)
    first_row = jnp.clip(r - NA_WIN_ROWS // 2, 0, GRID_ROWS - NA_WIN_ROWS)
    local = pl.ds(pl.multiple_of(first_row * GRID_W, GRID_W), NA_WIN_ROWS * GRID_W)
    dr0 = first_row - r + NA_ROWS - 1
    outs = []
    for j in range(2):
        lanes = slice(j * HEAD_DIM, (j + 1) * HEAD_DIM)
        bias = jnp.concatenate([bias_ref[hp * 2 + j, dr0 + a] for a in range(0, NA_WIN_ROWS, 2)], axis=1)
        parts = [
            (k_ref[local, :][:, lanes].astype(jnp.bfloat16), v_ref[local, :][:, lanes].astype(jnp.bfloat16), bias),
            (ck_ref[0, :, lanes].astype(jnp.bfloat16), cv_ref[0, :, lanes].astype(jnp.bfloat16), None),
        ]
        outs.append(_attend(_scaled_bf16(q_ref[:, lanes]), parts))
    o_ref[...] = jnp.concatenate(outs, axis=1)


def _lat_neighbourhood(p, cache_k, cache_v, bias_tiles):
    assert NA_WIN_ROWS % 2 == 0 and N_CTX_TOK % DEC_SEQ == 0
    q0, k0, v0 = COL0_QC // LANE, COL0_KC // LANE, COL0_VC // LANE
    row_base = N_CTX_TOK // GRID_W
    seq_base = N_CTX_TOK // DEC_SEQ
    whole = lambda c0: pl.BlockSpec((DEC_SEQ, LANE), lambda b, hp, r: (seq_base + b, c0 + hp))
    cache = pl.BlockSpec((1, PAST_LEN, LANE), lambda b, hp, r: (b, 0, hp))
    return pl.pallas_call(
        _lat_neighbourhood_kernel,
        grid=(DEC_BATCH, C_HEADS // 2, GRID_ROWS),
        in_specs=[
            pl.BlockSpec((GRID_W, LANE), lambda b, hp, r: (row_base + b * GRID_ROWS + r, q0 + hp)),
            whole(k0), whole(v0), cache, cache,
            pl.BlockSpec(bias_tiles.shape, lambda b, hp, r: (0, 0, 0, 0)),
        ],
        out_specs=pl.BlockSpec((GRID_W, LANE), lambda b, hp, r: (b * GRID_ROWS + r, hp)),
        out_shape=jax.ShapeDtypeStruct((N_LAT_TOK, C_HEADS * HEAD_DIM), jnp.float32),
        compiler_params=pltpu.CompilerParams(
            dimension_semantics=("arbitrary", "arbitrary", "arbitrary"), vmem_limit_bytes=VMEM_LIMIT_BYTES),
        name="lat_neighbourhood",
    )(p, p, p, cache_k, cache_v, bias_tiles)


def _rope_tables():
    quarter = HEAD_DIM // 4
    t = jnp.arange(DEC_SEQ)
    pos = jnp.stack([t // GRID_W, t % GRID_W], axis=-1).astype(jnp.float32)
    freqs = ROPE_BASE ** (-jnp.arange(quarter, dtype=jnp.float32) / quarter)
    ang = pos[:, :, None] * freqs
    ang = jnp.concatenate([ang, ang], axis=-1).reshape(DEC_SEQ, HEAD_DIM)
    ang = jnp.concatenate([ang] * (LANE // HEAD_DIM), axis=-1)
    return jnp.cos(ang), jnp.sin(ang)


def _final_norm_kernel(x_ref, g_ref, o_ref):
    x = x_ref[...]
    o_ref[...] = x * lax.rsqrt(jnp.mean(x * x, axis=-1, keepdims=True) + RMS_EPS) * g_ref[...]


def _final_norm(x, g):
    n, d = x.shape
    return pl.pallas_call(
        _final_norm_kernel,
        grid=(n // ROW_TILE,),
        in_specs=[pl.BlockSpec((ROW_TILE, d), lambda i: (i, 0)), pl.BlockSpec((1, d), lambda i: (0, 0))],
        out_specs=pl.BlockSpec((ROW_TILE, d), lambda i: (i, 0)),
        out_shape=jax.ShapeDtypeStruct((n, d), jnp.float32),
        compiler_params=pltpu.CompilerParams(dimension_semantics=("arbitrary",)),
        name="final_norm",
    )(x, g)


def kernel(x_prompt, x_sample, cache_a_k, cache_a_v, cache_c_k, cache_c_v, state_rwkv, c, c_ctx, ln1_g, ln2_g, lnf_g, ada_w, ada_b, w_in, a_sink, a_out, rw_mu, rw_w0, rw_w2, rw_a0, rw_a2, rw_g2, rw_kk, rw_ka, rw_rk, rw_lnx_g, rw_lnx_b, rw_out, na_rpb, na_out, w_o, pe_q, pe_subkeys, pe_u, pe_v):
    bf16 = jnp.bfloat16
    x = jnp.concatenate([x_prompt.reshape(N_CTX_TOK, D_MODEL), x_sample.reshape(N_LAT_TOK, D_MODEL)], axis=0)
    silu_all = jnp.concatenate([jax.nn.silu(c_ctx)[None, :], jax.nn.silu(c)], axis=0)
    n_mix = A_COLS + B_COLS + C_COLS
    rope_cos, rope_sin = _rope_tables()
    list_ak, list_av, list_ck, list_cv, list_st = [], [], [], [], []
    for l in range(DEPTH):
        lp = {
            "rw_mu": rw_mu[l], "rw_w0": rw_w0[l], "rw_w2": rw_w2[l], "rw_a0": rw_a0[l],
            "rw_a2": rw_a2[l], "rw_g2": rw_g2[l], "rw_kk": rw_kk[l], "rw_ka": rw_ka[l],
            "rw_rk": rw_rk[l], "rw_lnx_g": rw_lnx_g[l], "rw_lnx_b": rw_lnx_b[l],
        }
        mod = (silu_all @ ada_w[l] + ada_b[l]).reshape(N_SEG, N_MOD, D_MODEL)

        w_in_l = jnp.concatenate([w_in[l][:, n_mix:], w_in[l][:, :n_mix]], axis=1).astype(bf16)
        p, _ = _norm_proj(x, ln1_g[l][None, :], mod, w_in_l, 0, 1, col_tile=IN_COLS // 3)
        sink_rows = jnp.broadcast_to(a_sink[l][:, None], (A_HEADS, LANE))
        flat_cache = lambda cache: cache[:, l].reshape(DEC_BATCH, PAST_LEN, -1)
        ya = jnp.concatenate([
            _ctx_gqa(p, sink_rows),
            _lat_window(p, flat_cache(cache_a_k), flat_cache(cache_a_v), rope_cos, rope_sin, sink_rows)], axis=0)
        yc = jnp.concatenate([
            _ctx_mha(p),
            _lat_neighbourhood(p, flat_cache(cache_c_k), flat_cache(cache_c_v), _na_bias(na_rpb[l]))], axis=0)
        yb, st = _rwkv_mixer(p, lp, state_rwkv[:, l])
        new_cache = lambda col0, heads: p[:N_CTX_TOK, col0:col0 + heads * HEAD_DIM].reshape(BATCH, SEQ, heads, HEAD_DIM)
        list_ak.append(new_cache(COL0_KA, A_KV))
        list_av.append(new_cache(COL0_VA, A_KV))
        list_ck.append(new_cache(COL0_KC, C_HEADS))
        list_cv.append(new_cache(COL0_VC, C_HEADS))
        list_st.append(st)
        x = _merge(x, ya, yb, yc, p, mod,
                   a_out[l].astype(bf16), rw_out[l].astype(bf16), na_out[l].astype(bf16), w_o[l].astype(bf16))

        q, h2 = _norm_proj(x, ln2_g[l][None, :], mod, pe_q[l].astype(bf16), 3, 4,
                           col_tile=PEER_HEADS * PEER_KEY_DIM)
        s1m, s2m, thr, c0 = _peer_route(q, pe_subkeys[l])
        x = _peer_dense(h2, x, mod, s1m, s2m, thr, c0, pe_u[l].astype(bf16), pe_v[l].T.astype(bf16))

    y = _final_norm(x, lnf_g[None, :])
    y_prompt = y[:N_CTX_TOK].reshape(BATCH, SEQ, D_MODEL)
    y_sample = y[N_CTX_TOK:].reshape(DEC_BATCH, DEC_SEQ, D_MODEL)
    return (y_prompt, y_sample, jnp.stack(list_ak, axis=1), jnp.stack(list_av, axis=1),
            jnp.stack(list_ck, axis=1), jnp.stack(list_cv, axis=1), jnp.stack(list_st, axis=1))
```

```python
import functools

import jax
import jax.numpy as jnp
from jax import lax
from jax.experimental import pallas as pl
from jax.experimental.pallas import tpu as pltpu

D_MODEL = 1024
BATCH = 16
SEQ = 256
DEPTH = 4
DEC_BATCH = 2
DEC_SEQ = 2048
PAST_LEN = 512
GRID_W = 64
HEAD_DIM = 64
A_HEADS = 8
A_KV = 2
A_GROUP = A_HEADS // A_KV
WINDOW = 128
Q_BLOCK = 128
B_HEADS = 8
B_WIDTH = B_HEADS * HEAD_DIM
LORA_W = 64
LORA_A = 64
LORA_G = 128
GN_EPS = 64e-5
C_HEADS = 8
NA_ROWS = 8
NA_COLS = 16
N_BRANCH = 3
A_COLS = (A_HEADS + 2 * A_KV) * HEAD_DIM
B_COLS = 3 * B_WIDTH + 2 * LORA_W + 2 * LORA_A + LORA_G
C_COLS = 3 * C_HEADS * HEAD_DIM
IN_COLS = A_COLS + B_COLS + C_COLS + N_BRANCH * D_MODEL
PEER_HEADS = 8
N_KEYS = 128
N_EXPERTS = N_KEYS * N_KEYS
PEER_KEY_DIM = 256
PEER_TOPK = 16
PEER_CHUNK = 128
ROPE_BASE = 10000.0
RMS_EPS = 1e-6
N_MOD = 6

N_CTX_TOK = BATCH * SEQ
N_LAT_TOK = DEC_BATCH * DEC_SEQ
N_TOK = N_CTX_TOK + N_LAT_TOK
N_SEG = 1 + DEC_BATCH

VMEM_LIMIT_BYTES = 48 * 1024 * 1024

ROW_TILE = 512
assert N_CTX_TOK % ROW_TILE == 0 and DEC_SEQ % ROW_TILE == 0


def _segment_of_row_block(i):
    n_ctx_blocks = N_CTX_TOK // ROW_TILE
    blocks_per_lat = DEC_SEQ // ROW_TILE
    return jnp.where(i < n_ctx_blocks, 0, 1 + (i - n_ctx_blocks) // blocks_per_lat)


def _norm_proj_kernel(x_ref, g_ref, mod_ref, w_ref, o_ref, h_ref, *, shift_idx, scale_idx):
    @pl.when(pl.program_id(1) == 0)
    def _():
        x = x_ref[...]
        y = x * lax.rsqrt(jnp.mean(x * x, axis=-1, keepdims=True) + RMS_EPS) * g_ref[...]
        h = y * (1.0 + mod_ref[0, scale_idx:scale_idx + 1, :]) + mod_ref[0, shift_idx:shift_idx + 1, :]
        h_ref[...] = h.astype(jnp.bfloat16)

    o_ref[...] = jnp.dot(h_ref[...], w_ref[...], preferred_element_type=jnp.float32)


def _norm_proj(x, g, mod, w_bf16, shift_idx, scale_idx, col_tile):
    n_rows, d = x.shape
    n_cols = w_bf16.shape[1]
    assert n_cols % col_tile == 0
    return pl.pallas_call(
        functools.partial(_norm_proj_kernel, shift_idx=shift_idx, scale_idx=scale_idx),
        grid=(n_rows // ROW_TILE, n_cols // col_tile),
        in_specs=[
            pl.BlockSpec((ROW_TILE, d), lambda i, j: (i, 0)),
            pl.BlockSpec((1, d), lambda i, j: (0, 0)),
            pl.BlockSpec((1, N_MOD, d), lambda i, j: (_segment_of_row_block(i), 0, 0)),
            pl.BlockSpec((d, col_tile), lambda i, j: (0, j)),
        ],
        out_specs=[pl.BlockSpec((ROW_TILE, col_tile), lambda i, j: (i, j)),
                   pl.BlockSpec((ROW_TILE, d), lambda i, j: (i, 0))],
        out_shape=[jax.ShapeDtypeStruct((n_rows, n_cols), jnp.float32),
                   jax.ShapeDtypeStruct((n_rows, d), jnp.bfloat16)],
        compiler_params=pltpu.CompilerParams(
            dimension_semantics=("arbitrary", "arbitrary"), vmem_limit_bytes=VMEM_LIMIT_BYTES),
        name="norm_proj",
    )(x, g, mod, w_bf16)


LANE = 128
ROUTE_TOK = 512
PEER_TOK = 512
PEER_EXP = 1024
NEG_INF = float("-inf")
LOG2_E = 1.4426950408889634


def _take_top(vals, k):
    rows = lax.broadcasted_iota(jnp.int32, vals.shape, 0)
    taken = []
    for _ in range(k):
        m = jnp.max(vals, axis=0, keepdims=True)
        first = jnp.min(jnp.where(vals == m, rows, vals.shape[0]), axis=0, keepdims=True)
        vals = jnp.where(rows == first, NEG_INF, vals)
        taken.append(m)
    return jnp.concatenate(taken, axis=0), vals


def _take_top_distinct(vals, k):
    taken = []
    for _ in range(k):
        m = jnp.max(vals, axis=0, keepdims=True)
        vals = jnp.where(vals == m, NEG_INF, vals)
        taken.append(m)
    removed = jnp.sum((vals == NEG_INF).astype(jnp.int32), axis=0, keepdims=True)
    return jnp.concatenate(taken, axis=0), vals, removed


def _peer_route_kernel(q_ref, sk_ref, s1m_ref, s2m_ref, thr_ref, c0_ref):
    def route(tok, take_top):
        masked, top_vals, removed = [], [], []
        for z in range(2):
            qz = q_ref[tok, z * LANE:(z + 1) * LANE]
            s = lax.dot_general(sk_ref[0, z], qz, (((1,), (1,)), ((), ())),
                                preferred_element_type=jnp.float32)
            v, rest, n = take_top(s, PEER_TOPK)
            masked.append(jnp.where(rest == NEG_INF, s, NEG_INF))
            top_vals.append(v)
            removed.append(n)
        half = PEER_TOPK // 2
        v1, v2 = top_vals
        cand = jnp.concatenate(
            [v1[0:1] + v2] + [v1[i:i + 1] + v2[:half] for i in range(1, half)] + [v1[half:] + v2[0:1]], axis=0)
        sv, _, n = take_top(cand, PEER_TOPK)
        removed.append(n)
        z_sum = jnp.sum(jnp.exp(sv - sv[0:1]), axis=0, keepdims=True)
        s1m_ref[0, :, tok] = masked[0]
        s2m_ref[0, :, tok] = masked[1]
        thr_ref[0, :, tok] = sv[PEER_TOPK - 1:PEER_TOPK]
        c0_ref[0, :, tok] = -sv[0:1] - jnp.log(z_sum)
        return jnp.max(functools.reduce(jnp.maximum, removed))

    def exact_take_top(vals, k):
        v, rest = _take_top(vals, k)
        return v, rest, jnp.full((1, vals.shape[1]), k, jnp.int32)

    toks = [slice(i * LANE, (i + 1) * LANE) for i in range(ROUTE_TOK // LANE)]
    worst = [route(tok, _take_top_distinct) for tok in toks]
    for tok, n in zip(toks, worst):
        @pl.when(n > PEER_TOPK)
        def _():
            route(tok, exact_take_top)


def _peer_route(q, subkeys):
    n = q.shape[0]
    big = pl.BlockSpec((1, N_KEYS, ROUTE_TOK), lambda i, h: (h, 0, i))
    small = pl.BlockSpec((1, 1, ROUTE_TOK), lambda i, h: (h, 0, i))
    return pl.pallas_call(
        _peer_route_kernel,
        grid=(n // ROUTE_TOK, PEER_HEADS),
        in_specs=[
            pl.BlockSpec((ROUTE_TOK, PEER_KEY_DIM), lambda i, h: (i, h)),
            pl.BlockSpec((1, 2, N_KEYS, PEER_KEY_DIM // 2), lambda i, h: (h, 0, 0, 0)),
        ],
        out_specs=[big, big, small, small],
        out_shape=[jax.ShapeDtypeStruct((PEER_HEADS, N_KEYS, n), jnp.float32)] * 2
        + [jax.ShapeDtypeStruct((PEER_HEADS, 1, n), jnp.float32)] * 2,
        compiler_params=pltpu.CompilerParams(
            dimension_semantics=("arbitrary", "arbitrary"), vmem_limit_bytes=VMEM_LIMIT_BYTES),
        name="peer_route",
    )(q, subkeys)


def _gelu_tanh(x):
    return 0.5 * x * (1.0 + jnp.tanh(0.7978845608028654 * (x + 0.044715 * (x * x * x))))


def _peer_dense_kernel(h_ref, x_ref, mod_ref, s1m_ref, s2m_ref, thr_ref, c0_ref, u_ref, vt_ref, o_ref,
                       ht_scr, s1l_scr, s2l_scr, coef_scr, acc_scr, *, gate_idx):
    e = pl.program_id(1)

    @pl.when(e == 0)
    def _():
        ht_scr[...] = h_ref[...].astype(jnp.float32).T.astype(jnp.bfloat16)
        s1l_scr[...] = s1m_ref[...] * LOG2_E
        s2l_scr[...] = (s2m_ref[...] + c0_ref[...]) * LOG2_E
        acc_scr[...] = jnp.zeros_like(acc_scr)

    act = jnp.dot(u_ref[...], ht_scr[...], preferred_element_type=jnp.float32)
    a_per_step = PEER_EXP // N_KEYS
    a_rows = pl.ds(pl.multiple_of(e * a_per_step, a_per_step), a_per_step)
    for t in range(PEER_TOK // LANE):
        tok = slice(t * LANE, (t + 1) * LANE)
        s1_all = [s1m_ref[hh, a_rows, tok] for hh in range(PEER_HEADS)]
        s1l_all = [s1l_scr[hh, a_rows, tok] for hh in range(PEER_HEADS)]
        for a in range(a_per_step):
            rows = slice(a * N_KEYS, (a + 1) * N_KEYS)
            w = jnp.zeros((N_KEYS, LANE), jnp.float32)
            for hh in range(PEER_HEADS):
                s1 = s1_all[hh][a:a + 1]
                picked = (s1 + s2m_ref[hh, :, tok]) >= thr_ref[hh, :, tok]
                w = w + jnp.where(picked, jnp.exp2(s1l_all[hh][a:a + 1] + s2l_scr[hh, :, tok]), 0.0)
            coef_scr[rows, tok] = (w * _gelu_tanh(act[rows, tok])).astype(jnp.bfloat16)
    acc_scr[...] += jnp.dot(vt_ref[...], coef_scr[...], preferred_element_type=jnp.float32)

    @pl.when(e == pl.num_programs(1) - 1)
    def _():
        o_ref[...] = x_ref[...] + mod_ref[0, gate_idx:gate_idx + 1, :] * acc_scr[...].T


def _peer_dense(h, x, mod, s1m, s2m, thr, c0, u_bf16, vt_bf16):
    n, d = x.shape
    assert PEER_TOK == ROW_TILE
    tok_row = lambda i, e: (i, 0)
    big = pl.BlockSpec((PEER_HEADS, N_KEYS, PEER_TOK), lambda i, e: (0, 0, i))
    small = pl.BlockSpec((PEER_HEADS, 1, PEER_TOK), lambda i, e: (0, 0, i))
    return pl.pallas_call(
        functools.partial(_peer_dense_kernel, gate_idx=5),
        grid=(n // PEER_TOK, N_EXPERTS // PEER_EXP),
        in_specs=[
            pl.BlockSpec((PEER_TOK, d), tok_row),
            pl.BlockSpec((PEER_TOK, d), tok_row),
            pl.BlockSpec((1, N_MOD, d), lambda i, e: (_segment_of_row_block(i), 0, 0)),
            big, big, small, small,
            pl.BlockSpec((PEER_EXP, d), lambda i, e: (e, 0)),
            pl.BlockSpec((d, PEER_EXP), lambda i, e: (0, e)),
        ],
        out_specs=pl.BlockSpec((PEER_TOK, d), tok_row),
        out_shape=jax.ShapeDtypeStruct((n, d), jnp.float32),
        scratch_shapes=[
            pltpu.VMEM((d, PEER_TOK), jnp.bfloat16),
            pltpu.VMEM((PEER_HEADS, N_KEYS, PEER_TOK), jnp.float32),
            pltpu.VMEM((PEER_HEADS, N_KEYS, PEER_TOK), jnp.float32),
            pltpu.VMEM((PEER_EXP, PEER_TOK), jnp.bfloat16),
            pltpu.VMEM((d, PEER_TOK), jnp.float32),
        ],
        compiler_params=pltpu.CompilerParams(
            dimension_semantics=("arbitrary", "arbitrary"), vmem_limit_bytes=VMEM_LIMIT_BYTES),
        name="peer_dense",
    )(h, x, mod, s1m, s2m, thr, c0, u_bf16, vt_bf16)


def _merge_kernel(x_ref, ya_ref, yb_ref, yc_ref, ga_ref, gb_ref, gc_ref, mod_ref,
                  wa_ref, wb_ref, wc_ref, wo_ref, o_ref, *, gate_idx):
    def branch(y_ref, gate_ref, w_ref):
        proj = jnp.dot(y_ref[...].astype(jnp.bfloat16), w_ref[...], preferred_element_type=jnp.float32)
        return jax.nn.sigmoid(gate_ref[...]) * proj

    merged = branch(ya_ref, ga_ref, wa_ref) + branch(yb_ref, gb_ref, wb_ref) + branch(yc_ref, gc_ref, wc_ref)
    out = jnp.dot(merged.astype(jnp.bfloat16), wo_ref[...], preferred_element_type=jnp.float32)
    o_ref[...] = x_ref[...] + mod_ref[0, gate_idx:gate_idx + 1, :] * out


def _merge(x, ya, yb, yc, p, mod, wa, wb, wc, wo):
    n_rows, d = x.shape
    gate_col0 = 0
    row = lambda i: (i, 0)
    const = lambda i: (0, 0)
    y_spec = pl.BlockSpec((ROW_TILE, ya.shape[1]), row)
    w_spec = pl.BlockSpec(wa.shape, const)
    return pl.pallas_call(
        functools.partial(_merge_kernel, gate_idx=2),
        grid=(n_rows // ROW_TILE,),
        in_specs=[
            pl.BlockSpec((ROW_TILE, d), row), y_spec, y_spec, y_spec,
            pl.BlockSpec((ROW_TILE, d), lambda i: (i, gate_col0)),
            pl.BlockSpec((ROW_TILE, d), lambda i: (i, gate_col0 + 1)),
            pl.BlockSpec((ROW_TILE, d), lambda i: (i, gate_col0 + 2)),
            pl.BlockSpec((1, N_MOD, d), lambda i: (_segment_of_row_block(i), 0, 0)),
            w_spec, w_spec, w_spec, pl.BlockSpec(wo.shape, const),
        ],
        out_specs=pl.BlockSpec((ROW_TILE, d), row),
        out_shape=jax.ShapeDtypeStruct((n_rows, d), jnp.float32),
        compiler_params=pltpu.CompilerParams(
            dimension_semantics=("arbitrary",), vmem_limit_bytes=VMEM_LIMIT_BYTES),
        name="merge_out_proj",
    )(x, ya, yb, yc, p, p, p, mod, wa, wb, wc, wo)


HEAD_PAIRS = B_HEADS // 2
PREP_TOK = 256
SCAN_CHUNK = 128
assert PREP_TOK == SEQ and DEC_SEQ % PREP_TOK == 0 and 2 * HEAD_DIM == LANE
_HI = lax.Precision.HIGHEST


def _head_ones(n):
    r = lax.broadcasted_iota(jnp.int32, (n, n), 0) // HEAD_DIM
    c = lax.broadcasted_iota(jnp.int32, (n, n), 1) // HEAD_DIM
    return (r == c).astype(jnp.float32)


def _softplus(y):
    return jnp.maximum(y, 0.0) + jnp.log(1.0 + jnp.exp(-jnp.abs(y)))


def _rwkv_prep_kernel(pb_ref, prev_ref, next_ref, mu_ref, w0_ref, w2_ref, a0_ref, a2_ref, g2_ref,
                      kks_ref, ka_ref, rk_ref, ones_ref,
                      r_o, kk_o, v_o, w0_o, w1_o, k0_o, k1_o, nb0_o, nb1_o, g_o, bonus_o):
    i = pl.program_id(0)
    n_ctx = N_CTX_TOK // PREP_TOK
    per_lat = DEC_SEQ // PREP_TOK
    j = (i - n_ctx) % per_lat
    starts_seq = jnp.logical_or(i < n_ctx, j == 0)
    ends_seq = jnp.logical_or(i < n_ctx, j == per_lat - 1)

    pb = pb_ref[...]
    rows = lax.broadcasted_iota(jnp.int32, (PREP_TOK, 1), 0)
    before = jnp.where(starts_seq, 0.0, prev_ref[7:8, :])
    after = jnp.where(ends_seq, 0.0, next_ref[0:1, :])
    prev = jnp.where(rows == 0, before, pltpu.roll(pb, 1, 0))
    nxt = jnp.where(rows == PREP_TOK - 1, after, pltpu.roll(pb, PREP_TOK - 1, 0))
    xb = pb + mu_ref[...] * (0.5 * (prev + nxt) - pb)

    w = B_WIDTH
    r, k, v = xb[:, 0:w], xb[:, w:2 * w], xb[:, 2 * w:3 * w]
    wd = xb[:, 3 * w:3 * w + 2 * LORA_W]
    ad = xb[:, 3 * w + 2 * LORA_W:3 * w + 2 * LORA_W + 2 * LORA_A]
    gd = xb[:, 3 * w + 2 * LORA_W + 2 * LORA_A:]
    ones = ones_ref[...]

    def head_sum(x):
        return jnp.dot(x, ones, preferred_element_type=jnp.float32, precision=_HI)

    lw = jnp.dot(jnp.tanh(wd), w2_ref[...], preferred_element_type=jnp.float32, precision=_HI)
    la = jnp.dot(ad, a2_ref[...], preferred_element_type=jnp.float32, precision=_HI)
    kk = k * kks_ref[...]
    kk = kk * lax.rsqrt(head_sum(kk * kk) + 1e-12)
    kd_sum = jnp.zeros_like(k)
    for z, (w_o, k_o, nb_o) in enumerate(((w0_o, k0_o, nb0_o), (w1_o, k1_o, nb1_o))):
        wlog = -_softplus(-(w0_ref[z:z + 1, :] + lw[:, z * w:(z + 1) * w])) - 0.5
        a = jax.nn.sigmoid(a0_ref[z:z + 1, :] + la[:, z * w:(z + 1) * w])
        kd = k * (1.0 + (a - 1.0) * ka_ref[...])
        kd_sum = kd_sum + kd
        w_o[...] = jnp.exp(-jnp.exp(wlog))
        k_o[...] = kd
        nb_o[...] = -(kk * a)
    r_o[...] = r
    kk_o[...] = kk
    v_o[...] = v
    g_o[...] = jnp.dot(jax.nn.sigmoid(gd), g2_ref[...], preferred_element_type=jnp.float32, precision=_HI)
    bonus_o[...] = head_sum(r * kd_sum * rk_ref[...]) * v


def _rwkv_prep(p, mu, w0, w2_pad, a0, a2_pad, g2, kk_scale, ka, rk):
    n = p.shape[0]
    col_blk = (N_BRANCH * D_MODEL + A_COLS) // B_COLS
    assert col_blk * B_COLS == N_BRANCH * D_MODEL + A_COLS
    halo = 8
    per = PREP_TOK // halo
    full = lambda a: pl.BlockSpec(a.shape, lambda i: (0,) * a.ndim)
    ones = _head_ones(B_WIDTH)
    consts = (mu, w0, w2_pad, a0, a2_pad, g2, kk_scale, ka, rk, ones)
    out_spec = pl.BlockSpec((PREP_TOK, B_WIDTH), lambda i: (i, 0))
    return pl.pallas_call(
        _rwkv_prep_kernel,
        grid=(n // PREP_TOK,),
        in_specs=[
            pl.BlockSpec((PREP_TOK, B_COLS), lambda i: (i, col_blk)),
            pl.BlockSpec((halo, B_COLS), lambda i: (jnp.maximum(i * per - 1, 0), col_blk)),
            pl.BlockSpec((halo, B_COLS), lambda i: (jnp.minimum((i + 1) * per, n // halo - 1), col_blk)),
        ] + [full(a) for a in consts],
        out_specs=[out_spec] * 11,
        out_shape=[jax.ShapeDtypeStruct((n, B_WIDTH), jnp.float32)] * 11,
        compiler_params=pltpu.CompilerParams(
            dimension_semantics=("arbitrary",), vmem_limit_bytes=VMEM_LIMIT_BYTES),
        name="rwkv_prep",
    )(p, p, p, *consts)


SCAN_LANES = 2 * LANE


def _head_sums(head_ones, *terms):
    sums = []
    for x, passes in terms:
        hi = x.astype(jnp.bfloat16)
        out = jnp.dot(hi, head_ones, preferred_element_type=jnp.float32)
        if passes == 2:
            lo = (x - hi.astype(jnp.float32)).astype(jnp.bfloat16)
            out = out + jnp.dot(lo, head_ones, preferred_element_type=jnp.float32)
        sums.append(out)
    return sums


def _rwkv_scan_kernel(rf, kkf, vf, wf, kf, nbf, rb, kkb, vb, wb, kb, nbb, s0_ref, ones_ref, eye_ref,
                      yf_ref, yb_ref, sfin_ref, s_scr, *, n_seq):
    c = pl.program_id(1)

    @pl.when(c == 0)
    def _():
        s_scr[...] = s0_ref[...]

    head_ones = ones_ref[...]
    eye = eye_ref[...]
    fwd_refs = (rf, kkf, vf, wf, kf, nbf)
    bwd_refs = (rb, kkb, vb, wb, kb, nbb)
    R, KK, V, W, K, NB = range(6)

    def step(t, carry):
        t_f = pl.ds(t, 1)
        t_b = pl.ds(SCAN_CHUNK - 1 - t, 1)
        rows_f = [[a[g, t_f, :] for a in fwd_refs] for g in range(n_seq)]
        rows_b = [[a[g, t_b, :] for a in bwd_refs] for g in range(n_seq)]

        def both(which, chain):
            g, p = chain
            lanes = slice(p * LANE, (p + 1) * LANE)
            return jnp.concatenate([rows_f[g][which][:, lanes], rows_b[g][which][:, lanes]], axis=1)

        chains = [(g, p) for g in range(n_seq) for p in range(HEAD_PAIRS)]
        rows = {ch: slice(i * HEAD_DIM, (i + 1) * HEAD_DIM) for i, ch in enumerate(chains)}
        s_old = {ch: s_scr[ch[0], ch[1]] for ch in chains}
        s_kk, v_col = _head_sums(
            head_ones,
            (jnp.concatenate([s_old[ch] * both(KK, ch) for ch in chains], axis=0), 1),
            (jnp.concatenate([eye * both(V, ch) for ch in chains], axis=0), 1))
        s_new = {}
        for ch in chains:
            s = s_old[ch] * both(W, ch) + s_kk[rows[ch]] * both(NB, ch) + v_col[rows[ch]] * both(K, ch)
            s_scr[ch[0], ch[1]] = s
            s_new[ch] = s
        (y_all,) = _head_sums(
            head_ones, (jnp.concatenate([s_new[ch] * both(R, ch) for ch in chains], axis=0), 1))
        for g in range(n_seq):
            y_rows = [jnp.sum(y_all[rows[(g, p)]] * eye, axis=0, keepdims=True) for p in range(HEAD_PAIRS)]
            yf_ref[g, t_f, :] = jnp.concatenate([y[:, :LANE] for y in y_rows], axis=1)
            yb_ref[g, t_b, :] = jnp.concatenate([y[:, LANE:] for y in y_rows], axis=1)
        return carry

    lax.fori_loop(0, SCAN_CHUNK, step, 0, unroll=16)

    @pl.when(c == pl.num_programs(1) - 1)
    def _():
        sfin_ref[...] = s_scr[...]


def _rwkv_scan(flat, s0, t_len, first_seq, seq_group):
    r, kk, v, w0, w1, k0, k1, nb0, nb1 = (a.reshape(-1, t_len, B_WIDTH) for a in flat)
    n_seq = s0.shape[0]
    n_chunks = t_len // SCAN_CHUNK
    assert n_seq % seq_group == 0 and first_seq % seq_group == 0 and n_chunks * SCAN_CHUNK == t_len
    first_blk = first_seq // seq_group
    blk = (seq_group, SCAN_CHUNK, B_WIDTH)
    fwd_in = pl.BlockSpec(blk, lambda i, c: (first_blk + i, c, 0))
    bwd_in = pl.BlockSpec(blk, lambda i, c: (first_blk + i, n_chunks - 1 - c, 0))
    fwd = pl.BlockSpec(blk, lambda i, c: (i, c, 0))
    bwd = pl.BlockSpec(blk, lambda i, c: (i, n_chunks - 1 - c, 0))
    state_blk = (seq_group, HEAD_PAIRS, HEAD_DIM, SCAN_LANES)
    state = pl.BlockSpec(state_blk, lambda i, c: (i, 0, 0, 0))
    head_ones = _head_ones(SCAN_LANES).astype(jnp.bfloat16)
    eye = (lax.broadcasted_iota(jnp.int32, (HEAD_DIM, SCAN_LANES), 0)
           == lax.broadcasted_iota(jnp.int32, (HEAD_DIM, SCAN_LANES), 1) % HEAD_DIM).astype(jnp.float32)
    const = lambda a: pl.BlockSpec(a.shape, lambda i, c: (0, 0))
    y_shape = jax.ShapeDtypeStruct((n_seq, t_len, B_WIDTH), jnp.float32)
    return pl.pallas_call(
        functools.partial(_rwkv_scan_kernel, n_seq=seq_group),
        grid=(n_seq // seq_group, n_chunks),
        in_specs=[fwd_in] * 6 + [bwd_in] * 6 + [state, const(head_ones), const(eye)],
        out_specs=[fwd, bwd, state],
        out_shape=[y_shape, y_shape, jax.ShapeDtypeStruct(s0.shape, jnp.float32)],
        scratch_shapes=[pltpu.VMEM(state_blk, jnp.float32)],
        compiler_params=pltpu.CompilerParams(
            dimension_semantics=("arbitrary", "arbitrary"), vmem_limit_bytes=VMEM_LIMIT_BYTES),
        name="rwkv_scan",
    )(r, kk, v, w0, k0, nb0, r, kk, v, w1, k1, nb1, s0, head_ones, eye)


def _rwkv_post_kernel(yfc_ref, ybc_ref, yfl_ref, ybl_ref, g_ref, bonus_ref, lng_ref, lnb_ref, ones_ref, o_ref):
    ones = ones_ref[...]
    is_ctx = pl.program_id(0) < N_CTX_TOK // PREP_TOK
    y = jnp.where(is_ctx, yfc_ref[...] + ybc_ref[...], yfl_ref[...] + ybl_ref[...])
    mean = jnp.dot(y, ones, preferred_element_type=jnp.float32, precision=_HI) * (1.0 / HEAD_DIM)
    d = y - mean
    var = jnp.dot(d * d, ones, preferred_element_type=jnp.float32, precision=_HI) * (1.0 / HEAD_DIM)
    y = d * lax.rsqrt(var + GN_EPS) * lng_ref[...] + lnb_ref[...]
    o_ref[...] = (y + bonus_ref[...]) * g_ref[...]


def _rwkv_post(yf_ctx, yb_ctx, yf_lat, yb_lat, g, bonus, lnx_g, lnx_b):
    n = g.shape[0]
    n_ctx = N_CTX_TOK // PREP_TOK
    n_lat = N_LAT_TOK // PREP_TOK
    row = pl.BlockSpec((PREP_TOK, B_WIDTH), lambda i: (i, 0))
    ctx_row = pl.BlockSpec((PREP_TOK, B_WIDTH), lambda i: (jnp.minimum(i, n_ctx - 1), 0))
    lat_row = pl.BlockSpec((PREP_TOK, B_WIDTH), lambda i: (jnp.clip(i - n_ctx, 0, n_lat - 1), 0))
    vec = pl.BlockSpec((1, B_WIDTH), lambda i: (0, 0))
    ones = _head_ones(B_WIDTH)
    return pl.pallas_call(
        _rwkv_post_kernel,
        grid=(n // PREP_TOK,),
        in_specs=[ctx_row, ctx_row, lat_row, lat_row, row, row, vec, vec, pl.BlockSpec(ones.shape, lambda i: (0, 0))],
        out_specs=row,
        out_shape=jax.ShapeDtypeStruct((n, B_WIDTH), jnp.float32),
        compiler_params=pltpu.CompilerParams(
            dimension_semantics=("arbitrary",), vmem_limit_bytes=VMEM_LIMIT_BYTES),
        name="rwkv_post",
    )(yf_ctx, yb_ctx, yf_lat, yb_lat, g, bonus, lnx_g, lnx_b, ones)


def _state_to_pairs(s):
    b = s.shape[0]
    s = s.reshape(b, 2, HEAD_PAIRS, 2, HEAD_DIM, HEAD_DIM)
    return jnp.transpose(s, (0, 2, 4, 1, 3, 5)).reshape(b, HEAD_PAIRS, HEAD_DIM, SCAN_LANES)


def _state_from_pairs(s):
    b = s.shape[0]
    s = s.reshape(b, HEAD_PAIRS, HEAD_DIM, 2, 2, HEAD_DIM)
    return jnp.transpose(s, (0, 3, 1, 4, 2, 5)).reshape(b, 2, B_HEADS, HEAD_DIM, HEAD_DIM)


def _rwkv_mixer(p, lp, state_lat):
    zeros = jnp.zeros((LORA_W, B_WIDTH), jnp.float32)
    w2_pad = jnp.concatenate([jnp.concatenate([lp["rw_w2"][0], zeros], axis=1),
                              jnp.concatenate([zeros, lp["rw_w2"][1]], axis=1)], axis=0)
    a2_pad = jnp.concatenate([jnp.concatenate([lp["rw_a2"][0], zeros], axis=1),
                              jnp.concatenate([zeros, lp["rw_a2"][1]], axis=1)], axis=0)
    row = lambda a: a.reshape(1, -1)
    prep = _rwkv_prep(p, row(lp["rw_mu"]), lp["rw_w0"], w2_pad, lp["rw_a0"], a2_pad, lp["rw_g2"],
                      row(lp["rw_kk"]), row(lp["rw_ka"]), row(lp["rw_rk"]))
    scan_in, (g, bonus) = prep[:9], prep[9:]
    s0_ctx = jnp.zeros((BATCH, HEAD_PAIRS, HEAD_DIM, SCAN_LANES), jnp.float32)
    yf_c, yb_c, s_ctx = _rwkv_scan(scan_in, s0_ctx, t_len=SEQ, first_seq=0, seq_group=4)
    yf_l, yb_l, _ = _rwkv_scan(scan_in, _state_to_pairs(state_lat), t_len=DEC_SEQ,
                               first_seq=N_CTX_TOK // DEC_SEQ, seq_group=DEC_BATCH)
    flat = lambda a: a.reshape(-1, B_WIDTH)
    y = _rwkv_post(flat(yf_c), flat(yb_c), flat(yf_l), flat(yb_l), g, bonus,
                   row(lp["rw_lnx_g"]), row(lp["rw_lnx_b"]))
    return y, _state_from_pairs(s_ctx)


ATT_SCALE = HEAD_DIM ** -0.5
COL0_QA = N_BRANCH * D_MODEL
COL0_KA = COL0_QA + A_HEADS * HEAD_DIM
COL0_VA = COL0_KA + A_KV * HEAD_DIM
COL0_QC = COL0_QA + A_COLS + B_COLS
COL0_KC = COL0_QC + C_HEADS * HEAD_DIM
COL0_VC = COL0_KC + C_HEADS * HEAD_DIM
A_GROUP_COLS = A_GROUP * HEAD_DIM
assert A_KV * HEAD_DIM == LANE and all(c % LANE == 0 for c in (COL0_QA, COL0_KA, COL0_VA, COL0_QC, COL0_KC, COL0_VC))
assert COL0_QA % A_GROUP_COLS == 0
GRID_ROWS = DEC_SEQ // GRID_W
NA_WIN_ROWS = min(NA_ROWS, GRID_ROWS)


def _attend(q, parts, sink=None):
    scores = []
    for k, _, bias in parts:
        s = lax.dot_general(q, k, (((1,), (1,)), ((), ())), preferred_element_type=jnp.float32)
        scores.append(s if bias is None else s + bias)
    m = functools.reduce(jnp.maximum, [jnp.max(s, axis=-1, keepdims=True) for s in scores])
    den = 0.0
    if sink is not None:
        m = jnp.maximum(m, sink)
        den = jnp.exp(sink - m)
    out = 0.0
    for s, (_, v, _) in zip(scores, parts):
        prob = jnp.exp(s - m)
        den = den + jnp.sum(prob, axis=-1, keepdims=True)
        out = out + jnp.dot(prob.astype(jnp.bfloat16), v, preferred_element_type=jnp.float32)
    return out / den


def _scaled_bf16(q):
    return (q * ATT_SCALE).astype(jnp.bfloat16)


def _kv_half(x, kv):
    return jnp.where(kv == 0, x[:, :HEAD_DIM], x[:, HEAD_DIM:])


def _ctx_gqa_kernel(q_ref, k_ref, v_ref, sink_ref, o_ref):
    kv = pl.program_id(1)
    k = _kv_half(k_ref[...], kv).astype(jnp.bfloat16)
    v = _kv_half(v_ref[...], kv).astype(jnp.bfloat16)
    outs = []
    for g in range(A_GROUP):
        q = _scaled_bf16(q_ref[:, g * HEAD_DIM:(g + 1) * HEAD_DIM])
        sink = sink_ref[pl.ds(kv * A_GROUP + g, 1), :][:, 0:1]
        outs.append(_attend(q, [(k, v, None)], sink))
    o_ref[...] = jnp.concatenate(outs, axis=1)


def _ctx_gqa(p, sink_rows):
    q0, k0, v0 = COL0_QA // A_GROUP_COLS, COL0_KA // LANE, COL0_VA // LANE
    return pl.pallas_call(
        _ctx_gqa_kernel,
        grid=(BATCH, A_KV),
        in_specs=[
            pl.BlockSpec((SEQ, A_GROUP_COLS), lambda b, kv: (b, q0 + kv)),
            pl.BlockSpec((SEQ, LANE), lambda b, kv: (b, k0)),
            pl.BlockSpec((SEQ, LANE), lambda b, kv: (b, v0)),
            pl.BlockSpec(sink_rows.shape, lambda b, kv: (0, 0)),
        ],
        out_specs=pl.BlockSpec((SEQ, A_GROUP_COLS), lambda b, kv: (b, kv)),
        out_shape=jax.ShapeDtypeStruct((N_CTX_TOK, A_HEADS * HEAD_DIM), jnp.float32),
        compiler_params=pltpu.CompilerParams(
            dimension_semantics=("arbitrary", "arbitrary"), vmem_limit_bytes=VMEM_LIMIT_BYTES),
        name="ctx_gqa",
    )(p, p, p, sink_rows)


def _ctx_mha_kernel(q_ref, k_ref, v_ref, o_ref):
    outs = []
    for j in range(2):
        lanes = slice(j * HEAD_DIM, (j + 1) * HEAD_DIM)
        outs.append(_attend(_scaled_bf16(q_ref[:, lanes]),
                            [(k_ref[:, lanes].astype(jnp.bfloat16), v_ref[:, lanes].astype(jnp.bfloat16), None)]))
    o_ref[...] = jnp.concatenate(outs, axis=1)


def _ctx_mha(p):
    q0, k0, v0 = COL0_QC // LANE, COL0_KC // LANE, COL0_VC // LANE
    return pl.pallas_call(
        _ctx_mha_kernel,
        grid=(BATCH, C_HEADS // 2),
        in_specs=[
            pl.BlockSpec((SEQ, LANE), lambda b, hp: (b, q0 + hp)),
            pl.BlockSpec((SEQ, LANE), lambda b, hp: (b, k0 + hp)),
            pl.BlockSpec((SEQ, LANE), lambda b, hp: (b, v0 + hp)),
        ],
        out_specs=pl.BlockSpec((SEQ, LANE), lambda b, hp: (b, hp)),
        out_shape=jax.ShapeDtypeStruct((N_CTX_TOK, C_HEADS * HEAD_DIM), jnp.float32),
        compiler_params=pltpu.CompilerParams(
            dimension_semantics=("arbitrary", "arbitrary"), vmem_limit_bytes=VMEM_LIMIT_BYTES),
        name="ctx_mha",
    )(p, p, p)


def _rope(x, cos, sin):
    reps = x.shape[1] // LANE
    if reps > 1:
        cos = jnp.concatenate([cos] * reps, axis=1)
        sin = jnp.concatenate([sin] * reps, axis=1)
    lane = lax.broadcasted_iota(jnp.int32, x.shape, 1)
    first = (lane % (HEAD_DIM // 2)) < (HEAD_DIM // 4)
    quarter = HEAD_DIM // 4
    pieces = [x[:, i * LANE:(i + 1) * LANE] for i in range(reps)]
    up = jnp.concatenate([pltpu.roll(piece, LANE - quarter, 1) for piece in pieces], axis=1)
    down = jnp.concatenate([pltpu.roll(piece, quarter, 1) for piece in pieces], axis=1)
    return x * cos + jnp.where(first, -up, down) * sin


def _lat_window_kernel(q_ref, kp_ref, kc_ref, kn_ref, vp_ref, vc_ref, vn_ref, ck_ref, cv_ref,
                       cosq_ref, sinq_ref, cosp_ref, sinp_ref, cosn_ref, sinn_ref, sink_ref, o_ref):
    qb = pl.program_id(1)
    kv = pl.program_id(2)
    n_blocks = pl.num_programs(1)
    row = lax.broadcasted_iota(jnp.int32, (Q_BLOCK, Q_BLOCK), 0)
    col = lax.broadcasted_iota(jnp.int32, (Q_BLOCK, Q_BLOCK), 1)
    bias_prev = jnp.where(jnp.logical_and(col >= row, qb > 0), 0.0, NEG_INF)
    bias_next = jnp.where(jnp.logical_and(col <= row, qb < n_blocks - 1), 0.0, NEG_INF)
    assert WINDOW == Q_BLOCK

    def keys(k_ref, cos_ref, sin_ref):
        return _kv_half(_rope(k_ref[...], cos_ref[...], sin_ref[...]), kv).astype(jnp.bfloat16)

    def vals(v_ref):
        return _kv_half(v_ref[...], kv).astype(jnp.bfloat16)

    parts = [
        (keys(kp_ref, cosp_ref, sinp_ref), vals(vp_ref), bias_prev),
        (keys(kc_ref, cosq_ref, sinq_ref), vals(vc_ref), None),
        (keys(kn_ref, cosn_ref, sinn_ref), vals(vn_ref), bias_next),
        (_kv_half(ck_ref[0], kv).astype(jnp.bfloat16), _kv_half(cv_ref[0], kv).astype(jnp.bfloat16), None),
    ]
    q_all = _rope(q_ref[...], cosq_ref[...], sinq_ref[...])
    outs = []
    for g in range(A_GROUP):
        sink = sink_ref[pl.ds(kv * A_GROUP + g, 1), :][:, 0:1]
        outs.append(_attend(_scaled_bf16(q_all[:, g * HEAD_DIM:(g + 1) * HEAD_DIM]), parts, sink))
    o_ref[...] = jnp.concatenate(outs, axis=1)


def _lat_window(p, cache_k, cache_v, cos, sin, sink_rows):
    n_blocks = DEC_SEQ // Q_BLOCK
    base = N_CTX_TOK // Q_BLOCK
    q0, k0, v0 = COL0_QA // A_GROUP_COLS, COL0_KA // LANE, COL0_VA // LANE
    prev = lambda b, qb, kv: base + b * n_blocks + jnp.maximum(qb - 1, 0)
    here = lambda b, qb, kv: base + b * n_blocks + qb
    nxt = lambda b, qb, kv: base + b * n_blocks + jnp.minimum(qb + 1, n_blocks - 1)
    slab = lambda rows, c0: pl.BlockSpec((Q_BLOCK, LANE), lambda b, qb, kv: (rows(b, qb, kv), c0))
    table = lambda blk: pl.BlockSpec((Q_BLOCK, LANE), lambda b, qb, kv: (blk(b, qb, kv), 0))
    cache = pl.BlockSpec((1, PAST_LEN, LANE), lambda b, qb, kv: (b, 0, 0))
    return pl.pallas_call(
        _lat_window_kernel,
        grid=(DEC_BATCH, n_blocks, A_KV),
        in_specs=[
            pl.BlockSpec((Q_BLOCK, A_GROUP_COLS), lambda b, qb, kv: (here(b, qb, kv), q0 + kv)),
            slab(prev, k0), slab(here, k0), slab(nxt, k0),
            slab(prev, v0), slab(here, v0), slab(nxt, v0),
            cache, cache,
            table(lambda b, qb, kv: qb), table(lambda b, qb, kv: qb),
            table(lambda b, qb, kv: jnp.maximum(qb - 1, 0)), table(lambda b, qb, kv: jnp.maximum(qb - 1, 0)),
            table(lambda b, qb, kv: jnp.minimum(qb + 1, n_blocks - 1)),
            table(lambda b, qb, kv: jnp.minimum(qb + 1, n_blocks - 1)),
            pl.BlockSpec(sink_rows.shape, lambda b, qb, kv: (0, 0)),
        ],
        out_specs=pl.BlockSpec((Q_BLOCK, A_GROUP_COLS), lambda b, qb, kv: (b * n_blocks + qb, kv)),
        out_shape=jax.ShapeDtypeStruct((N_LAT_TOK, A_HEADS * HEAD_DIM), jnp.float32),
        compiler_params=pltpu.CompilerParams(
            dimension_semantics=("arbitrary", "arbitrary", "arbitrary"), vmem_limit_bytes=VMEM_LIMIT_BYTES),
        name="lat_window",
    )(p, p, p, p, p, p, p, cache_k, cache_v, cos, sin, cos, sin, cos, sin, sink_rows)


N_DC = 2 * NA_COLS - 1
N_DR = 2 * NA_ROWS - 1


def _na_bias_kernel(rpb_ref, o_ref):
    h = pl.program_id(0)
    shape = (GRID_W, 2 * GRID_W)
    qc = lax.broadcasted_iota(jnp.int32, shape, 0)
    kc = lax.broadcasted_iota(jnp.int32, shape, 1) % GRID_W
    second = lax.broadcasted_iota(jnp.int32, shape, 1) >= GRID_W
    start = jnp.clip(qc - NA_COLS // 2, 0, GRID_W - NA_COLS)
    inside = jnp.logical_and(kc >= start, kc < start + NA_COLS)
    dc = jnp.clip(kc - qc, -(NA_COLS - 1), NA_COLS - 1) + NA_COLS - 1
    for dr in range(N_DR - 1):
        bias = jnp.zeros(shape, jnp.float32)
        for d in range(N_DC):
            lo = rpb_ref[(h * N_DR + dr) * N_DC + d]
            hi = rpb_ref[(h * N_DR + dr + 1) * N_DC + d]
            bias = jnp.where(dc == d, jnp.where(second, hi, lo), bias)
        o_ref[0, dr] = jnp.where(inside, bias, NEG_INF)


def _na_bias(rpb):
    return pl.pallas_call(
        _na_bias_kernel,
        grid=(C_HEADS,),
        in_specs=[pl.BlockSpec(memory_space=pltpu.SMEM)],
        out_specs=pl.BlockSpec((1, N_DR - 1, GRID_W, 2 * GRID_W), lambda h: (h, 0, 0, 0)),
        out_shape=jax.ShapeDtypeStruct((C_HEADS, N_DR - 1, GRID_W, 2 * GRID_W), jnp.float32),
        compiler_params=pltpu.CompilerParams(dimension_semantics=("arbitrary",)),
        name="na_bias",
    )(rpb.reshape(-1))


def _lat_neighbourhood_kernel(q_ref, k_ref, v_ref, ck_ref, cv_ref, bias_ref, o_ref):
    hp = pl.program_id(1)
    r = pl.program_id(2)
    first_row = jnp.clip(r - NA_WIN_ROWS // 2, 0, GRID_ROWS - NA_WIN_ROWS)
    local = pl.ds(pl.multiple_of(first_row * GRID_W, GRID_W), NA_WIN_ROWS * GRID_W)
    dr0 = first_row - r + NA_ROWS - 1
    outs = []
    for j in range(2):
        lanes = slice(j * HEAD_DIM, (j + 1) * HEAD_DIM)
        bias = jnp.concatenate([bias_ref[hp * 2 + j, dr0 + a] for a in range(0, NA_WIN_ROWS, 2)], axis=1)
        parts = [
            (k_ref[local, :][:, lanes].astype(jnp.bfloat16), v_ref[local, :][:, lanes].astype(jnp.bfloat16), bias),
            (ck_ref[0, :, lanes].astype(jnp.bfloat16), cv_ref[0, :, lanes].astype(jnp.bfloat16), None),
        ]
        outs.append(_attend(_scaled_bf16(q_ref[:, lanes]), parts))
    o_ref[...] = jnp.concatenate(outs, axis=1)


def _lat_neighbourhood(p, cache_k, cache_v, bias_tiles):
    assert NA_WIN_ROWS % 2 == 0 and N_CTX_TOK % DEC_SEQ == 0
    q0, k0, v0 = COL0_QC // LANE, COL0_KC // LANE, COL0_VC // LANE
    row_base = N_CTX_TOK // GRID_W
    seq_base = N_CTX_TOK // DEC_SEQ
    whole = lambda c0: pl.BlockSpec((DEC_SEQ, LANE), lambda b, hp, r: (seq_base + b, c0 + hp))
    cache = pl.BlockSpec((1, PAST_LEN, LANE), lambda b, hp, r: (b, 0, hp))
    return pl.pallas_call(
        _lat_neighbourhood_kernel,
        grid=(DEC_BATCH, C_HEADS // 2, GRID_ROWS),
        in_specs=[
            pl.BlockSpec((GRID_W, LANE), lambda b, hp, r: (row_base + b * GRID_ROWS + r, q0 + hp)),
            whole(k0), whole(v0), cache, cache,
            pl.BlockSpec(bias_tiles.shape, lambda b, hp, r: (0, 0, 0, 0)),
        ],
        out_specs=pl.BlockSpec((GRID_W, LANE), lambda b, hp, r: (b * GRID_ROWS + r, hp)),
        out_shape=jax.ShapeDtypeStruct((N_LAT_TOK, C_HEADS * HEAD_DIM), jnp.float32),
        compiler_params=pltpu.CompilerParams(
            dimension_semantics=("arbitrary", "arbitrary", "arbitrary"), vmem_limit_bytes=VMEM_LIMIT_BYTES),
        name="lat_neighbourhood",
    )(p, p, p, cache_k, cache_v, bias_tiles)


def _rope_tables():
    quarter = HEAD_DIM // 4
    t = jnp.arange(DEC_SEQ)
    pos = jnp.stack([t // GRID_W, t % GRID_W], axis=-1).astype(jnp.float32)
    freqs = ROPE_BASE ** (-jnp.arange(quarter, dtype=jnp.float32) / quarter)
    ang = pos[:, :, None] * freqs
    ang = jnp.concatenate([ang, ang], axis=-1).reshape(DEC_SEQ, HEAD_DIM)
    ang = jnp.concatenate([ang] * (LANE // HEAD_DIM), axis=-1)
    return jnp.cos(ang), jnp.sin(ang)


def _final_norm_kernel(x_ref, g_ref, o_ref):
    x = x_ref[...]
    o_ref[...] = x * lax.rsqrt(jnp.mean(x * x, axis=-1, keepdims=True) + RMS_EPS) * g_ref[...]


def _final_norm(x, g):
    n, d = x.shape
    return pl.pallas_call(
        _final_norm_kernel,
        grid=(n // ROW_TILE,),
        in_specs=[pl.BlockSpec((ROW_TILE, d), lambda i: (i, 0)), pl.BlockSpec((1, d), lambda i: (0, 0))],
        out_specs=pl.BlockSpec((ROW_TILE, d), lambda i: (i, 0)),
        out_shape=jax.ShapeDtypeStruct((n, d), jnp.float32),
        compiler_params=pltpu.CompilerParams(dimension_semantics=("arbitrary",)),
        name="final_norm",
    )(x, g)


def kernel(x_prompt, x_sample, cache_a_k, cache_a_v, cache_c_k, cache_c_v, state_rwkv, c, c_ctx, ln1_g, ln2_g, lnf_g, ada_w, ada_b, w_in, a_sink, a_out, rw_mu, rw_w0, rw_w2, rw_a0, rw_a2, rw_g2, rw_kk, rw_ka, rw_rk, rw_lnx_g, rw_lnx_b, rw_out, na_rpb, na_out, w_o, pe_q, pe_subkeys, pe_u, pe_v):
    bf16 = jnp.bfloat16
    x = jnp.concatenate([x_prompt.reshape(N_CTX_TOK, D_MODEL), x_sample.reshape(N_LAT_TOK, D_MODEL)], axis=0)
    silu_all = jnp.concatenate([jax.nn.silu(c_ctx)[None, :], jax.nn.silu(c)], axis=0)
    n_mix = A_COLS + B_COLS + C_COLS
    rope_cos, rope_sin = _rope_tables()
    list_ak, list_av, list_ck, list_cv, list_st = [], [], [], [], []
    for l in range(DEPTH):
        lp = {
            "rw_mu": rw_mu[l], "rw_w0": rw_w0[l], "rw_w2": rw_w2[l], "rw_a0": rw_a0[l],
            "rw_a2": rw_a2[l], "rw_g2": rw_g2[l], "rw_kk": rw_kk[l], "rw_ka": rw_ka[l],
            "rw_rk": rw_rk[l], "rw_lnx_g": rw_lnx_g[l], "rw_lnx_b": rw_lnx_b[l],
        }
        mod = (silu_all @ ada_w[l] + ada_b[l]).reshape(N_SEG, N_MOD, D_MODEL)

        w_in_l = jnp.concatenate([w_in[l][:, n_mix:], w_in[l][:, :n_mix]], axis=1).astype(bf16)
        p, _ = _norm_proj(x, ln1_g[l][None, :], mod, w_in_l, 0, 1, col_tile=IN_COLS // 3)
        sink_rows = jnp.broadcast_to(a_sink[l][:, None], (A_HEADS, LANE))
        flat_cache = lambda cache: cache[:, l].reshape(DEC_BATCH, PAST_LEN, -1)
        ya = jnp.concatenate([
            _ctx_gqa(p, sink_rows),
            _lat_window(p, flat_cache(cache_a_k), flat_cache(cache_a_v), rope_cos, rope_sin, sink_rows)], axis=0)
        yc = jnp.concatenate([
            _ctx_mha(p),
            _lat_neighbourhood(p, flat_cache(cache_c_k), flat_cache(cache_c_v), _na_bias(na_rpb[l]))], axis=0)
        yb, st = _rwkv_mixer(p, lp, state_rwkv[:, l])
        new_cache = lambda col0, heads: p[:N_CTX_TOK, col0:col0 + heads * HEAD_DIM].reshape(BATCH, SEQ, heads, HEAD_DIM)
        list_ak.append(new_cache(COL0_KA, A_KV))
        list_av.append(new_cache(COL0_VA, A_KV))
        list_ck.append(new_cache(COL0_KC, C_HEADS))
        list_cv.append(new_cache(COL0_VC, C_HEADS))
        list_st.append(st)
        x = _merge(x, ya, yb, yc, p, mod,
                   a_out[l].astype(bf16), rw_out[l].astype(bf16), na_out[l].astype(bf16), w_o[l].astype(bf16))

        q, h2 = _norm_proj(x, ln2_g[l][None, :], mod, pe_q[l].astype(bf16), 3, 4,
                           col_tile=PEER_HEADS * PEER_KEY_DIM)
        s1m, s2m, thr, c0 = _peer_route(q, pe_subkeys[l])
        x = _peer_dense(h2, x, mod, s1m, s2m, thr, c0, pe_u[l].astype(bf16), pe_v[l].T.astype(bf16))

    y = _final_norm(x, lnf_g[None, :])
    y_prompt = y[:N_CTX_TOK].reshape(BATCH, SEQ, D_MODEL)
    y_sample = y[N_CTX_TOK:].reshape(DEC_BATCH, DEC_SEQ, D_MODEL)
    return (y_prompt, y_sample, jnp.stack(list_ak, axis=1), jnp.stack(list_av, axis=1),
            jnp.stack(list_ck, axis=1), jnp.stack(list_cv, axis=1), jnp.stack(list_st, axis=1))
```

```python
import functools

import jax
import jax.numpy as jnp
from jax import lax
from jax.experimental import pallas as pl
from jax.experimental.pallas import tpu as pltpu

D_MODEL = 1024
BATCH = 16
SEQ = 256
DEPTH = 4
DEC_BATCH = 2
DEC_SEQ = 2048
PAST_LEN = 512
GRID_W = 64
HEAD_DIM = 64
A_HEADS = 8
A_KV = 2
A_GROUP = A_HEADS // A_KV
WINDOW = 128
Q_BLOCK = 128
B_HEADS = 8
B_WIDTH = B_HEADS * HEAD_DIM
LORA_W = 64
LORA_A = 64
LORA_G = 128
GN_EPS = 64e-5
C_HEADS = 8
NA_ROWS = 8
NA_COLS = 16
N_BRANCH = 3
A_COLS = (A_HEADS + 2 * A_KV) * HEAD_DIM
B_COLS = 3 * B_WIDTH + 2 * LORA_W + 2 * LORA_A + LORA_G
C_COLS = 3 * C_HEADS * HEAD_DIM
IN_COLS = A_COLS + B_COLS + C_COLS + N_BRANCH * D_MODEL
PEER_HEADS = 8
N_KEYS = 128
N_EXPERTS = N_KEYS * N_KEYS
PEER_KEY_DIM = 256
PEER_TOPK = 16
PEER_CHUNK = 128
ROPE_BASE = 10000.0
RMS_EPS = 1e-6
N_MOD = 6

N_CTX_TOK = BATCH * SEQ
N_LAT_TOK = DEC_BATCH * DEC_SEQ
N_TOK = N_CTX_TOK + N_LAT_TOK
N_SEG = 1 + DEC_BATCH

VMEM_LIMIT_BYTES = 48 * 1024 * 1024

ROW_TILE = 512
assert N_CTX_TOK % ROW_TILE == 0 and DEC_SEQ % ROW_TILE == 0


def _segment_of_row_block(i):
    n_ctx_blocks = N_CTX_TOK // ROW_TILE
    blocks_per_lat = DEC_SEQ // ROW_TILE
    return jnp.where(i < n_ctx_blocks, 0, 1 + (i - n_ctx_blocks) // blocks_per_lat)


def _ada_mod_kernel(c_ref, w_ref, b_ref, o_ref):
    c = c_ref[...]
    o_ref[...] = jnp.dot(c * jax.nn.sigmoid(c), w_ref[...], preferred_element_type=jnp.float32,
                         precision=lax.Precision.HIGHEST) + b_ref[...]


def _ada_mod(cond, w, b):
    n_seg, d = cond.shape
    n_out = w.shape[1]
    return pl.pallas_call(
        _ada_mod_kernel,
        grid=(n_out // d,),
        in_specs=[pl.BlockSpec((n_seg, d), lambda j: (0, 0)),
                  pl.BlockSpec((d, d), lambda j: (0, j)),
                  pl.BlockSpec((1, d), lambda j: (0, j))],
        out_specs=pl.BlockSpec((n_seg, d), lambda j: (0, j)),
        out_shape=jax.ShapeDtypeStruct((n_seg, n_out), jnp.float32),
        compiler_params=pltpu.CompilerParams(dimension_semantics=("arbitrary",)),
        name="ada_mod",
    )(cond, w, b)


def _norm_proj_kernel(x_ref, g_ref, mod_ref, w_ref, o_ref, h_ref, *, shift_idx, scale_idx):
    @pl.when(pl.program_id(1) == 0)
    def _():
        x = x_ref[...]
        y = x * lax.rsqrt(jnp.mean(x * x, axis=-1, keepdims=True) + RMS_EPS) * g_ref[...]
        h = y * (1.0 + mod_ref[0, scale_idx:scale_idx + 1, :]) + mod_ref[0, shift_idx:shift_idx + 1, :]
        h_ref[...] = h.astype(jnp.bfloat16)

    o_ref[...] = jnp.dot(h_ref[...], w_ref[...], preferred_element_type=jnp.float32)


def _norm_proj(x, g, mod, w_bf16, shift_idx, scale_idx, col_tile):
    n_rows, d = x.shape
    n_cols = w_bf16.shape[1]
    assert n_cols % col_tile == 0
    return pl.pallas_call(
        functools.partial(_norm_proj_kernel, shift_idx=shift_idx, scale_idx=scale_idx),
        grid=(n_rows // ROW_TILE, n_cols // col_tile),
        in_specs=[
            pl.BlockSpec((ROW_TILE, d), lambda i, j: (i, 0)),
            pl.BlockSpec((1, d), lambda i, j: (0, 0)),
            pl.BlockSpec((1, N_MOD, d), lambda i, j: (_segment_of_row_block(i), 0, 0)),
            pl.BlockSpec((d, col_tile), lambda i, j: (0, j)),
        ],
        out_specs=[pl.BlockSpec((ROW_TILE, col_tile), lambda i, j: (i, j)),
                   pl.BlockSpec((ROW_TILE, d), lambda i, j: (i, 0))],
        out_shape=[jax.ShapeDtypeStruct((n_rows, n_cols), jnp.float32),
                   jax.ShapeDtypeStruct((n_rows, d), jnp.bfloat16)],
        compiler_params=pltpu.CompilerParams(
            dimension_semantics=("arbitrary", "arbitrary"), vmem_limit_bytes=VMEM_LIMIT_BYTES),
        name="norm_proj",
    )(x, g, mod, w_bf16)


LANE = 128
ROUTE_TOK = 512
PEER_TOK = 512
PEER_EXP = 1024
NEG_INF = float("-inf")
LOG2_E = 1.4426950408889634


def _take_top(vals, k):
    rows = lax.broadcasted_iota(jnp.int32, vals.shape, 0)
    taken = []
    for _ in range(k):
        m = jnp.max(vals, axis=0, keepdims=True)
        first = jnp.min(jnp.where(vals == m, rows, vals.shape[0]), axis=0, keepdims=True)
        vals = jnp.where(rows == first, NEG_INF, vals)
        taken.append(m)
    return jnp.concatenate(taken, axis=0), vals


def _take_top_distinct(vals, k):
    taken = []
    for _ in range(k):
        m = jnp.max(vals, axis=0, keepdims=True)
        vals = jnp.where(vals == m, NEG_INF, vals)
        taken.append(m)
    removed = jnp.sum((vals == NEG_INF).astype(jnp.int32), axis=0, keepdims=True)
    return jnp.concatenate(taken, axis=0), vals, removed


def _peer_route_kernel(q_ref, sk_ref, s1m_ref, s2m_ref, thr_ref, c0_ref):
    def route(tok, take_top):
        masked, top_vals, removed = [], [], []
        for z in range(2):
            qz = q_ref[tok, z * LANE:(z + 1) * LANE]
            s = lax.dot_general(sk_ref[0, z], qz, (((1,), (1,)), ((), ())),
                                preferred_element_type=jnp.float32)
            v, rest, n = take_top(s, PEER_TOPK)
            masked.append(jnp.where(rest == NEG_INF, s, NEG_INF))
            top_vals.append(v)
            removed.append(n)
        half = PEER_TOPK // 2
        v1, v2 = top_vals
        cand = jnp.concatenate(
            [v1[0:1] + v2] + [v1[i:i + 1] + v2[:half] for i in range(1, half)] + [v1[half:] + v2[0:1]], axis=0)
        sv, _, n = take_top(cand, PEER_TOPK)
        removed.append(n)
        z_sum = jnp.sum(jnp.exp(sv - sv[0:1]), axis=0, keepdims=True)
        s1m_ref[0, :, tok] = masked[0]
        s2m_ref[0, :, tok] = masked[1]
        thr_ref[0, :, tok] = sv[PEER_TOPK - 1:PEER_TOPK]
        c0_ref[0, :, tok] = -sv[0:1] - jnp.log(z_sum)
        return jnp.max(functools.reduce(jnp.maximum, removed))

    def exact_take_top(vals, k):
        v, rest = _take_top(vals, k)
        return v, rest, jnp.full((1, vals.shape[1]), k, jnp.int32)

    toks = [slice(i * LANE, (i + 1) * LANE) for i in range(ROUTE_TOK // LANE)]
    worst = [route(tok, _take_top_distinct) for tok in toks]
    for tok, n in zip(toks, worst):
        @pl.when(n > PEER_TOPK)
        def _():
            route(tok, exact_take_top)


def _peer_route(q, subkeys):
    n = q.shape[0]
    big = pl.BlockSpec((1, N_KEYS, ROUTE_TOK), lambda i, h: (h, 0, i))
    small = pl.BlockSpec((1, 1, ROUTE_TOK), lambda i, h: (h, 0, i))
    return pl.pallas_call(
        _peer_route_kernel,
        grid=(n // ROUTE_TOK, PEER_HEADS),
        in_specs=[
            pl.BlockSpec((ROUTE_TOK, PEER_KEY_DIM), lambda i, h: (i, h)),
            pl.BlockSpec((1, 2, N_KEYS, PEER_KEY_DIM // 2), lambda i, h: (h, 0, 0, 0)),
        ],
        out_specs=[big, big, small, small],
        out_shape=[jax.ShapeDtypeStruct((PEER_HEADS, N_KEYS, n), jnp.float32)] * 2
        + [jax.ShapeDtypeStruct((PEER_HEADS, 1, n), jnp.float32)] * 2,
        compiler_params=pltpu.CompilerParams(
            dimension_semantics=("arbitrary", "arbitrary"), vmem_limit_bytes=VMEM_LIMIT_BYTES),
        name="peer_route",
    )(q, subkeys)


def _gelu_tanh(x):
    return 0.5 * x * (1.0 + jnp.tanh(0.7978845608028654 * (x + 0.044715 * (x * x * x))))


def _peer_dense_kernel(h_ref, x_ref, mod_ref, s1m_ref, s2m_ref, thr_ref, c0_ref, u_ref, vt_ref, o_ref,
                       ht_scr, s1l_scr, s2l_scr, coef_scr, acc_scr, *, gate_idx):
    e = pl.program_id(1)

    @pl.when(e == 0)
    def _():
        ht_scr[...] = h_ref[...].astype(jnp.float32).T.astype(jnp.bfloat16)
        s1l_scr[...] = s1m_ref[...] * LOG2_E
        s2l_scr[...] = (s2m_ref[...] + c0_ref[...]) * LOG2_E
        acc_scr[...] = jnp.zeros_like(acc_scr)

    act = jnp.dot(u_ref[...], ht_scr[...], preferred_element_type=jnp.float32)
    a_per_step = PEER_EXP // N_KEYS
    a_rows = pl.ds(pl.multiple_of(e * a_per_step, a_per_step), a_per_step)
    for t in range(PEER_TOK // LANE):
        tok = slice(t * LANE, (t + 1) * LANE)
        s1_all = [s1m_ref[hh, a_rows, tok] for hh in range(PEER_HEADS)]
        s1l_all = [s1l_scr[hh, a_rows, tok] for hh in range(PEER_HEADS)]
        for a in range(a_per_step):
            rows = slice(a * N_KEYS, (a + 1) * N_KEYS)
            w = jnp.zeros((N_KEYS, LANE), jnp.float32)
            for hh in range(PEER_HEADS):
                s1 = s1_all[hh][a:a + 1]
                picked = (s1 + s2m_ref[hh, :, tok]) >= thr_ref[hh, :, tok]
                w = w + jnp.where(picked, jnp.exp2(s1l_all[hh][a:a + 1] + s2l_scr[hh, :, tok]), 0.0)
            coef_scr[rows, tok] = (w * _gelu_tanh(act[rows, tok])).astype(jnp.bfloat16)
    acc_scr[...] += jnp.dot(vt_ref[...], coef_scr[...], preferred_element_type=jnp.float32)

    @pl.when(e == pl.num_programs(1) - 1)
    def _():
        o_ref[...] = x_ref[...] + mod_ref[0, gate_idx:gate_idx + 1, :] * acc_scr[...].T


def _peer_dense(h, x, mod, s1m, s2m, thr, c0, u_bf16, vt_bf16):
    n, d = x.shape
    assert PEER_TOK == ROW_TILE
    tok_row = lambda i, e: (i, 0)
    big = pl.BlockSpec((PEER_HEADS, N_KEYS, PEER_TOK), lambda i, e: (0, 0, i))
    small = pl.BlockSpec((PEER_HEADS, 1, PEER_TOK), lambda i, e: (0, 0, i))
    return pl.pallas_call(
        functools.partial(_peer_dense_kernel, gate_idx=5),
        grid=(n // PEER_TOK, N_EXPERTS // PEER_EXP),
        in_specs=[
            pl.BlockSpec((PEER_TOK, d), tok_row),
            pl.BlockSpec((PEER_TOK, d), tok_row),
            pl.BlockSpec((1, N_MOD, d), lambda i, e: (_segment_of_row_block(i), 0, 0)),
            big, big, small, small,
            pl.BlockSpec((PEER_EXP, d), lambda i, e: (e, 0)),
            pl.BlockSpec((d, PEER_EXP), lambda i, e: (0, e)),
        ],
        out_specs=pl.BlockSpec((PEER_TOK, d), tok_row),
        out_shape=jax.ShapeDtypeStruct((n, d), jnp.float32),
        scratch_shapes=[
            pltpu.VMEM((d, PEER_TOK), jnp.bfloat16),
            pltpu.VMEM((PEER_HEADS, N_KEYS, PEER_TOK), jnp.float32),
            pltpu.VMEM((PEER_HEADS, N_KEYS, PEER_TOK), jnp.float32),
            pltpu.VMEM((PEER_EXP, PEER_TOK), jnp.bfloat16),
            pltpu.VMEM((d, PEER_TOK), jnp.float32),
        ],
        compiler_params=pltpu.CompilerParams(
            dimension_semantics=("arbitrary", "arbitrary"), vmem_limit_bytes=VMEM_LIMIT_BYTES),
        name="peer_dense",
    )(h, x, mod, s1m, s2m, thr, c0, u_bf16, vt_bf16)


def _merge_kernel(x_ref, yac_ref, yal_ref, yb_ref, ycc_ref, ycl_ref, ga_ref, gb_ref, gc_ref, mod_ref,
                  wa_ref, wb_ref, wc_ref, wo_ref, o_ref, *, gate_idx):
    is_ctx = pl.program_id(0) < N_CTX_TOK // ROW_TILE

    def branch(y, gate_ref, w_ref):
        proj = jnp.dot(y.astype(jnp.bfloat16), w_ref[...], preferred_element_type=jnp.float32)
        return jax.nn.sigmoid(gate_ref[...]) * proj

    ya = jnp.where(is_ctx, yac_ref[...], yal_ref[...])
    yc = jnp.where(is_ctx, ycc_ref[...], ycl_ref[...])
    merged = branch(ya, ga_ref, wa_ref) + branch(yb_ref[...], gb_ref, wb_ref) + branch(yc, gc_ref, wc_ref)
    out = jnp.dot(merged.astype(jnp.bfloat16), wo_ref[...], preferred_element_type=jnp.float32)
    o_ref[...] = x_ref[...] + mod_ref[0, gate_idx:gate_idx + 1, :] * out


def _merge(x, ya_ctx, ya_lat, yb, yc_ctx, yc_lat, p, mod, wa, wb, wc, wo):
    n_rows, d = x.shape
    n_ctx = N_CTX_TOK // ROW_TILE
    n_lat = N_LAT_TOK // ROW_TILE
    row = lambda i: (i, 0)
    const = lambda i: (0, 0)
    width = yb.shape[1]
    y_spec = pl.BlockSpec((ROW_TILE, width), row)
    ctx_spec = pl.BlockSpec((ROW_TILE, width), lambda i: (jnp.minimum(i, n_ctx - 1), 0))
    lat_spec = pl.BlockSpec((ROW_TILE, width), lambda i: (jnp.clip(i - n_ctx, 0, n_lat - 1), 0))
    w_spec = pl.BlockSpec(wa.shape, const)
    return pl.pallas_call(
        functools.partial(_merge_kernel, gate_idx=2),
        grid=(n_rows // ROW_TILE,),
        in_specs=[
            pl.BlockSpec((ROW_TILE, d), row), ctx_spec, lat_spec, y_spec, ctx_spec, lat_spec,
            pl.BlockSpec((ROW_TILE, d), lambda i: (i, 0)),
            pl.BlockSpec((ROW_TILE, d), lambda i: (i, 1)),
            pl.BlockSpec((ROW_TILE, d), lambda i: (i, 2)),
            pl.BlockSpec((1, N_MOD, d), lambda i: (_segment_of_row_block(i), 0, 0)),
            w_spec, w_spec, w_spec, pl.BlockSpec(wo.shape, const),
        ],
        out_specs=pl.BlockSpec((ROW_TILE, d), row),
        out_shape=jax.ShapeDtypeStruct((n_rows, d), jnp.float32),
        compiler_params=pltpu.CompilerParams(
            dimension_semantics=("arbitrary",), vmem_limit_bytes=VMEM_LIMIT_BYTES),
        name="merge_out_proj",
    )(x, ya_ctx, ya_lat, yb, yc_ctx, yc_lat, p, p, p, mod, wa, wb, wc, wo)


HEAD_PAIRS = B_HEADS // 2
PREP_TOK = 256
SCAN_CHUNK = 128
assert PREP_TOK == SEQ and DEC_SEQ % PREP_TOK == 0 and 2 * HEAD_DIM == LANE
_HI = lax.Precision.HIGHEST


def _head_ones(n):
    r = lax.broadcasted_iota(jnp.int32, (n, n), 0) // HEAD_DIM
    c = lax.broadcasted_iota(jnp.int32, (n, n), 1) // HEAD_DIM
    return (r == c).astype(jnp.float32)


def _softplus(y):
    return jnp.maximum(y, 0.0) + jnp.log(1.0 + jnp.exp(-jnp.abs(y)))


def _rwkv_prep_kernel(pb_ref, prev_ref, next_ref, mu_ref, w0_ref, w2_ref, a0_ref, a2_ref, g2_ref,
                      kks_ref, ka_ref, rk_ref, ones_ref,
                      r_o, kk_o, v_o, w0_o, w1_o, k0_o, k1_o, nb0_o, nb1_o, g_o, bonus_o):
    i = pl.program_id(0)
    n_ctx = N_CTX_TOK // PREP_TOK
    per_lat = DEC_SEQ // PREP_TOK
    j = (i - n_ctx) % per_lat
    starts_seq = jnp.logical_or(i < n_ctx, j == 0)
    ends_seq = jnp.logical_or(i < n_ctx, j == per_lat - 1)

    pb = pb_ref[...]
    rows = lax.broadcasted_iota(jnp.int32, (PREP_TOK, 1), 0)
    before = jnp.where(starts_seq, 0.0, prev_ref[7:8, :])
    after = jnp.where(ends_seq, 0.0, next_ref[0:1, :])
    prev = jnp.where(rows == 0, before, pltpu.roll(pb, 1, 0))
    nxt = jnp.where(rows == PREP_TOK - 1, after, pltpu.roll(pb, PREP_TOK - 1, 0))
    xb = pb + mu_ref[...] * (0.5 * (prev + nxt) - pb)

    w = B_WIDTH
    r, k, v = xb[:, 0:w], xb[:, w:2 * w], xb[:, 2 * w:3 * w]
    wd = xb[:, 3 * w:3 * w + 2 * LORA_W]
    ad = xb[:, 3 * w + 2 * LORA_W:3 * w + 2 * LORA_W + 2 * LORA_A]
    gd = xb[:, 3 * w + 2 * LORA_W + 2 * LORA_A:]
    ones = ones_ref[...]

    def head_sum(x):
        return jnp.dot(x, ones, preferred_element_type=jnp.float32, precision=_HI)

    lw = jnp.dot(jnp.tanh(wd), w2_ref[...], preferred_element_type=jnp.float32, precision=_HI)
    la = jnp.dot(ad, a2_ref[...], preferred_element_type=jnp.float32, precision=_HI)
    kk = k * kks_ref[...]
    kk = kk * lax.rsqrt(head_sum(kk * kk) + 1e-12)
    kd_sum = jnp.zeros_like(k)
    for z, (w_o, k_o, nb_o) in enumerate(((w0_o, k0_o, nb0_o), (w1_o, k1_o, nb1_o))):
        wlog = -_softplus(-(w0_ref[z:z + 1, :] + lw[:, z * w:(z + 1) * w])) - 0.5
        a = jax.nn.sigmoid(a0_ref[z:z + 1, :] + la[:, z * w:(z + 1) * w])
        kd = k * (1.0 + (a - 1.0) * ka_ref[...])
        kd_sum = kd_sum + kd
        w_o[...] = jnp.exp(-jnp.exp(wlog))
        k_o[...] = kd
        nb_o[...] = -(kk * a)
    r_o[...] = r
    kk_o[...] = kk
    v_o[...] = v
    g_o[...] = jnp.dot(jax.nn.sigmoid(gd), g2_ref[...], preferred_element_type=jnp.float32, precision=_HI)
    bonus_o[...] = head_sum(r * kd_sum * rk_ref[...]) * v


def _rwkv_prep(p, mu, w0, w2_pad, a0, a2_pad, g2, kk_scale, ka, rk):
    n = p.shape[0]
    col_blk = (N_BRANCH * D_MODEL + A_COLS) // B_COLS
    assert col_blk * B_COLS == N_BRANCH * D_MODEL + A_COLS
    halo = 8
    per = PREP_TOK // halo
    full = lambda a: pl.BlockSpec(a.shape, lambda i: (0,) * a.ndim)
    ones = _head_ones(B_WIDTH)
    consts = (mu, w0, w2_pad, a0, a2_pad, g2, kk_scale, ka, rk, ones)
    out_spec = pl.BlockSpec((PREP_TOK, B_WIDTH), lambda i: (i, 0))
    return pl.pallas_call(
        _rwkv_prep_kernel,
        grid=(n // PREP_TOK,),
        in_specs=[
            pl.BlockSpec((PREP_TOK, B_COLS), lambda i: (i, col_blk)),
            pl.BlockSpec((halo, B_COLS), lambda i: (jnp.maximum(i * per - 1, 0), col_blk)),
            pl.BlockSpec((halo, B_COLS), lambda i: (jnp.minimum((i + 1) * per, n // halo - 1), col_blk)),
        ] + [full(a) for a in consts],
        out_specs=[out_spec] * 11,
        out_shape=[jax.ShapeDtypeStruct((n, B_WIDTH), jnp.float32)] * 11,
        compiler_params=pltpu.CompilerParams(
            dimension_semantics=("arbitrary",), vmem_limit_bytes=VMEM_LIMIT_BYTES),
        name="rwkv_prep",
    )(p, p, p, *consts)


SCAN_LANES = 2 * LANE


def _head_sums(head_ones, *terms):
    sums = []
    for x, passes in terms:
        hi = x.astype(jnp.bfloat16)
        out = jnp.dot(hi, head_ones, preferred_element_type=jnp.float32)
        if passes == 2:
            lo = (x - hi.astype(jnp.float32)).astype(jnp.bfloat16)
            out = out + jnp.dot(lo, head_ones, preferred_element_type=jnp.float32)
        sums.append(out)
    return sums


def _rwkv_scan_kernel(rf, kkf, vf, wf, kf, nbf, rb, kkb, vb, wb, kb, nbb, s0_ref, ones_ref, eye_ref,
                      yf_ref, yb_ref, sfin_ref, s_scr, *, n_seq):
    c = pl.program_id(1)

    @pl.when(c == 0)
    def _():
        s_scr[...] = s0_ref[...]

    head_ones = ones_ref[...]
    eye = eye_ref[...]
    fwd_refs = (rf, kkf, vf, wf, kf, nbf)
    bwd_refs = (rb, kkb, vb, wb, kb, nbb)
    R, KK, V, W, K, NB = range(6)

    def step(t, carry):
        t_f = pl.ds(t, 1)
        t_b = pl.ds(SCAN_CHUNK - 1 - t, 1)
        rows_f = [[a[g, t_f, :] for a in fwd_refs] for g in range(n_seq)]
        rows_b = [[a[g, t_b, :] for a in bwd_refs] for g in range(n_seq)]

        def both(which, chain):
            g, p = chain
            lanes = slice(p * LANE, (p + 1) * LANE)
            return jnp.concatenate([rows_f[g][which][:, lanes], rows_b[g][which][:, lanes]], axis=1)

        chains = [(g, p) for g in range(n_seq) for p in range(HEAD_PAIRS)]
        rows = {ch: slice(i * HEAD_DIM, (i + 1) * HEAD_DIM) for i, ch in enumerate(chains)}
        s_old = {ch: s_scr[ch[0], ch[1]] for ch in chains}
        s_kk, v_col = _head_sums(
            head_ones,
            (jnp.concatenate([s_old[ch] * both(KK, ch) for ch in chains], axis=0), 1),
            (jnp.concatenate([eye * both(V, ch) for ch in chains], axis=0), 1))
        s_new = {}
        for ch in chains:
            s = s_old[ch] * both(W, ch) + s_kk[rows[ch]] * both(NB, ch) + v_col[rows[ch]] * both(K, ch)
            s_scr[ch[0], ch[1]] = s
            s_new[ch] = s
        (y_all,) = _head_sums(
            head_ones, (jnp.concatenate([s_new[ch] * both(R, ch) for ch in chains], axis=0), 1))
        for g in range(n_seq):
            y_rows = [jnp.sum(y_all[rows[(g, p)]] * eye, axis=0, keepdims=True) for p in range(HEAD_PAIRS)]
            yf_ref[g, t_f, :] = jnp.concatenate([y[:, :LANE] for y in y_rows], axis=1)
            yb_ref[g, t_b, :] = jnp.concatenate([y[:, LANE:] for y in y_rows], axis=1)
        return carry

    lax.fori_loop(0, SCAN_CHUNK, step, 0, unroll=16)

    @pl.when(c == pl.num_programs(1) - 1)
    def _():
        sfin_ref[...] = s_scr[...]


def _rwkv_scan(flat, s0, t_len, first_seq, seq_group):
    r, kk, v, w0, w1, k0, k1, nb0, nb1 = (a.reshape(-1, t_len, B_WIDTH) for a in flat)
    n_seq = s0.shape[0]
    n_chunks = t_len // SCAN_CHUNK
    assert n_seq % seq_group == 0 and first_seq % seq_group == 0 and n_chunks * SCAN_CHUNK == t_len
    first_blk = first_seq // seq_group
    blk = (seq_group, SCAN_CHUNK, B_WIDTH)
    fwd_in = pl.BlockSpec(blk, lambda i, c: (first_blk + i, c, 0))
    bwd_in = pl.BlockSpec(blk, lambda i, c: (first_blk + i, n_chunks - 1 - c, 0))
    fwd = pl.BlockSpec(blk, lambda i, c: (i, c, 0))
    bwd = pl.BlockSpec(blk, lambda i, c: (i, n_chunks - 1 - c, 0))
    state_blk = (seq_group, HEAD_PAIRS, HEAD_DIM, SCAN_LANES)
    state = pl.BlockSpec(state_blk, lambda i, c: (i, 0, 0, 0))
    head_ones = _head_ones(SCAN_LANES).astype(jnp.bfloat16)
    eye = (lax.broadcasted_iota(jnp.int32, (HEAD_DIM, SCAN_LANES), 0)
           == lax.broadcasted_iota(jnp.int32, (HEAD_DIM, SCAN_LANES), 1) % HEAD_DIM).astype(jnp.float32)
    const = lambda a: pl.BlockSpec(a.shape, lambda i, c: (0, 0))
    y_shape = jax.ShapeDtypeStruct((n_seq, t_len, B_WIDTH), jnp.float32)
    return pl.pallas_call(
        functools.partial(_rwkv_scan_kernel, n_seq=seq_group),
        grid=(n_seq // seq_group, n_chunks),
        in_specs=[fwd_in] * 6 + [bwd_in] * 6 + [state, const(head_ones), const(eye)],
        out_specs=[fwd, bwd, state],
        out_shape=[y_shape, y_shape, jax.ShapeDtypeStruct(s0.shape, jnp.float32)],
        scratch_shapes=[pltpu.VMEM(state_blk, jnp.float32)],
        compiler_params=pltpu.CompilerParams(
            dimension_semantics=("arbitrary", "arbitrary"), vmem_limit_bytes=VMEM_LIMIT_BYTES),
        name="rwkv_scan",
    )(r, kk, v, w0, k0, nb0, r, kk, v, w1, k1, nb1, s0, head_ones, eye)


def _rwkv_post_kernel(yfc_ref, ybc_ref, yfl_ref, ybl_ref, g_ref, bonus_ref, lng_ref, lnb_ref, ones_ref, o_ref):
    ones = ones_ref[...]
    is_ctx = pl.program_id(0) < N_CTX_TOK // PREP_TOK
    y = jnp.where(is_ctx, yfc_ref[...] + ybc_ref[...], yfl_ref[...] + ybl_ref[...])
    mean = jnp.dot(y, ones, preferred_element_type=jnp.float32, precision=_HI) * (1.0 / HEAD_DIM)
    d = y - mean
    var = jnp.dot(d * d, ones, preferred_element_type=jnp.float32, precision=_HI) * (1.0 / HEAD_DIM)
    y = d * lax.rsqrt(var + GN_EPS) * lng_ref[...] + lnb_ref[...]
    o_ref[...] = (y + bonus_ref[...]) * g_ref[...]


def _rwkv_post(yf_ctx, yb_ctx, yf_lat, yb_lat, g, bonus, lnx_g, lnx_b):
    n = g.shape[0]
    n_ctx = N_CTX_TOK // PREP_TOK
    n_lat = N_LAT_TOK // PREP_TOK
    row = pl.BlockSpec((PREP_TOK, B_WIDTH), lambda i: (i, 0))
    ctx_row = pl.BlockSpec((PREP_TOK, B_WIDTH), lambda i: (jnp.minimum(i, n_ctx - 1), 0))
    lat_row = pl.BlockSpec((PREP_TOK, B_WIDTH), lambda i: (jnp.clip(i - n_ctx, 0, n_lat - 1), 0))
    vec = pl.BlockSpec((1, B_WIDTH), lambda i: (0, 0))
    ones = _head_ones(B_WIDTH)
    return pl.pallas_call(
        _rwkv_post_kernel,
        grid=(n // PREP_TOK,),
        in_specs=[ctx_row, ctx_row, lat_row, lat_row, row, row, vec, vec, pl.BlockSpec(ones.shape, lambda i: (0, 0))],
        out_specs=row,
        out_shape=jax.ShapeDtypeStruct((n, B_WIDTH), jnp.float32),
        compiler_params=pltpu.CompilerParams(
            dimension_semantics=("arbitrary",), vmem_limit_bytes=VMEM_LIMIT_BYTES),
        name="rwkv_post",
    )(yf_ctx, yb_ctx, yf_lat, yb_lat, g, bonus, lnx_g, lnx_b, ones)


def _state_to_pairs(s):
    b = s.shape[0]
    s = s.reshape(b, 2, HEAD_PAIRS, 2, HEAD_DIM, HEAD_DIM)
    return jnp.transpose(s, (0, 2, 4, 1, 3, 5)).reshape(b, HEAD_PAIRS, HEAD_DIM, SCAN_LANES)


def _state_from_pairs(s):
    b = s.shape[0]
    s = s.reshape(b, HEAD_PAIRS, HEAD_DIM, 2, 2, HEAD_DIM)
    return jnp.transpose(s, (0, 3, 1, 4, 2, 5)).reshape(b, 2, B_HEADS, HEAD_DIM, HEAD_DIM)


def _rwkv_mixer(p, lp, state_lat):
    zeros = jnp.zeros((LORA_W, B_WIDTH), jnp.float32)
    w2_pad = jnp.concatenate([jnp.concatenate([lp["rw_w2"][0], zeros], axis=1),
                              jnp.concatenate([zeros, lp["rw_w2"][1]], axis=1)], axis=0)
    a2_pad = jnp.concatenate([jnp.concatenate([lp["rw_a2"][0], zeros], axis=1),
                              jnp.concatenate([zeros, lp["rw_a2"][1]], axis=1)], axis=0)
    row = lambda a: a.reshape(1, -1)
    prep = _rwkv_prep(p, row(lp["rw_mu"]), lp["rw_w0"], w2_pad, lp["rw_a0"], a2_pad, lp["rw_g2"],
                      row(lp["rw_kk"]), row(lp["rw_ka"]), row(lp["rw_rk"]))
    scan_in, (g, bonus) = prep[:9], prep[9:]
    s0_ctx = jnp.zeros((BATCH, HEAD_PAIRS, HEAD_DIM, SCAN_LANES), jnp.float32)
    yf_c, yb_c, s_ctx = _rwkv_scan(scan_in, s0_ctx, t_len=SEQ, first_seq=0, seq_group=4)
    yf_l, yb_l, _ = _rwkv_scan(scan_in, _state_to_pairs(state_lat), t_len=DEC_SEQ,
                               first_seq=N_CTX_TOK // DEC_SEQ, seq_group=DEC_BATCH)
    flat = lambda a: a.reshape(-1, B_WIDTH)
    y = _rwkv_post(flat(yf_c), flat(yb_c), flat(yf_l), flat(yb_l), g, bonus,
                   row(lp["rw_lnx_g"]), row(lp["rw_lnx_b"]))
    return y, _state_from_pairs(s_ctx)


ATT_SCALE = HEAD_DIM ** -0.5
COL0_QA = N_BRANCH * D_MODEL
COL0_KA = COL0_QA + A_HEADS * HEAD_DIM
COL0_VA = COL0_KA + A_KV * HEAD_DIM
COL0_QC = COL0_QA + A_COLS + B_COLS
COL0_KC = COL0_QC + C_HEADS * HEAD_DIM
COL0_VC = COL0_KC + C_HEADS * HEAD_DIM
A_GROUP_COLS = A_GROUP * HEAD_DIM
assert A_KV * HEAD_DIM == LANE and all(c % LANE == 0 for c in (COL0_QA, COL0_KA, COL0_VA, COL0_QC, COL0_KC, COL0_VC))
assert COL0_QA % A_GROUP_COLS == 0
GRID_ROWS = DEC_SEQ // GRID_W
NA_WIN_ROWS = min(NA_ROWS, GRID_ROWS)


def _attend(q, parts, sink=None):
    scores = []
    for k, _, bias in parts:
        s = lax.dot_general(q, k, (((1,), (1,)), ((), ())), preferred_element_type=jnp.float32)
        scores.append(s if bias is None else s + bias)
    m = functools.reduce(jnp.maximum, [jnp.max(s, axis=-1, keepdims=True) for s in scores])
    den = 0.0
    if sink is not None:
        m = jnp.maximum(m, sink)
        den = jnp.exp(sink - m)
    out = 0.0
    for s, (_, v, _) in zip(scores, parts):
        prob = jnp.exp(s - m)
        den = den + jnp.sum(prob, axis=-1, keepdims=True)
        out = out + jnp.dot(prob.astype(jnp.bfloat16), v, preferred_element_type=jnp.float32)
    return out / den


def _scaled_bf16(q):
    return (q * ATT_SCALE).astype(jnp.bfloat16)


def _kv_half(x, kv):
    return jnp.where(kv == 0, x[:, :HEAD_DIM], x[:, HEAD_DIM:])


def _ctx_gqa_kernel(q_ref, k_ref, v_ref, sink_ref, o_ref):
    kv = pl.program_id(1)
    k = _kv_half(k_ref[...], kv).astype(jnp.bfloat16)
    v = _kv_half(v_ref[...], kv).astype(jnp.bfloat16)
    outs = []
    for g in range(A_GROUP):
        q = _scaled_bf16(q_ref[:, g * HEAD_DIM:(g + 1) * HEAD_DIM])
        sink = sink_ref[pl.ds(kv * A_GROUP + g, 1), :][:, 0:1]
        outs.append(_attend(q, [(k, v, None)], sink))
    o_ref[...] = jnp.concatenate(outs, axis=1)


def _ctx_gqa(p, sink_rows):
    q0, k0, v0 = COL0_QA // A_GROUP_COLS, COL0_KA // LANE, COL0_VA // LANE
    return pl.pallas_call(
        _ctx_gqa_kernel,
        grid=(BATCH, A_KV),
        in_specs=[
            pl.BlockSpec((SEQ, A_GROUP_COLS), lambda b, kv: (b, q0 + kv)),
            pl.BlockSpec((SEQ, LANE), lambda b, kv: (b, k0)),
            pl.BlockSpec((SEQ, LANE), lambda b, kv: (b, v0)),
            pl.BlockSpec(sink_rows.shape, lambda b, kv: (0, 0)),
        ],
        out_specs=pl.BlockSpec((SEQ, A_GROUP_COLS), lambda b, kv: (b, kv)),
        out_shape=jax.ShapeDtypeStruct((N_CTX_TOK, A_HEADS * HEAD_DIM), jnp.float32),
        compiler_params=pltpu.CompilerParams(
            dimension_semantics=("arbitrary", "arbitrary"), vmem_limit_bytes=VMEM_LIMIT_BYTES),
        name="ctx_gqa",
    )(p, p, p, sink_rows)


def _ctx_mha_kernel(q_ref, k_ref, v_ref, o_ref):
    outs = []
    for j in range(2):
        lanes = slice(j * HEAD_DIM, (j + 1) * HEAD_DIM)
        outs.append(_attend(_scaled_bf16(q_ref[:, lanes]),
                            [(k_ref[:, lanes].astype(jnp.bfloat16), v_ref[:, lanes].astype(jnp.bfloat16), None)]))
    o_ref[...] = jnp.concatenate(outs, axis=1)


def _ctx_mha(p):
    q0, k0, v0 = COL0_QC // LANE, COL0_KC // LANE, COL0_VC // LANE
    return pl.pallas_call(
        _ctx_mha_kernel,
        grid=(BATCH, C_HEADS // 2),
        in_specs=[
            pl.BlockSpec((SEQ, LANE), lambda b, hp: (b, q0 + hp)),
            pl.BlockSpec((SEQ, LANE), lambda b, hp: (b, k0 + hp)),
            pl.BlockSpec((SEQ, LANE), lambda b, hp: (b, v0 + hp)),
        ],
        out_specs=pl.BlockSpec((SEQ, LANE), lambda b, hp: (b, hp)),
        out_shape=jax.ShapeDtypeStruct((N_CTX_TOK, C_HEADS * HEAD_DIM), jnp.float32),
        compiler_params=pltpu.CompilerParams(
            dimension_semantics=("arbitrary", "arbitrary"), vmem_limit_bytes=VMEM_LIMIT_BYTES),
        name="ctx_mha",
    )(p, p, p)


def _rope(x, cos, sin):
    reps = x.shape[1] // LANE
    if reps > 1:
        cos = jnp.concatenate([cos] * reps, axis=1)
        sin = jnp.concatenate([sin] * reps, axis=1)
    lane = lax.broadcasted_iota(jnp.int32, x.shape, 1)
    first = (lane % (HEAD_DIM // 2)) < (HEAD_DIM // 4)
    quarter = HEAD_DIM // 4
    pieces = [x[:, i * LANE:(i + 1) * LANE] for i in range(reps)]
    up = jnp.concatenate([pltpu.roll(piece, LANE - quarter, 1) for piece in pieces], axis=1)
    down = jnp.concatenate([pltpu.roll(piece, quarter, 1) for piece in pieces], axis=1)
    return x * cos + jnp.where(first, -up, down) * sin


def _lat_window_kernel(q_ref, kp_ref, kc_ref, kn_ref, vp_ref, vc_ref, vn_ref, ck_ref, cv_ref,
                       cosq_ref, sinq_ref, cosp_ref, sinp_ref, cosn_ref, sinn_ref, sink_ref, o_ref):
    qb = pl.program_id(1)
    kv = pl.program_id(2)
    n_blocks = pl.num_programs(1)
    row = lax.broadcasted_iota(jnp.int32, (Q_BLOCK, Q_BLOCK), 0)
    col = lax.broadcasted_iota(jnp.int32, (Q_BLOCK, Q_BLOCK), 1)
    bias_prev = jnp.where(jnp.logical_and(col >= row, qb > 0), 0.0, NEG_INF)
    bias_next = jnp.where(jnp.logical_and(col <= row, qb < n_blocks - 1), 0.0, NEG_INF)
    assert WINDOW == Q_BLOCK

    def keys(k_ref, cos_ref, sin_ref):
        return _kv_half(_rope(k_ref[...], cos_ref[...], sin_ref[...]), kv).astype(jnp.bfloat16)

    def vals(v_ref):
        return _kv_half(v_ref[...], kv).astype(jnp.bfloat16)

    parts = [
        (keys(kp_ref, cosp_ref, sinp_ref), vals(vp_ref), bias_prev),
        (keys(kc_ref, cosq_ref, sinq_ref), vals(vc_ref), None),
        (keys(kn_ref, cosn_ref, sinn_ref), vals(vn_ref), bias_next),
        (_kv_half(ck_ref[0], kv).astype(jnp.bfloat16), _kv_half(cv_ref[0], kv).astype(jnp.bfloat16), None),
    ]
    q_all = _rope(q_ref[...], cosq_ref[...], sinq_ref[...])
    outs = []
    for g in range(A_GROUP):
        sink = sink_ref[pl.ds(kv * A_GROUP + g, 1), :][:, 0:1]
        outs.append(_attend(_scaled_bf16(q_all[:, g * HEAD_DIM:(g + 1) * HEAD_DIM]), parts, sink))
    o_ref[...] = jnp.concatenate(outs, axis=1)


def _lat_window(p, cache_k, cache_v, cos, sin, sink_rows):
    n_blocks = DEC_SEQ // Q_BLOCK
    base = N_CTX_TOK // Q_BLOCK
    q0, k0, v0 = COL0_QA // A_GROUP_COLS, COL0_KA // LANE, COL0_VA // LANE
    prev = lambda b, qb, kv: base + b * n_blocks + jnp.maximum(qb - 1, 0)
    here = lambda b, qb, kv: base + b * n_blocks + qb
    nxt = lambda b, qb, kv: base + b * n_blocks + jnp.minimum(qb + 1, n_blocks - 1)
    slab = lambda rows, c0: pl.BlockSpec((Q_BLOCK, LANE), lambda b, qb, kv: (rows(b, qb, kv), c0))
    table = lambda blk: pl.BlockSpec((Q_BLOCK, LANE), lambda b, qb, kv: (blk(b, qb, kv), 0))
    cache = pl.BlockSpec((1, PAST_LEN, LANE), lambda b, qb, kv: (b, 0, 0))
    return pl.pallas_call(
        _lat_window_kernel,
        grid=(DEC_BATCH, n_blocks, A_KV),
        in_specs=[
            pl.BlockSpec((Q_BLOCK, A_GROUP_COLS), lambda b, qb, kv: (here(b, qb, kv), q0 + kv)),
            slab(prev, k0), slab(here, k0), slab(nxt, k0),
            slab(prev, v0), slab(here, v0), slab(nxt, v0),
            cache, cache,
            table(lambda b, qb, kv: qb), table(lambda b, qb, kv: qb),
            table(lambda b, qb, kv: jnp.maximum(qb - 1, 0)), table(lambda b, qb, kv: jnp.maximum(qb - 1, 0)),
            table(lambda b, qb, kv: jnp.minimum(qb + 1, n_blocks - 1)),
            table(lambda b, qb, kv: jnp.minimum(qb + 1, n_blocks - 1)),
            pl.BlockSpec(sink_rows.shape, lambda b, qb, kv: (0, 0)),
        ],
        out_specs=pl.BlockSpec((Q_BLOCK, A_GROUP_COLS), lambda b, qb, kv: (b * n_blocks + qb, kv)),
        out_shape=jax.ShapeDtypeStruct((N_LAT_TOK, A_HEADS * HEAD_DIM), jnp.float32),
        compiler_params=pltpu.CompilerParams(
            dimension_semantics=("arbitrary", "arbitrary", "arbitrary"), vmem_limit_bytes=VMEM_LIMIT_BYTES),
        name="lat_window",
    )(p, p, p, p, p, p, p, cache_k, cache_v, cos, sin, cos, sin, cos, sin, sink_rows)


N_DC = 2 * NA_COLS - 1
N_DR = 2 * NA_ROWS - 1


def _na_bias_kernel(rpb_ref, o_ref):
    h = pl.program_id(0)
    shape = (GRID_W, 2 * GRID_W)
    qc = lax.broadcasted_iota(jnp.int32, shape, 0)
    kc = lax.broadcasted_iota(jnp.int32, shape, 1) % GRID_W
    second = lax.broadcasted_iota(jnp.int32, shape, 1) >= GRID_W
    start = jnp.clip(qc - NA_COLS // 2, 0, GRID_W - NA_COLS)
    inside = jnp.logical_and(kc >= start, kc < start + NA_COLS)
    dc = jnp.clip(kc - qc, -(NA_COLS - 1), NA_COLS - 1) + NA_COLS - 1
    for dr in range(N_DR - 1):
        bias = jnp.zeros(shape, jnp.float32)
        for d in range(N_DC):
            lo = rpb_ref[(h * N_DR + dr) * N_DC + d]
            hi = rpb_ref[(h * N_DR + dr + 1) * N_DC + d]
            bias = jnp.where(dc == d, jnp.where(second, hi, lo), bias)
        o_ref[0, dr] = jnp.where(inside, bias, NEG_INF)


def _na_bias(rpb):
    return pl.pallas_call(
        _na_bias_kernel,
        grid=(C_HEADS,),
        in_specs=[pl.BlockSpec(memory_space=pltpu.SMEM)],
        out_specs=pl.BlockSpec((1, N_DR - 1, GRID_W, 2 * GRID_W), lambda h: (h, 0, 0, 0)),
        out_shape=jax.ShapeDtypeStruct((C_HEADS, N_DR - 1, GRID_W, 2 * GRID_W), jnp.float32),
        compiler_params=pltpu.CompilerParams(dimension_semantics=("arbitrary",)),
        name="na_bias",
    )(rpb.reshape(-1))


def _lat_neighbourhood_kernel(q_ref, k_ref, v_ref, ck_ref, cv_ref, bias_ref, o_ref):
    hp = pl.program_id(1)
    r = pl.program_id(2)
    first_row = jnp.clip(r - NA_WIN_ROWS // 2, 0, GRID_ROWS - NA_WIN_ROWS)
    local = pl.ds(pl.multiple_of(first_row * GRID_W, GRID_W), NA_WIN_ROWS * GRID_W)
    dr0 = first_row - r + NA_ROWS - 1
    outs = []
    for j in range(2):
        lanes = slice(j * HEAD_DIM, (j + 1) * HEAD_DIM)
        bias = jnp.concatenate([bias_ref[hp * 2 + j, dr0 + a] for a in range(0, NA_WIN_ROWS, 2)], axis=1)
        parts = [
            (k_ref[local, :][:, lanes].astype(jnp.bfloat16), v_ref[local, :][:, lanes].astype(jnp.bfloat16), bias),
            (ck_ref[0, :, lanes].astype(jnp.bfloat16), cv_ref[0, :, lanes].astype(jnp.bfloat16), None),
        ]
        outs.append(_attend(_scaled_bf16(q_ref[:, lanes]), parts))
    o_ref[...] = jnp.concatenate(outs, axis=1)


def _lat_neighbourhood(p, cache_k, cache_v, bias_tiles):
    assert NA_WIN_ROWS % 2 == 0 and N_CTX_TOK % DEC_SEQ == 0
    q0, k0, v0 = COL0_QC // LANE, COL0_KC // LANE, COL0_VC // LANE
    row_base = N_CTX_TOK // GRID_W
    seq_base = N_CTX_TOK // DEC_SEQ
    whole = lambda c0: pl.BlockSpec((DEC_SEQ, LANE), lambda b, hp, r: (seq_base + b, c0 + hp))
    cache = pl.BlockSpec((1, PAST_LEN, LANE), lambda b, hp, r: (b, 0, hp))
    return pl.pallas_call(
        _lat_neighbourhood_kernel,
        grid=(DEC_BATCH, C_HEADS // 2, GRID_ROWS),
        in_specs=[
            pl.BlockSpec((GRID_W, LANE), lambda b, hp, r: (row_base + b * GRID_ROWS + r, q0 + hp)),
            whole(k0), whole(v0), cache, cache,
            pl.BlockSpec(bias_tiles.shape, lambda b, hp, r: (0, 0, 0, 0)),
        ],
        out_specs=pl.BlockSpec((GRID_W, LANE), lambda b, hp, r: (b * GRID_ROWS + r, hp)),
        out_shape=jax.ShapeDtypeStruct((N_LAT_TOK, C_HEADS * HEAD_DIM), jnp.float32),
        compiler_params=pltpu.CompilerParams(
            dimension_semantics=("arbitrary", "arbitrary", "arbitrary"), vmem_limit_bytes=VMEM_LIMIT_BYTES),
        name="lat_neighbourhood",
    )(p, p, p, cache_k, cache_v, bias_tiles)


def _rope_tables():
    quarter = HEAD_DIM // 4
    t = jnp.arange(DEC_SEQ)
    pos = jnp.stack([t // GRID_W, t % GRID_W], axis=-1).astype(jnp.float32)
    freqs = ROPE_BASE ** (-jnp.arange(quarter, dtype=jnp.float32) / quarter)
    ang = pos[:, :, None] * freqs
    ang = jnp.concatenate([ang, ang], axis=-1).reshape(DEC_SEQ, HEAD_DIM)
    ang = jnp.concatenate([ang] * (LANE // HEAD_DIM), axis=-1)
    return jnp.cos(ang), jnp.sin(ang)


def _final_norm_kernel(x_ref, g_ref, o_ref):
    x = x_ref[...]
    o_ref[...] = x * lax.rsqrt(jnp.mean(x * x, axis=-1, keepdims=True) + RMS_EPS) * g_ref[...]


def _final_norm(x, g):
    n, d = x.shape
    return pl.pallas_call(
        _final_norm_kernel,
        grid=(n // ROW_TILE,),
        in_specs=[pl.BlockSpec((ROW_TILE, d), lambda i: (i, 0)), pl.BlockSpec((1, d), lambda i: (0, 0))],
        out_specs=pl.BlockSpec((ROW_TILE, d), lambda i: (i, 0)),
        out_shape=jax.ShapeDtypeStruct((n, d), jnp.float32),
        compiler_params=pltpu.CompilerParams(dimension_semantics=("arbitrary",)),
        name="final_norm",
    )(x, g)


def kernel(x_prompt, x_sample, cache_a_k, cache_a_v, cache_c_k, cache_c_v, state_rwkv, c, c_ctx, ln1_g, ln2_g, lnf_g, ada_w, ada_b, w_in, a_sink, a_out, rw_mu, rw_w0, rw_w2, rw_a0, rw_a2, rw_g2, rw_kk, rw_ka, rw_rk, rw_lnx_g, rw_lnx_b, rw_out, na_rpb, na_out, w_o, pe_q, pe_subkeys, pe_u, pe_v):
    bf16 = jnp.bfloat16
    x = jnp.concatenate([x_prompt.reshape(N_CTX_TOK, D_MODEL), x_sample.reshape(N_LAT_TOK, D_MODEL)], axis=0)
    cond = jnp.concatenate([c_ctx[None, :], c], axis=0)
    n_mix = A_COLS + B_COLS + C_COLS
    rope_cos, rope_sin = _rope_tables()
    list_ak, list_av, list_ck, list_cv, list_st = [], [], [], [], []
    for l in range(DEPTH):
        lp = {
            "rw_mu": rw_mu[l], "rw_w0": rw_w0[l], "rw_w2": rw_w2[l], "rw_a0": rw_a0[l],
            "rw_a2": rw_a2[l], "rw_g2": rw_g2[l], "rw_kk": rw_kk[l], "rw_ka": rw_ka[l],
            "rw_rk": rw_rk[l], "rw_lnx_g": rw_lnx_g[l], "rw_lnx_b": rw_lnx_b[l],
        }
        mod = _ada_mod(cond, ada_w[l], ada_b[l][None, :]).reshape(N_SEG, N_MOD, D_MODEL)

        w_in_l = jnp.concatenate([w_in[l][:, n_mix:], w_in[l][:, :n_mix]], axis=1).astype(bf16)
        p, _ = _norm_proj(x, ln1_g[l][None, :], mod, w_in_l, 0, 1, col_tile=IN_COLS // 3)
        sink_rows = jnp.broadcast_to(a_sink[l][:, None], (A_HEADS, LANE))
        flat_cache = lambda cache: cache[:, l].reshape(DEC_BATCH, PAST_LEN, -1)
        ya_ctx = _ctx_gqa(p, sink_rows)
        ya_lat = _lat_window(p, flat_cache(cache_a_k), flat_cache(cache_a_v), rope_cos, rope_sin, sink_rows)
        yc_ctx = _ctx_mha(p)
        yc_lat = _lat_neighbourhood(p, flat_cache(cache_c_k), flat_cache(cache_c_v), _na_bias(na_rpb[l]))
        yb, st = _rwkv_mixer(p, lp, state_rwkv[:, l])
        new_cache = lambda col0, heads: p[:N_CTX_TOK, col0:col0 + heads * HEAD_DIM].reshape(BATCH, SEQ, heads, HEAD_DIM)
        list_ak.append(new_cache(COL0_KA, A_KV))
        list_av.append(new_cache(COL0_VA, A_KV))
        list_ck.append(new_cache(COL0_KC, C_HEADS))
        list_cv.append(new_cache(COL0_VC, C_HEADS))
        list_st.append(st)
        x = _merge(x, ya_ctx, ya_lat, yb, yc_ctx, yc_lat, p, mod,
                   a_out[l].astype(bf16), rw_out[l].astype(bf16), na_out[l].astype(bf16), w_o[l].astype(bf16))

        q, h2 = _norm_proj(x, ln2_g[l][None, :], mod, pe_q[l].astype(bf16), 3, 4,
                           col_tile=PEER_HEADS * PEER_KEY_DIM)
        s1m, s2m, thr, c0 = _peer_route(q, pe_subkeys[l])
        x = _peer_dense(h2, x, mod, s1m, s2m, thr, c0, pe_u[l].astype(bf16), pe_v[l].T.astype(bf16))

    y = _final_norm(x, lnf_g[None, :])
    y_prompt = y[:N_CTX_TOK].reshape(BATCH, SEQ, D_MODEL)
    y_sample = y[N_CTX_TOK:].reshape(DEC_BATCH, DEC_SEQ, D_MODEL)
    return (y_prompt, y_sample, jnp.stack(list_ak, axis=1), jnp.stack(list_av, axis=1),
            jnp.stack(list_ck, axis=1), jnp.stack(list_cv, axis=1), jnp.stack(list_st, axis=1))
```

```python
import functools

import jax
import jax.numpy as jnp
from jax import lax
from jax.experimental import pallas as pl
from jax.experimental.pallas import tpu as pltpu

D_MODEL = 1024
BATCH = 16
SEQ = 256
DEPTH = 4
DEC_BATCH = 2
DEC_SEQ = 2048
PAST_LEN = 512
GRID_W = 64
HEAD_DIM = 64
A_HEADS = 8
A_KV = 2
A_GROUP = A_HEADS // A_KV
WINDOW = 128
Q_BLOCK = 128
B_HEADS = 8
B_WIDTH = B_HEADS * HEAD_DIM
LORA_W = 64
LORA_A = 64
LORA_G = 128
GN_EPS = 64e-5
C_HEADS = 8
NA_ROWS = 8
NA_COLS = 16
N_BRANCH = 3
A_COLS = (A_HEADS + 2 * A_KV) * HEAD_DIM
B_COLS = 3 * B_WIDTH + 2 * LORA_W + 2 * LORA_A + LORA_G
C_COLS = 3 * C_HEADS * HEAD_DIM
IN_COLS = A_COLS + B_COLS + C_COLS + N_BRANCH * D_MODEL
PEER_HEADS = 8
N_KEYS = 128
N_EXPERTS = N_KEYS * N_KEYS
PEER_KEY_DIM = 256
PEER_TOPK = 16
PEER_CHUNK = 128
ROPE_BASE = 10000.0
RMS_EPS = 1e-6
N_MOD = 6

N_CTX_TOK = BATCH * SEQ
N_LAT_TOK = DEC_BATCH * DEC_SEQ
N_TOK = N_CTX_TOK + N_LAT_TOK
N_SEG = 1 + DEC_BATCH

VMEM_LIMIT_BYTES = 48 * 1024 * 1024

ROW_TILE = 512
assert N_CTX_TOK % ROW_TILE == 0 and DEC_SEQ % ROW_TILE == 0


def _segment_of_row_block(i):
    n_ctx_blocks = N_CTX_TOK // ROW_TILE
    blocks_per_lat = DEC_SEQ // ROW_TILE
    return jnp.where(i < n_ctx_blocks, 0, 1 + (i - n_ctx_blocks) // blocks_per_lat)


def _ada_mod_kernel(c_ref, w_ref, b_ref, o_ref):
    c = c_ref[...]
    o_ref[...] = jnp.dot(c * jax.nn.sigmoid(c), w_ref[...], preferred_element_type=jnp.float32,
                         precision=lax.Precision.HIGHEST) + b_ref[...]


def _ada_mod(cond, w, b):
    n_seg, d = cond.shape
    n_out = w.shape[1]
    return pl.pallas_call(
        _ada_mod_kernel,
        grid=(n_out // d,),
        in_specs=[pl.BlockSpec((n_seg, d), lambda j: (0, 0)),
                  pl.BlockSpec((d, d), lambda j: (0, j)),
                  pl.BlockSpec((1, d), lambda j: (0, j))],
        out_specs=pl.BlockSpec((n_seg, d), lambda j: (0, j)),
        out_shape=jax.ShapeDtypeStruct((n_seg, n_out), jnp.float32),
        compiler_params=pltpu.CompilerParams(dimension_semantics=("arbitrary",)),
        name="ada_mod",
    )(cond, w, b)


def _norm_proj_kernel(x_ref, g_ref, mod_ref, w_ref, o_ref, h_ref, *, shift_idx, scale_idx):
    @pl.when(pl.program_id(1) == 0)
    def _():
        x = x_ref[...]
        y = x * lax.rsqrt(jnp.mean(x * x, axis=-1, keepdims=True) + RMS_EPS) * g_ref[...]
        h = y * (1.0 + mod_ref[0, scale_idx:scale_idx + 1, :]) + mod_ref[0, shift_idx:shift_idx + 1, :]
        h_ref[...] = h.astype(jnp.bfloat16)

    o_ref[...] = jnp.dot(h_ref[...], w_ref[...], preferred_element_type=jnp.float32)


def _norm_proj(x, g, mod, w_bf16, shift_idx, scale_idx, col_tile):
    n_rows, d = x.shape
    n_cols = w_bf16.shape[1]
    assert n_cols % col_tile == 0
    return pl.pallas_call(
        functools.partial(_norm_proj_kernel, shift_idx=shift_idx, scale_idx=scale_idx),
        grid=(n_rows // ROW_TILE, n_cols // col_tile),
        in_specs=[
            pl.BlockSpec((ROW_TILE, d), lambda i, j: (i, 0)),
            pl.BlockSpec((1, d), lambda i, j: (0, 0)),
            pl.BlockSpec((1, N_MOD, d), lambda i, j: (_segment_of_row_block(i), 0, 0)),
            pl.BlockSpec((d, col_tile), lambda i, j: (0, j)),
        ],
        out_specs=[pl.BlockSpec((ROW_TILE, col_tile), lambda i, j: (i, j)),
                   pl.BlockSpec((ROW_TILE, d), lambda i, j: (i, 0))],
        out_shape=[jax.ShapeDtypeStruct((n_rows, n_cols), jnp.float32),
                   jax.ShapeDtypeStruct((n_rows, d), jnp.bfloat16)],
        compiler_params=pltpu.CompilerParams(
            dimension_semantics=("arbitrary", "arbitrary"), vmem_limit_bytes=VMEM_LIMIT_BYTES),
        name="norm_proj",
    )(x, g, mod, w_bf16)


LANE = 128
ROUTE_TOK = 512
PEER_TOK = 512
PEER_EXP = 1024
NEG_INF = float("-inf")
LOG2_E = 1.4426950408889634


def _take_top(vals, k):
    rows = lax.broadcasted_iota(jnp.int32, vals.shape, 0)
    taken = []
    for _ in range(k):
        m = jnp.max(vals, axis=0, keepdims=True)
        first = jnp.min(jnp.where(vals == m, rows, vals.shape[0]), axis=0, keepdims=True)
        vals = jnp.where(rows == first, NEG_INF, vals)
        taken.append(m)
    return jnp.concatenate(taken, axis=0), vals


def _take_top_distinct(vals, k):
    taken = []
    for _ in range(k):
        m = jnp.max(vals, axis=0, keepdims=True)
        vals = jnp.where(vals == m, NEG_INF, vals)
        taken.append(m)
    removed = jnp.sum((vals == NEG_INF).astype(jnp.int32), axis=0, keepdims=True)
    return jnp.concatenate(taken, axis=0), vals, removed


def _peer_route_kernel(q_ref, sk_ref, s1m_ref, s2m_ref, thr_ref, c0_ref):
    def route(tok, take_top):
        masked, top_vals, removed = [], [], []
        for z in range(2):
            qz = q_ref[tok, z * LANE:(z + 1) * LANE]
            s = lax.dot_general(sk_ref[0, z], qz, (((1,), (1,)), ((), ())),
                                preferred_element_type=jnp.float32)
            v, rest, n = take_top(s, PEER_TOPK)
            masked.append(jnp.where(rest == NEG_INF, s, NEG_INF))
            top_vals.append(v)
            removed.append(n)
        half = PEER_TOPK // 2
        v1, v2 = top_vals
        cand = jnp.concatenate(
            [v1[0:1] + v2] + [v1[i:i + 1] + v2[:half] for i in range(1, half)] + [v1[half:] + v2[0:1]], axis=0)
        sv, _, n = take_top(cand, PEER_TOPK)
        removed.append(n)
        z_sum = jnp.sum(jnp.exp(sv - sv[0:1]), axis=0, keepdims=True)
        s1m_ref[0, :, tok] = masked[0]
        s2m_ref[0, :, tok] = masked[1]
        thr_ref[0, :, tok] = sv[PEER_TOPK - 1:PEER_TOPK]
        c0_ref[0, :, tok] = -sv[0:1] - jnp.log(z_sum)
        return jnp.max(functools.reduce(jnp.maximum, removed))

    def exact_take_top(vals, k):
        v, rest = _take_top(vals, k)
        return v, rest, jnp.full((1, vals.shape[1]), k, jnp.int32)

    toks = [slice(i * LANE, (i + 1) * LANE) for i in range(ROUTE_TOK // LANE)]
    worst = [route(tok, _take_top_distinct) for tok in toks]
    for tok, n in zip(toks, worst):
        @pl.when(n > PEER_TOPK)
        def _():
            route(tok, exact_take_top)


def _peer_route(q, subkeys):
    n = q.shape[0]
    big = pl.BlockSpec((1, N_KEYS, ROUTE_TOK), lambda i, h: (h, 0, i))
    small = pl.BlockSpec((1, 1, ROUTE_TOK), lambda i, h: (h, 0, i))
    return pl.pallas_call(
        _peer_route_kernel,
        grid=(n // ROUTE_TOK, PEER_HEADS),
        in_specs=[
            pl.BlockSpec((ROUTE_TOK, PEER_KEY_DIM), lambda i, h: (i, h)),
            pl.BlockSpec((1, 2, N_KEYS, PEER_KEY_DIM // 2), lambda i, h: (h, 0, 0, 0)),
        ],
        out_specs=[big, big, small, small],
        out_shape=[jax.ShapeDtypeStruct((PEER_HEADS, N_KEYS, n), jnp.float32)] * 2
        + [jax.ShapeDtypeStruct((PEER_HEADS, 1, n), jnp.float32)] * 2,
        compiler_params=pltpu.CompilerParams(
            dimension_semantics=("arbitrary", "arbitrary"), vmem_limit_bytes=VMEM_LIMIT_BYTES),
        name="peer_route",
    )(q, subkeys)


def _gelu_tanh(x):
    return 0.5 * x * (1.0 + jnp.tanh(0.7978845608028654 * (x + 0.044715 * (x * x * x))))


def _peer_dense_kernel(h_ref, x_ref, mod_ref, s1m_ref, s2m_ref, thr_ref, c0_ref, u_ref, vt_ref, o_ref,
                       ht_scr, s1l_scr, s2l_scr, coef_scr, acc_scr, *, gate_idx):
    e = pl.program_id(1)

    @pl.when(e == 0)
    def _():
        ht_scr[...] = h_ref[...].astype(jnp.float32).T.astype(jnp.bfloat16)
        s1l_scr[...] = s1m_ref[...] * LOG2_E
        s2l_scr[...] = (s2m_ref[...] + c0_ref[...]) * LOG2_E
        acc_scr[...] = jnp.zeros_like(acc_scr)

    act = jnp.dot(u_ref[...], ht_scr[...], preferred_element_type=jnp.float32)
    a_per_step = PEER_EXP // N_KEYS
    a_rows = pl.ds(pl.multiple_of(e * a_per_step, a_per_step), a_per_step)
    for t in range(PEER_TOK // LANE):
        tok = slice(t * LANE, (t + 1) * LANE)
        s1_all = [s1m_ref[hh, a_rows, tok] for hh in range(PEER_HEADS)]
        s1l_all = [s1l_scr[hh, a_rows, tok] for hh in range(PEER_HEADS)]
        for a in range(a_per_step):
            rows = slice(a * N_KEYS, (a + 1) * N_KEYS)
            w = jnp.zeros((N_KEYS, LANE), jnp.float32)
            for hh in range(PEER_HEADS):
                s1 = s1_all[hh][a:a + 1]
                picked = (s1 + s2m_ref[hh, :, tok]) >= thr_ref[hh, :, tok]
                w = w + jnp.where(picked, jnp.exp2(s1l_all[hh][a:a + 1] + s2l_scr[hh, :, tok]), 0.0)
            coef_scr[rows, tok] = (w * _gelu_tanh(act[rows, tok])).astype(jnp.bfloat16)
    acc_scr[...] += jnp.dot(vt_ref[...], coef_scr[...], preferred_element_type=jnp.float32)

    @pl.when(e == pl.num_programs(1) - 1)
    def _():
        o_ref[...] = x_ref[...] + mod_ref[0, gate_idx:gate_idx + 1, :] * acc_scr[...].T


def _peer_dense(h, x, mod, s1m, s2m, thr, c0, u_bf16, vt_bf16):
    n, d = x.shape
    assert PEER_TOK == ROW_TILE
    tok_row = lambda i, e: (i, 0)
    big = pl.BlockSpec((PEER_HEADS, N_KEYS, PEER_TOK), lambda i, e: (0, 0, i))
    small = pl.BlockSpec((PEER_HEADS, 1, PEER_TOK), lambda i, e: (0, 0, i))
    return pl.pallas_call(
        functools.partial(_peer_dense_kernel, gate_idx=5),
        grid=(n // PEER_TOK, N_EXPERTS // PEER_EXP),
        in_specs=[
            pl.BlockSpec((PEER_TOK, d), tok_row),
            pl.BlockSpec((PEER_TOK, d), tok_row),
            pl.BlockSpec((1, N_MOD, d), lambda i, e: (_segment_of_row_block(i), 0, 0)),
            big, big, small, small,
            pl.BlockSpec((PEER_EXP, d), lambda i, e: (e, 0)),
            pl.BlockSpec((d, PEER_EXP), lambda i, e: (0, e)),
        ],
        out_specs=pl.BlockSpec((PEER_TOK, d), tok_row),
        out_shape=jax.ShapeDtypeStruct((n, d), jnp.float32),
        scratch_shapes=[
            pltpu.VMEM((d, PEER_TOK), jnp.bfloat16),
            pltpu.VMEM((PEER_HEADS, N_KEYS, PEER_TOK), jnp.float32),
            pltpu.VMEM((PEER_HEADS, N_KEYS, PEER_TOK), jnp.float32),
            pltpu.VMEM((PEER_EXP, PEER_TOK), jnp.bfloat16),
            pltpu.VMEM((d, PEER_TOK), jnp.float32),
        ],
        compiler_params=pltpu.CompilerParams(
            dimension_semantics=("arbitrary", "arbitrary"), vmem_limit_bytes=VMEM_LIMIT_BYTES),
        name="peer_dense",
    )(h, x, mod, s1m, s2m, thr, c0, u_bf16, vt_bf16)


def _merge_kernel(x_ref, yac_ref, yal_ref, ycc_ref, ycl_ref,
                  sfc_ref, sbc_ref, sfl_ref, sbl_ref, g_ref, bonus_ref, lng_ref, lnb_ref, ones_ref,
                  ga_ref, gb_ref, gc_ref, mod_ref, wa_ref, wb_ref, wc_ref, wo_ref, o_ref, *, gate_idx):
    is_ctx = pl.program_id(0) < N_CTX_TOK // ROW_TILE

    def branch(y, gate_ref, w_ref):
        proj = jnp.dot(y.astype(jnp.bfloat16), w_ref[...], preferred_element_type=jnp.float32)
        return jax.nn.sigmoid(gate_ref[...]) * proj

    def head_mean(v):
        return jnp.dot(v, ones_ref[...], preferred_element_type=jnp.float32,
                       precision=lax.Precision.HIGHEST) * (1.0 / HEAD_DIM)

    y = jnp.where(is_ctx, sfc_ref[...] + sbc_ref[...], sfl_ref[...] + sbl_ref[...])
    dev = y - head_mean(y)
    y = dev * lax.rsqrt(head_mean(dev * dev) + GN_EPS) * lng_ref[...] + lnb_ref[...]
    yb = (y + bonus_ref[...]) * g_ref[...]

    ya = jnp.where(is_ctx, yac_ref[...], yal_ref[...])
    yc = jnp.where(is_ctx, ycc_ref[...], ycl_ref[...])
    merged = branch(ya, ga_ref, wa_ref) + branch(yb, gb_ref, wb_ref) + branch(yc, gc_ref, wc_ref)
    out = jnp.dot(merged.astype(jnp.bfloat16), wo_ref[...], preferred_element_type=jnp.float32)
    o_ref[...] = x_ref[...] + mod_ref[0, gate_idx:gate_idx + 1, :] * out


def _merge(x, ya_ctx, ya_lat, yc_ctx, yc_lat, rwkv_post_in, head_ones, p, mod, wa, wb, wc, wo):
    n_rows, d = x.shape
    n_ctx = N_CTX_TOK // ROW_TILE
    n_lat = N_LAT_TOK // ROW_TILE
    scan_fc, scan_bc, scan_fl, scan_bl, g, bonus, lnx_g, lnx_b = rwkv_post_in
    row = lambda i: (i, 0)
    const = lambda i: (0, 0)
    width = g.shape[1]
    y_spec = pl.BlockSpec((ROW_TILE, width), row)
    ctx_spec = pl.BlockSpec((ROW_TILE, width), lambda i: (jnp.minimum(i, n_ctx - 1), 0))
    lat_spec = pl.BlockSpec((ROW_TILE, width), lambda i: (jnp.clip(i - n_ctx, 0, n_lat - 1), 0))
    vec = pl.BlockSpec((1, width), const)
    w_spec = pl.BlockSpec(wa.shape, const)
    return pl.pallas_call(
        functools.partial(_merge_kernel, gate_idx=2),
        grid=(n_rows // ROW_TILE,),
        in_specs=[
            pl.BlockSpec((ROW_TILE, d), row), ctx_spec, lat_spec, ctx_spec, lat_spec,
            ctx_spec, ctx_spec, lat_spec, lat_spec, y_spec, y_spec, vec, vec, pl.BlockSpec(head_ones.shape, const),
            pl.BlockSpec((ROW_TILE, d), lambda i: (i, 0)),
            pl.BlockSpec((ROW_TILE, d), lambda i: (i, 1)),
            pl.BlockSpec((ROW_TILE, d), lambda i: (i, 2)),
            pl.BlockSpec((1, N_MOD, d), lambda i: (_segment_of_row_block(i), 0, 0)),
            w_spec, w_spec, w_spec, pl.BlockSpec(wo.shape, const),
        ],
        out_specs=pl.BlockSpec((ROW_TILE, d), row),
        out_shape=jax.ShapeDtypeStruct((n_rows, d), jnp.float32),
        compiler_params=pltpu.CompilerParams(
            dimension_semantics=("arbitrary",), vmem_limit_bytes=VMEM_LIMIT_BYTES),
        name="merge_out_proj",
    )(x, ya_ctx, ya_lat, yc_ctx, yc_lat, scan_fc, scan_bc, scan_fl, scan_bl, g, bonus, lnx_g, lnx_b, head_ones,
      p, p, p, mod, wa, wb, wc, wo)


HEAD_PAIRS = B_HEADS // 2
PREP_TOK = 256
SCAN_CHUNK = 128
assert PREP_TOK == SEQ and DEC_SEQ % PREP_TOK == 0 and 2 * HEAD_DIM == LANE
_HI = lax.Precision.HIGHEST


def _head_ones(n):
    r = lax.broadcasted_iota(jnp.int32, (n, n), 0) // HEAD_DIM
    c = lax.broadcasted_iota(jnp.int32, (n, n), 1) // HEAD_DIM
    return (r == c).astype(jnp.float32)


def _softplus(y):
    return jnp.maximum(y, 0.0) + jnp.log(1.0 + jnp.exp(-jnp.abs(y)))


def _rwkv_prep_kernel(pb_ref, prev_ref, next_ref, mu_ref, w0_ref, w2_ref, a0_ref, a2_ref, g2_ref,
                      kks_ref, ka_ref, rk_ref, ones_ref,
                      r_o, kk_o, v_o, w0_o, w1_o, k0_o, k1_o, nb0_o, nb1_o, g_o, bonus_o):
    i = pl.program_id(0)
    n_ctx = N_CTX_TOK // PREP_TOK
    per_lat = DEC_SEQ // PREP_TOK
    j = (i - n_ctx) % per_lat
    starts_seq = jnp.logical_or(i < n_ctx, j == 0)
    ends_seq = jnp.logical_or(i < n_ctx, j == per_lat - 1)

    pb = pb_ref[...]
    rows = lax.broadcasted_iota(jnp.int32, (PREP_TOK, 1), 0)
    before = jnp.where(starts_seq, 0.0, prev_ref[7:8, :])
    after = jnp.where(ends_seq, 0.0, next_ref[0:1, :])
    prev = jnp.where(rows == 0, before, pltpu.roll(pb, 1, 0))
    nxt = jnp.where(rows == PREP_TOK - 1, after, pltpu.roll(pb, PREP_TOK - 1, 0))
    xb = pb + mu_ref[...] * (0.5 * (prev + nxt) - pb)

    w = B_WIDTH
    r, k, v = xb[:, 0:w], xb[:, w:2 * w], xb[:, 2 * w:3 * w]
    wd = xb[:, 3 * w:3 * w + 2 * LORA_W]
    ad = xb[:, 3 * w + 2 * LORA_W:3 * w + 2 * LORA_W + 2 * LORA_A]
    gd = xb[:, 3 * w + 2 * LORA_W + 2 * LORA_A:]
    ones = ones_ref[...]

    def head_sum(x):
        return jnp.dot(x, ones, preferred_element_type=jnp.float32, precision=_HI)

    lw = jnp.dot(jnp.tanh(wd), w2_ref[...], preferred_element_type=jnp.float32, precision=_HI)
    la = jnp.dot(ad, a2_ref[...], preferred_element_type=jnp.float32, precision=_HI)
    kk = k * kks_ref[...]
    kk = kk * lax.rsqrt(head_sum(kk * kk) + 1e-12)
    kd_sum = jnp.zeros_like(k)
    for z, (w_o, k_o, nb_o) in enumerate(((w0_o, k0_o, nb0_o), (w1_o, k1_o, nb1_o))):
        wlog = -_softplus(-(w0_ref[z:z + 1, :] + lw[:, z * w:(z + 1) * w])) - 0.5
        a = jax.nn.sigmoid(a0_ref[z:z + 1, :] + la[:, z * w:(z + 1) * w])
        kd = k * (1.0 + (a - 1.0) * ka_ref[...])
        kd_sum = kd_sum + kd
        w_o[...] = jnp.exp(-jnp.exp(wlog))
        k_o[...] = kd
        nb_o[...] = -(kk * a)
    r_o[...] = r
    kk_o[...] = kk
    v_o[...] = v
    g_o[...] = jnp.dot(jax.nn.sigmoid(gd), g2_ref[...], preferred_element_type=jnp.float32, precision=_HI)
    bonus_o[...] = head_sum(r * kd_sum * rk_ref[...]) * v


def _rwkv_prep(p, mu, w0, w2_pad, a0, a2_pad, g2, kk_scale, ka, rk):
    n = p.shape[0]
    col_blk = (N_BRANCH * D_MODEL + A_COLS) // B_COLS
    assert col_blk * B_COLS == N_BRANCH * D_MODEL + A_COLS
    halo = 8
    per = PREP_TOK // halo
    full = lambda a: pl.BlockSpec(a.shape, lambda i: (0,) * a.ndim)
    ones = _head_ones(B_WIDTH)
    consts = (mu, w0, w2_pad, a0, a2_pad, g2, kk_scale, ka, rk, ones)
    out_spec = pl.BlockSpec((PREP_TOK, B_WIDTH), lambda i: (i, 0))
    return pl.pallas_call(
        _rwkv_prep_kernel,
        grid=(n // PREP_TOK,),
        in_specs=[
            pl.BlockSpec((PREP_TOK, B_COLS), lambda i: (i, col_blk)),
            pl.BlockSpec((halo, B_COLS), lambda i: (jnp.maximum(i * per - 1, 0), col_blk)),
            pl.BlockSpec((halo, B_COLS), lambda i: (jnp.minimum((i + 1) * per, n // halo - 1), col_blk)),
        ] + [full(a) for a in consts],
        out_specs=[out_spec] * 11,
        out_shape=[jax.ShapeDtypeStruct((n, B_WIDTH), jnp.float32)] * 11,
        compiler_params=pltpu.CompilerParams(
            dimension_semantics=("arbitrary",), vmem_limit_bytes=VMEM_LIMIT_BYTES),
        name="rwkv_prep",
    )(p, p, p, *consts)


SCAN_LANES = 2 * LANE


def _head_sums(head_ones, *terms):
    sums = []
    for x, passes in terms:
        hi = x.astype(jnp.bfloat16)
        out = jnp.dot(hi, head_ones, preferred_element_type=jnp.float32)
        if passes == 2:
            lo = (x - hi.astype(jnp.float32)).astype(jnp.bfloat16)
            out = out + jnp.dot(lo, head_ones, preferred_element_type=jnp.float32)
        sums.append(out)
    return sums


def _rwkv_scan_kernel(rf, kkf, vf, wf, kf, nbf, rb, kkb, vb, wb, kb, nbb, s0_ref, ones_ref, eye_ref,
                      yf_ref, yb_ref, sfin_ref, s_scr, *, n_seq):
    c = pl.program_id(1)

    @pl.when(c == 0)
    def _():
        s_scr[...] = s0_ref[...]

    head_ones = ones_ref[...]
    eye = eye_ref[...]
    fwd_refs = (rf, kkf, vf, wf, kf, nbf)
    bwd_refs = (rb, kkb, vb, wb, kb, nbb)
    R, KK, V, W, K, NB = range(6)

    def step(t, carry):
        t_f = pl.ds(t, 1)
        t_b = pl.ds(SCAN_CHUNK - 1 - t, 1)
        rows_f = [[a[g, t_f, :] for a in fwd_refs] for g in range(n_seq)]
        rows_b = [[a[g, t_b, :] for a in bwd_refs] for g in range(n_seq)]

        def both(which, chain):
            g, p = chain
            lanes = slice(p * LANE, (p + 1) * LANE)
            return jnp.concatenate([rows_f[g][which][:, lanes], rows_b[g][which][:, lanes]], axis=1)

        chains = [(g, p) for g in range(n_seq) for p in range(HEAD_PAIRS)]
        rows = {ch: slice(i * HEAD_DIM, (i + 1) * HEAD_DIM) for i, ch in enumerate(chains)}
        s_old = {ch: s_scr[ch[0], ch[1]] for ch in chains}
        s_kk, v_col = _head_sums(
            head_ones,
            (jnp.concatenate([s_old[ch] * both(KK, ch) for ch in chains], axis=0), 1),
            (jnp.concatenate([eye * both(V, ch) for ch in chains], axis=0), 1))
        s_new = {}
        for ch in chains:
            s = s_old[ch] * both(W, ch) + s_kk[rows[ch]] * both(NB, ch) + v_col[rows[ch]] * both(K, ch)
            s_scr[ch[0], ch[1]] = s
            s_new[ch] = s
        (y_all,) = _head_sums(
            head_ones, (jnp.concatenate([s_new[ch] * both(R, ch) for ch in chains], axis=0), 1))
        for g in range(n_seq):
            y_rows = [jnp.sum(y_all[rows[(g, p)]] * eye, axis=0, keepdims=True) for p in range(HEAD_PAIRS)]
            yf_ref[g, t_f, :] = jnp.concatenate([y[:, :LANE] for y in y_rows], axis=1)
            yb_ref[g, t_b, :] = jnp.concatenate([y[:, LANE:] for y in y_rows], axis=1)
        return carry

    lax.fori_loop(0, SCAN_CHUNK, step, 0, unroll=16)

    @pl.when(c == pl.num_programs(1) - 1)
    def _():
        sfin_ref[...] = s_scr[...]


def _rwkv_scan(flat, s0, t_len, first_seq, seq_group):
    r, kk, v, w0, w1, k0, k1, nb0, nb1 = (a.reshape(-1, t_len, B_WIDTH) for a in flat)
    n_seq = s0.shape[0]
    n_chunks = t_len // SCAN_CHUNK
    assert n_seq % seq_group == 0 and first_seq % seq_group == 0 and n_chunks * SCAN_CHUNK == t_len
    first_blk = first_seq // seq_group
    blk = (seq_group, SCAN_CHUNK, B_WIDTH)
    fwd_in = pl.BlockSpec(blk, lambda i, c: (first_blk + i, c, 0))
    bwd_in = pl.BlockSpec(blk, lambda i, c: (first_blk + i, n_chunks - 1 - c, 0))
    fwd = pl.BlockSpec(blk, lambda i, c: (i, c, 0))
    bwd = pl.BlockSpec(blk, lambda i, c: (i, n_chunks - 1 - c, 0))
    state_blk = (seq_group, HEAD_PAIRS, HEAD_DIM, SCAN_LANES)
    state = pl.BlockSpec(state_blk, lambda i, c: (i, 0, 0, 0))
    head_ones = _head_ones(SCAN_LANES).astype(jnp.bfloat16)
    eye = (lax.broadcasted_iota(jnp.int32, (HEAD_DIM, SCAN_LANES), 0)
           == lax.broadcasted_iota(jnp.int32, (HEAD_DIM, SCAN_LANES), 1) % HEAD_DIM).astype(jnp.float32)
    const = lambda a: pl.BlockSpec(a.shape, lambda i, c: (0, 0))
    y_shape = jax.ShapeDtypeStruct((n_seq, t_len, B_WIDTH), jnp.float32)
    return pl.pallas_call(
        functools.partial(_rwkv_scan_kernel, n_seq=seq_group),
        grid=(n_seq // seq_group, n_chunks),
        in_specs=[fwd_in] * 6 + [bwd_in] * 6 + [state, const(head_ones), const(eye)],
        out_specs=[fwd, bwd, state],
        out_shape=[y_shape, y_shape, jax.ShapeDtypeStruct(s0.shape, jnp.float32)],
        scratch_shapes=[pltpu.VMEM(state_blk, jnp.float32)],
        compiler_params=pltpu.CompilerParams(
            dimension_semantics=("arbitrary", "arbitrary"), vmem_limit_bytes=VMEM_LIMIT_BYTES),
        name="rwkv_scan",
    )(r, kk, v, w0, k0, nb0, r, kk, v, w1, k1, nb1, s0, head_ones, eye)


def _state_to_pairs(s):
    b = s.shape[0]
    s = s.reshape(b, 2, HEAD_PAIRS, 2, HEAD_DIM, HEAD_DIM)
    return jnp.transpose(s, (0, 2, 4, 1, 3, 5)).reshape(b, HEAD_PAIRS, HEAD_DIM, SCAN_LANES)


def _state_from_pairs(s):
    b = s.shape[0]
    s = s.reshape(b, HEAD_PAIRS, HEAD_DIM, 2, 2, HEAD_DIM)
    return jnp.transpose(s, (0, 3, 1, 4, 2, 5)).reshape(b, 2, B_HEADS, HEAD_DIM, HEAD_DIM)


def _rwkv_mixer(p, lp, state_lat):
    zeros = jnp.zeros((LORA_W, B_WIDTH), jnp.float32)
    w2_pad = jnp.concatenate([jnp.concatenate([lp["rw_w2"][0], zeros], axis=1),
                              jnp.concatenate([zeros, lp["rw_w2"][1]], axis=1)], axis=0)
    a2_pad = jnp.concatenate([jnp.concatenate([lp["rw_a2"][0], zeros], axis=1),
                              jnp.concatenate([zeros, lp["rw_a2"][1]], axis=1)], axis=0)
    row = lambda a: a.reshape(1, -1)
    prep = _rwkv_prep(p, row(lp["rw_mu"]), lp["rw_w0"], w2_pad, lp["rw_a0"], a2_pad, lp["rw_g2"],
                      row(lp["rw_kk"]), row(lp["rw_ka"]), row(lp["rw_rk"]))
    scan_in, (g, bonus) = prep[:9], prep[9:]
    s0_ctx = jnp.zeros((BATCH, HEAD_PAIRS, HEAD_DIM, SCAN_LANES), jnp.float32)
    yf_c, yb_c, s_ctx = _rwkv_scan(scan_in, s0_ctx, t_len=SEQ, first_seq=0, seq_group=4)
    yf_l, yb_l, _ = _rwkv_scan(scan_in, _state_to_pairs(state_lat), t_len=DEC_SEQ,
                               first_seq=N_CTX_TOK // DEC_SEQ, seq_group=DEC_BATCH)
    flat = lambda a: a.reshape(-1, B_WIDTH)
    post_in = (flat(yf_c), flat(yb_c), flat(yf_l), flat(yb_l), g, bonus, row(lp["rw_lnx_g"]), row(lp["rw_lnx_b"]))
    return post_in, _state_from_pairs(s_ctx)


ATT_SCALE = HEAD_DIM ** -0.5
COL0_QA = N_BRANCH * D_MODEL
COL0_KA = COL0_QA + A_HEADS * HEAD_DIM
COL0_VA = COL0_KA + A_KV * HEAD_DIM
COL0_QC = COL0_QA + A_COLS + B_COLS
COL0_KC = COL0_QC + C_HEADS * HEAD_DIM
COL0_VC = COL0_KC + C_HEADS * HEAD_DIM
A_GROUP_COLS = A_GROUP * HEAD_DIM
assert A_KV * HEAD_DIM == LANE and all(c % LANE == 0 for c in (COL0_QA, COL0_KA, COL0_VA, COL0_QC, COL0_KC, COL0_VC))
assert COL0_QA % A_GROUP_COLS == 0
GRID_ROWS = DEC_SEQ // GRID_W
NA_WIN_ROWS = min(NA_ROWS, GRID_ROWS)


def _attend(q, parts, sink=None):
    scores = []
    for k, _, bias in parts:
        s = lax.dot_general(q, k, (((1,), (1,)), ((), ())), preferred_element_type=jnp.float32)
        scores.append(s if bias is None else s + bias)
    m = functools.reduce(jnp.maximum, [jnp.max(s, axis=-1, keepdims=True) for s in scores])
    den = 0.0
    if sink is not None:
        m = jnp.maximum(m, sink)
        den = jnp.exp(sink - m)
    out = 0.0
    for s, (_, v, _) in zip(scores, parts):
        prob = jnp.exp(s - m)
        den = den + jnp.sum(prob, axis=-1, keepdims=True)
        out = out + jnp.dot(prob.astype(jnp.bfloat16), v, preferred_element_type=jnp.float32)
    return out / den


def _scaled_bf16(q):
    return (q * ATT_SCALE).astype(jnp.bfloat16)


def _kv_half(x, kv):
    return jnp.where(kv == 0, x[:, :HEAD_DIM], x[:, HEAD_DIM:])


def _ctx_gqa_kernel(q_ref, k_ref, v_ref, sink_ref, o_ref):
    kv = pl.program_id(1)
    k = _kv_half(k_ref[...], kv).astype(jnp.bfloat16)
    v = _kv_half(v_ref[...], kv).astype(jnp.bfloat16)
    outs = []
    for g in range(A_GROUP):
        q = _scaled_bf16(q_ref[:, g * HEAD_DIM:(g + 1) * HEAD_DIM])
        sink = sink_ref[pl.ds(kv * A_GROUP + g, 1), :][:, 0:1]
        outs.append(_attend(q, [(k, v, None)], sink))
    o_ref[...] = jnp.concatenate(outs, axis=1)


def _ctx_gqa(p, sink_rows):
    q0, k0, v0 = COL0_QA // A_GROUP_COLS, COL0_KA // LANE, COL0_VA // LANE
    return pl.pallas_call(
        _ctx_gqa_kernel,
        grid=(BATCH, A_KV),
        in_specs=[
            pl.BlockSpec((SEQ, A_GROUP_COLS), lambda b, kv: (b, q0 + kv)),
            pl.BlockSpec((SEQ, LANE), lambda b, kv: (b, k0)),
            pl.BlockSpec((SEQ, LANE), lambda b, kv: (b, v0)),
            pl.BlockSpec(sink_rows.shape, lambda b, kv: (0, 0)),
        ],
        out_specs=pl.BlockSpec((SEQ, A_GROUP_COLS), lambda b, kv: (b, kv)),
        out_shape=jax.ShapeDtypeStruct((N_CTX_TOK, A_HEADS * HEAD_DIM), jnp.float32),
        compiler_params=pltpu.CompilerParams(
            dimension_semantics=("arbitrary", "arbitrary"), vmem_limit_bytes=VMEM_LIMIT_BYTES),
        name="ctx_gqa",
    )(p, p, p, sink_rows)


def _ctx_mha_kernel(q_ref, k_ref, v_ref, o_ref):
    outs = []
    for j in range(2):
        lanes = slice(j * HEAD_DIM, (j + 1) * HEAD_DIM)
        outs.append(_attend(_scaled_bf16(q_ref[:, lanes]),
                            [(k_ref[:, lanes].astype(jnp.bfloat16), v_ref[:, lanes].astype(jnp.bfloat16), None)]))
    o_ref[...] = jnp.concatenate(outs, axis=1)


def _ctx_mha(p):
    q0, k0, v0 = COL0_QC // LANE, COL0_KC // LANE, COL0_VC // LANE
    return pl.pallas_call(
        _ctx_mha_kernel,
        grid=(BATCH, C_HEADS // 2),
        in_specs=[
            pl.BlockSpec((SEQ, LANE), lambda b, hp: (b, q0 + hp)),
            pl.BlockSpec((SEQ, LANE), lambda b, hp: (b, k0 + hp)),
            pl.BlockSpec((SEQ, LANE), lambda b, hp: (b, v0 + hp)),
        ],
        out_specs=pl.BlockSpec((SEQ, LANE), lambda b, hp: (b, hp)),
        out_shape=jax.ShapeDtypeStruct((N_CTX_TOK, C_HEADS * HEAD_DIM), jnp.float32),
        compiler_params=pltpu.CompilerParams(
            dimension_semantics=("arbitrary", "arbitrary"), vmem_limit_bytes=VMEM_LIMIT_BYTES),
        name="ctx_mha",
    )(p, p, p)


def _rope(x, cos, sin):
    reps = x.shape[1] // LANE
    if reps > 1:
        cos = jnp.concatenate([cos] * reps, axis=1)
        sin = jnp.concatenate([sin] * reps, axis=1)
    lane = lax.broadcasted_iota(jnp.int32, x.shape, 1)
    first = (lane % (HEAD_DIM // 2)) < (HEAD_DIM // 4)
    quarter = HEAD_DIM // 4
    pieces = [x[:, i * LANE:(i + 1) * LANE] for i in range(reps)]
    up = jnp.concatenate([pltpu.roll(piece, LANE - quarter, 1) for piece in pieces], axis=1)
    down = jnp.concatenate([pltpu.roll(piece, quarter, 1) for piece in pieces], axis=1)
    return x * cos + jnp.where(first, -up, down) * sin


def _lat_window_kernel(q_ref, kp_ref, kc_ref, kn_ref, vp_ref, vc_ref, vn_ref, ck_ref, cv_ref,
                       cosq_ref, sinq_ref, cosp_ref, sinp_ref, cosn_ref, sinn_ref, sink_ref, o_ref):
    qb = pl.program_id(1)
    kv = pl.program_id(2)
    n_blocks = pl.num_programs(1)
    row = lax.broadcasted_iota(jnp.int32, (Q_BLOCK, Q_BLOCK), 0)
    col = lax.broadcasted_iota(jnp.int32, (Q_BLOCK, Q_BLOCK), 1)
    bias_prev = jnp.where(jnp.logical_and(col >= row, qb > 0), 0.0, NEG_INF)
    bias_next = jnp.where(jnp.logical_and(col <= row, qb < n_blocks - 1), 0.0, NEG_INF)
    assert WINDOW == Q_BLOCK

    def keys(k_ref, cos_ref, sin_ref):
        return _kv_half(_rope(k_ref[...], cos_ref[...], sin_ref[...]), kv).astype(jnp.bfloat16)

    def vals(v_ref):
        return _kv_half(v_ref[...], kv).astype(jnp.bfloat16)

    parts = [
        (keys(kp_ref, cosp_ref, sinp_ref), vals(vp_ref), bias_prev),
        (keys(kc_ref, cosq_ref, sinq_ref), vals(vc_ref), None),
        (keys(kn_ref, cosn_ref, sinn_ref), vals(vn_ref), bias_next),
        (_kv_half(ck_ref[0], kv).astype(jnp.bfloat16), _kv_half(cv_ref[0], kv).astype(jnp.bfloat16), None),
    ]
    q_all = _rope(q_ref[...], cosq_ref[...], sinq_ref[...])
    outs = []
    for g in range(A_GROUP):
        sink = sink_ref[pl.ds(kv * A_GROUP + g, 1), :][:, 0:1]
        outs.append(_attend(_scaled_bf16(q_all[:, g * HEAD_DIM:(g + 1) * HEAD_DIM]), parts, sink))
    o_ref[...] = jnp.concatenate(outs, axis=1)


def _lat_window(p, cache_k, cache_v, cos, sin, sink_rows):
    n_blocks = DEC_SEQ // Q_BLOCK
    base = N_CTX_TOK // Q_BLOCK
    q0, k0, v0 = COL0_QA // A_GROUP_COLS, COL0_KA // LANE, COL0_VA // LANE
    prev = lambda b, qb, kv: base + b * n_blocks + jnp.maximum(qb - 1, 0)
    here = lambda b, qb, kv: base + b * n_blocks + qb
    nxt = lambda b, qb, kv: base + b * n_blocks + jnp.minimum(qb + 1, n_blocks - 1)
    slab = lambda rows, c0: pl.BlockSpec((Q_BLOCK, LANE), lambda b, qb, kv: (rows(b, qb, kv), c0))
    table = lambda blk: pl.BlockSpec((Q_BLOCK, LANE), lambda b, qb, kv: (blk(b, qb, kv), 0))
    cache = pl.BlockSpec((1, PAST_LEN, LANE), lambda b, qb, kv: (b, 0, 0))
    return pl.pallas_call(
        _lat_window_kernel,
        grid=(DEC_BATCH, n_blocks, A_KV),
        in_specs=[
            pl.BlockSpec((Q_BLOCK, A_GROUP_COLS), lambda b, qb, kv: (here(b, qb, kv), q0 + kv)),
            slab(prev, k0), slab(here, k0), slab(nxt, k0),
            slab(prev, v0), slab(here, v0), slab(nxt, v0),
            cache, cache,
            table(lambda b, qb, kv: qb), table(lambda b, qb, kv: qb),
            table(lambda b, qb, kv: jnp.maximum(qb - 1, 0)), table(lambda b, qb, kv: jnp.maximum(qb - 1, 0)),
            table(lambda b, qb, kv: jnp.minimum(qb + 1, n_blocks - 1)),
            table(lambda b, qb, kv: jnp.minimum(qb + 1, n_blocks - 1)),
            pl.BlockSpec(sink_rows.shape, lambda b, qb, kv: (0, 0)),
        ],
        out_specs=pl.BlockSpec((Q_BLOCK, A_GROUP_COLS), lambda b, qb, kv: (b * n_blocks + qb, kv)),
        out_shape=jax.ShapeDtypeStruct((N_LAT_TOK, A_HEADS * HEAD_DIM), jnp.float32),
        compiler_params=pltpu.CompilerParams(
            dimension_semantics=("arbitrary", "arbitrary", "arbitrary"), vmem_limit_bytes=VMEM_LIMIT_BYTES),
        name="lat_window",
    )(p, p, p, p, p, p, p, cache_k, cache_v, cos, sin, cos, sin, cos, sin, sink_rows)


N_DC = 2 * NA_COLS - 1
N_DR = 2 * NA_ROWS - 1


def _na_bias_kernel(rpb_ref, o_ref):
    h = pl.program_id(0)
    shape = (GRID_W, 2 * GRID_W)
    qc = lax.broadcasted_iota(jnp.int32, shape, 0)
    kc = lax.broadcasted_iota(jnp.int32, shape, 1) % GRID_W
    second = lax.broadcasted_iota(jnp.int32, shape, 1) >= GRID_W
    start = jnp.clip(qc - NA_COLS // 2, 0, GRID_W - NA_COLS)
    inside = jnp.logical_and(kc >= start, kc < start + NA_COLS)
    dc = jnp.clip(kc - qc, -(NA_COLS - 1), NA_COLS - 1) + NA_COLS - 1
    for dr in range(N_DR - 1):
        bias = jnp.zeros(shape, jnp.float32)
        for d in range(N_DC):
            lo = rpb_ref[(h * N_DR + dr) * N_DC + d]
            hi = rpb_ref[(h * N_DR + dr + 1) * N_DC + d]
            bias = jnp.where(dc == d, jnp.where(second, hi, lo), bias)
        o_ref[0, dr] = jnp.where(inside, bias, NEG_INF)


def _na_bias(rpb):
    return pl.pallas_call(
        _na_bias_kernel,
        grid=(C_HEADS,),
        in_specs=[pl.BlockSpec(memory_space=pltpu.SMEM)],
        out_specs=pl.BlockSpec((1, N_DR - 1, GRID_W, 2 * GRID_W), lambda h: (h, 0, 0, 0)),
        out_shape=jax.ShapeDtypeStruct((C_HEADS, N_DR - 1, GRID_W, 2 * GRID_W), jnp.float32),
        compiler_params=pltpu.CompilerParams(dimension_semantics=("arbitrary",)),
        name="na_bias",
    )(rpb.reshape(-1))


def _lat_neighbourhood_kernel(q_ref, k_ref, v_ref, ck_ref, cv_ref, bias_ref, o_ref):
    hp = pl.program_id(1)
    r = pl.program_id(2)
    first_row = jnp.clip(r - NA_WIN_ROWS // 2, 0, GRID_ROWS - NA_WIN_ROWS)
    local = pl.ds(pl.multiple_of(first_row * GRID_W, GRID_W), NA_WIN_ROWS * GRID_W)
    dr0 = first_row - r + NA_ROWS - 1
    outs = []
    for j in range(2):
        lanes = slice(j * HEAD_DIM, (j + 1) * HEAD_DIM)
        bias = jnp.concatenate([bias_ref[hp * 2 + j, dr0 + a] for a in range(0, NA_WIN_ROWS, 2)], axis=1)
        parts = [
            (k_ref[local, :][:, lanes].astype(jnp.bfloat16), v_ref[local, :][:, lanes].astype(jnp.bfloat16), bias),
            (ck_ref[0, :, lanes].astype(jnp.bfloat16), cv_ref[0, :, lanes].astype(jnp.bfloat16), None),
        ]
        outs.append(_attend(_scaled_bf16(q_ref[:, lanes]), parts))
    o_ref[...] = jnp.concatenate(outs, axis=1)


def _lat_neighbourhood(p, cache_k, cache_v, bias_tiles):
    assert NA_WIN_ROWS % 2 == 0 and N_CTX_TOK % DEC_SEQ == 0
    q0, k0, v0 = COL0_QC // LANE, COL0_KC // LANE, COL0_VC // LANE
    row_base = N_CTX_TOK // GRID_W
    seq_base = N_CTX_TOK // DEC_SEQ
    whole = lambda c0: pl.BlockSpec((DEC_SEQ, LANE), lambda b, hp, r: (seq_base + b, c0 + hp))
    cache = pl.BlockSpec((1, PAST_LEN, LANE), lambda b, hp, r: (b, 0, hp))
    return pl.pallas_call(
        _lat_neighbourhood_kernel,
        grid=(DEC_BATCH, C_HEADS // 2, GRID_ROWS),
        in_specs=[
            pl.BlockSpec((GRID_W, LANE), lambda b, hp, r: (row_base + b * GRID_ROWS + r, q0 + hp)),
            whole(k0), whole(v0), cache, cache,
            pl.BlockSpec(bias_tiles.shape, lambda b, hp, r: (0, 0, 0, 0)),
        ],
        out_specs=pl.BlockSpec((GRID_W, LANE), lambda b, hp, r: (b * GRID_ROWS + r, hp)),
        out_shape=jax.ShapeDtypeStruct((N_LAT_TOK, C_HEADS * HEAD_DIM), jnp.float32),
        compiler_params=pltpu.CompilerParams(
            dimension_semantics=("arbitrary", "arbitrary", "arbitrary"), vmem_limit_bytes=VMEM_LIMIT_BYTES),
        name="lat_neighbourhood",
    )(p, p, p, cache_k, cache_v, bias_tiles)


def _rope_tables():
    quarter = HEAD_DIM // 4
    t = jnp.arange(DEC_SEQ)
    pos = jnp.stack([t // GRID_W, t % GRID_W], axis=-1).astype(jnp.float32)
    freqs = ROPE_BASE ** (-jnp.arange(quarter, dtype=jnp.float32) / quarter)
    ang = pos[:, :, None] * freqs
    ang = jnp.concatenate([ang, ang], axis=-1).reshape(DEC_SEQ, HEAD_DIM)
    ang = jnp.concatenate([ang] * (LANE // HEAD_DIM), axis=-1)
    return jnp.cos(ang), jnp.sin(ang)


def _final_norm_kernel(x_ref, g_ref, o_ref):
    x = x_ref[...]
    o_ref[...] = x * lax.rsqrt(jnp.mean(x * x, axis=-1, keepdims=True) + RMS_EPS) * g_ref[...]


def _final_norm(x, g):
    n, d = x.shape
    return pl.pallas_call(
        _final_norm_kernel,
        grid=(n // ROW_TILE,),
        in_specs=[pl.BlockSpec((ROW_TILE, d), lambda i: (i, 0)), pl.BlockSpec((1, d), lambda i: (0, 0))],
        out_specs=pl.BlockSpec((ROW_TILE, d), lambda i: (i, 0)),
        out_shape=jax.ShapeDtypeStruct((n, d), jnp.float32),
        compiler_params=pltpu.CompilerParams(dimension_semantics=("arbitrary",)),
        name="final_norm",
    )(x, g)


def kernel(x_prompt, x_sample, cache_a_k, cache_a_v, cache_c_k, cache_c_v, state_rwkv, c, c_ctx, ln1_g, ln2_g, lnf_g, ada_w, ada_b, w_in, a_sink, a_out, rw_mu, rw_w0, rw_w2, rw_a0, rw_a2, rw_g2, rw_kk, rw_ka, rw_rk, rw_lnx_g, rw_lnx_b, rw_out, na_rpb, na_out, w_o, pe_q, pe_subkeys, pe_u, pe_v):
    bf16 = jnp.bfloat16
    x = jnp.concatenate([x_prompt.reshape(N_CTX_TOK, D_MODEL), x_sample.reshape(N_LAT_TOK, D_MODEL)], axis=0)
    cond = jnp.concatenate([c_ctx[None, :], c], axis=0)
    n_mix = A_COLS + B_COLS + C_COLS
    rope_cos, rope_sin = _rope_tables()
    list_ak, list_av, list_ck, list_cv, list_st = [], [], [], [], []
    for l in range(DEPTH):
        lp = {
            "rw_mu": rw_mu[l], "rw_w0": rw_w0[l], "rw_w2": rw_w2[l], "rw_a0": rw_a0[l],
            "rw_a2": rw_a2[l], "rw_g2": rw_g2[l], "rw_kk": rw_kk[l], "rw_ka": rw_ka[l],
            "rw_rk": rw_rk[l], "rw_lnx_g": rw_lnx_g[l], "rw_lnx_b": rw_lnx_b[l],
        }
        mod = _ada_mod(cond, ada_w[l], ada_b[l][None, :]).reshape(N_SEG, N_MOD, D_MODEL)

        w_in_l = jnp.concatenate([w_in[l][:, n_mix:], w_in[l][:, :n_mix]], axis=1).astype(bf16)
        p, _ = _norm_proj(x, ln1_g[l][None, :], mod, w_in_l, 0, 1, col_tile=IN_COLS // 3)
        sink_rows = jnp.broadcast_to(a_sink[l][:, None], (A_HEADS, LANE))
        flat_cache = lambda cache: cache[:, l].reshape(DEC_BATCH, PAST_LEN, -1)
        ya_ctx = _ctx_gqa(p, sink_rows)
        ya_lat = _lat_window(p, flat_cache(cache_a_k), flat_cache(cache_a_v), rope_cos, rope_sin, sink_rows)
        yc_ctx = _ctx_mha(p)
        yc_lat = _lat_neighbourhood(p, flat_cache(cache_c_k), flat_cache(cache_c_v), _na_bias(na_rpb[l]))
        rwkv_post_in, st = _rwkv_mixer(p, lp, state_rwkv[:, l])
        new_cache = lambda col0, heads: p[:N_CTX_TOK, col0:col0 + heads * HEAD_DIM].reshape(BATCH, SEQ, heads, HEAD_DIM)
        list_ak.append(new_cache(COL0_KA, A_KV))
        list_av.append(new_cache(COL0_VA, A_KV))
        list_ck.append(new_cache(COL0_KC, C_HEADS))
        list_cv.append(new_cache(COL0_VC, C_HEADS))
        list_st.append(st)
        x = _merge(x, ya_ctx, ya_lat, yc_ctx, yc_lat, rwkv_post_in, _head_ones(B_WIDTH), p, mod,
                   a_out[l].astype(bf16), rw_out[l].astype(bf16), na_out[l].astype(bf16), w_o[l].astype(bf16))

        q, h2 = _norm_proj(x, ln2_g[l][None, :], mod, pe_q[l].astype(bf16), 3, 4,
                           col_tile=PEER_HEADS * PEER_KEY_DIM)
        s1m, s2m, thr, c0 = _peer_route(q, pe_subkeys[l])
        x = _peer_dense(h2, x, mod, s1m, s2m, thr, c0, pe_u[l].astype(bf16), pe_v[l].T.astype(bf16))

    y = _final_norm(x, lnf_g[None, :])
    y_prompt = y[:N_CTX_TOK].reshape(BATCH, SEQ, D_MODEL)
    y_sample = y[N_CTX_TOK:].reshape(DEC_BATCH, DEC_SEQ, D_MODEL)
    return (y_prompt, y_sample, jnp.stack(list_ak, axis=1), jnp.stack(list_av, axis=1),
            jnp.stack(list_ck, axis=1), jnp.stack(list_cv, axis=1), jnp.stack(list_st, axis=1))
```

```python
import functools

import jax
import jax.numpy as jnp
from jax import lax
from jax.experimental import pallas as pl
from jax.experimental.pallas import tpu as pltpu

D_MODEL = 1024
BATCH = 16
SEQ = 256
DEPTH = 4
DEC_BATCH = 2
DEC_SEQ = 2048
PAST_LEN = 512
GRID_W = 64
HEAD_DIM = 64
A_HEADS = 8
A_KV = 2
A_GROUP = A_HEADS // A_KV
WINDOW = 128
Q_BLOCK = 128
B_HEADS = 8
B_WIDTH = B_HEADS * HEAD_DIM
LORA_W = 64
LORA_A = 64
LORA_G = 128
GN_EPS = 64e-5
C_HEADS = 8
NA_ROWS = 8
NA_COLS = 16
N_BRANCH = 3
A_COLS = (A_HEADS + 2 * A_KV) * HEAD_DIM
B_COLS = 3 * B_WIDTH + 2 * LORA_W + 2 * LORA_A + LORA_G
C_COLS = 3 * C_HEADS * HEAD_DIM
IN_COLS = A_COLS + B_COLS + C_COLS + N_BRANCH * D_MODEL
PEER_HEADS = 8
N_KEYS = 128
N_EXPERTS = N_KEYS * N_KEYS
PEER_KEY_DIM = 256
PEER_TOPK = 16
PEER_CHUNK = 128
ROPE_BASE = 10000.0
RMS_EPS = 1e-6
N_MOD = 6

N_CTX_TOK = BATCH * SEQ
N_LAT_TOK = DEC_BATCH * DEC_SEQ
N_TOK = N_CTX_TOK + N_LAT_TOK
N_SEG = 1 + DEC_BATCH

VMEM_LIMIT_BYTES = 48 * 1024 * 1024

ROW_TILE = 512
assert N_CTX_TOK % ROW_TILE == 0 and DEC_SEQ % ROW_TILE == 0


def _segment_of_row_block(i):
    n_ctx_blocks = N_CTX_TOK // ROW_TILE
    blocks_per_lat = DEC_SEQ // ROW_TILE
    return jnp.where(i < n_ctx_blocks, 0, 1 + (i - n_ctx_blocks) // blocks_per_lat)


def _ada_mod_kernel(c_ref, w_ref, b_ref, o_ref):
    c = c_ref[...]
    o_ref[...] = jnp.dot(c * jax.nn.sigmoid(c), w_ref[...], preferred_element_type=jnp.float32,
                         precision=lax.Precision.HIGHEST) + b_ref[...]


def _ada_mod(cond, w, b):
    n_seg, d = cond.shape
    n_out = w.shape[1]
    return pl.pallas_call(
        _ada_mod_kernel,
        grid=(n_out // d,),
        in_specs=[pl.BlockSpec((n_seg, d), lambda j: (0, 0)),
                  pl.BlockSpec((d, d), lambda j: (0, j)),
                  pl.BlockSpec((1, d), lambda j: (0, j))],
        out_specs=pl.BlockSpec((n_seg, d), lambda j: (0, j)),
        out_shape=jax.ShapeDtypeStruct((n_seg, n_out), jnp.float32),
        compiler_params=pltpu.CompilerParams(dimension_semantics=("arbitrary",)),
        name="ada_mod",
    )(cond, w, b)


def _norm_proj_kernel(x_ref, g_ref, mod_ref, w_ref, o_ref, h_ref, *, shift_idx, scale_idx):
    @pl.when(pl.program_id(1) == 0)
    def _():
        x = x_ref[...]
        y = x * lax.rsqrt(jnp.mean(x * x, axis=-1, keepdims=True) + RMS_EPS) * g_ref[...]
        h = y * (1.0 + mod_ref[0, scale_idx:scale_idx + 1, :]) + mod_ref[0, shift_idx:shift_idx + 1, :]
        h_ref[...] = h.astype(jnp.bfloat16)

    o_ref[...] = jnp.dot(h_ref[...], w_ref[...], preferred_element_type=jnp.float32)


def _norm_proj(x, g, mod, w_bf16, shift_idx, scale_idx, col_tile):
    n_rows, d = x.shape
    n_cols = w_bf16.shape[1]
    assert n_cols % col_tile == 0
    return pl.pallas_call(
        functools.partial(_norm_proj_kernel, shift_idx=shift_idx, scale_idx=scale_idx),
        grid=(n_rows // ROW_TILE, n_cols // col_tile),
        in_specs=[
            pl.BlockSpec((ROW_TILE, d), lambda i, j: (i, 0)),
            pl.BlockSpec((1, d), lambda i, j: (0, 0)),
            pl.BlockSpec((1, N_MOD, d), lambda i, j: (_segment_of_row_block(i), 0, 0)),
            pl.BlockSpec((d, col_tile), lambda i, j: (0, j)),
        ],
        out_specs=[pl.BlockSpec((ROW_TILE, col_tile), lambda i, j: (i, j)),
                   pl.BlockSpec((ROW_TILE, d), lambda i, j: (i, 0))],
        out_shape=[jax.ShapeDtypeStruct((n_rows, n_cols), jnp.float32),
                   jax.ShapeDtypeStruct((n_rows, d), jnp.bfloat16)],
        compiler_params=pltpu.CompilerParams(
            dimension_semantics=("arbitrary", "arbitrary"), vmem_limit_bytes=VMEM_LIMIT_BYTES),
        name="norm_proj",
    )(x, g, mod, w_bf16)


LANE = 128
ROUTE_TOK = 512
PEER_TOK = 512
PEER_EXP = 1024
NEG_INF = float("-inf")
LOG2_E = 1.4426950408889634


def _take_top(vals, k):
    rows = lax.broadcasted_iota(jnp.int32, vals.shape, 0)
    taken = []
    for _ in range(k):
        m = jnp.max(vals, axis=0, keepdims=True)
        first = jnp.min(jnp.where(vals == m, rows, vals.shape[0]), axis=0, keepdims=True)
        vals = jnp.where(rows == first, NEG_INF, vals)
        taken.append(m)
    return jnp.concatenate(taken, axis=0), vals


def _take_top_distinct(vals, k):
    taken = []
    for _ in range(k):
        m = jnp.max(vals, axis=0, keepdims=True)
        vals = jnp.where(vals == m, NEG_INF, vals)
        taken.append(m)
    removed = jnp.sum((vals == NEG_INF).astype(jnp.int32), axis=0, keepdims=True)
    return jnp.concatenate(taken, axis=0), vals, removed


def _peer_route_kernel(q_ref, sk_ref, s1m_ref, s2m_ref, thr_ref, c0_ref):
    def route(tok, take_top):
        masked, top_vals, removed = [], [], []
        for z in range(2):
            qz = q_ref[tok, z * LANE:(z + 1) * LANE]
            s = lax.dot_general(sk_ref[0, z], qz, (((1,), (1,)), ((), ())),
                                preferred_element_type=jnp.float32)
            v, rest, n = take_top(s, PEER_TOPK)
            masked.append(jnp.where(rest == NEG_INF, s, NEG_INF))
            top_vals.append(v)
            removed.append(n)
        half = PEER_TOPK // 2
        v1, v2 = top_vals
        cand = jnp.concatenate(
            [v1[0:1] + v2] + [v1[i:i + 1] + v2[:half] for i in range(1, half)] + [v1[half:] + v2[0:1]], axis=0)
        sv, _, n = take_top(cand, PEER_TOPK)
        removed.append(n)
        z_sum = jnp.sum(jnp.exp(sv - sv[0:1]), axis=0, keepdims=True)
        s1m_ref[0, :, tok] = masked[0]
        s2m_ref[0, :, tok] = masked[1]
        thr_ref[0, :, tok] = sv[PEER_TOPK - 1:PEER_TOPK]
        c0_ref[0, :, tok] = -sv[0:1] - jnp.log(z_sum)
        return jnp.max(functools.reduce(jnp.maximum, removed))

    def exact_take_top(vals, k):
        v, rest = _take_top(vals, k)
        return v, rest, jnp.full((1, vals.shape[1]), k, jnp.int32)

    toks = [slice(i * LANE, (i + 1) * LANE) for i in range(ROUTE_TOK // LANE)]
    worst = [route(tok, _take_top_distinct) for tok in toks]
    for tok, n in zip(toks, worst):
        @pl.when(n > PEER_TOPK)
        def _():
            route(tok, exact_take_top)


def _peer_route(q, subkeys):
    n = q.shape[0]
    big = pl.BlockSpec((1, N_KEYS, ROUTE_TOK), lambda i, h: (h, 0, i))
    small = pl.BlockSpec((1, 1, ROUTE_TOK), lambda i, h: (h, 0, i))
    return pl.pallas_call(
        _peer_route_kernel,
        grid=(n // ROUTE_TOK, PEER_HEADS),
        in_specs=[
            pl.BlockSpec((ROUTE_TOK, PEER_KEY_DIM), lambda i, h: (i, h)),
            pl.BlockSpec((1, 2, N_KEYS, PEER_KEY_DIM // 2), lambda i, h: (h, 0, 0, 0)),
        ],
        out_specs=[big, big, small, small],
        out_shape=[jax.ShapeDtypeStruct((PEER_HEADS, N_KEYS, n), jnp.float32)] * 2
        + [jax.ShapeDtypeStruct((PEER_HEADS, 1, n), jnp.float32)] * 2,
        compiler_params=pltpu.CompilerParams(
            dimension_semantics=("arbitrary", "arbitrary"), vmem_limit_bytes=VMEM_LIMIT_BYTES),
        name="peer_route",
    )(q, subkeys)


def _gelu_tanh(x):
    return 0.5 * x * (1.0 + jnp.tanh(0.7978845608028654 * (x + 0.044715 * (x * x * x))))


def _peer_dense_kernel(h_ref, x_ref, mod_ref, s1m_ref, s2m_ref, thr_ref, c0_ref, u_ref, vt_ref, o_ref,
                       ht_scr, s1l_scr, s2l_scr, coef_scr, acc_scr, *, gate_idx):
    e = pl.program_id(1)

    @pl.when(e == 0)
    def _():
        ht_scr[...] = h_ref[...].astype(jnp.float32).T.astype(jnp.bfloat16)
        s1l_scr[...] = s1m_ref[...] * LOG2_E
        s2l_scr[...] = (s2m_ref[...] + c0_ref[...]) * LOG2_E
        acc_scr[...] = jnp.zeros_like(acc_scr)

    act = jnp.dot(u_ref[...], ht_scr[...], preferred_element_type=jnp.float32)
    a_per_step = PEER_EXP // N_KEYS
    a_rows = pl.ds(pl.multiple_of(e * a_per_step, a_per_step), a_per_step)
    for t in range(PEER_TOK // LANE):
        tok = slice(t * LANE, (t + 1) * LANE)
        s1_all = [s1m_ref[hh, a_rows, tok] for hh in range(PEER_HEADS)]
        s1l_all = [s1l_scr[hh, a_rows, tok] for hh in range(PEER_HEADS)]
        for a in range(a_per_step):
            rows = slice(a * N_KEYS, (a + 1) * N_KEYS)
            w = jnp.zeros((N_KEYS, LANE), jnp.float32)
            for hh in range(PEER_HEADS):
                s1 = s1_all[hh][a:a + 1]
                picked = (s1 + s2m_ref[hh, :, tok]) >= thr_ref[hh, :, tok]
                w = w + jnp.where(picked, jnp.exp2(s1l_all[hh][a:a + 1] + s2l_scr[hh, :, tok]), 0.0)
            coef_scr[rows, tok] = (w * _gelu_tanh(act[rows, tok])).astype(jnp.bfloat16)
    acc_scr[...] += jnp.dot(vt_ref[...], coef_scr[...], preferred_element_type=jnp.float32)

    @pl.when(e == pl.num_programs(1) - 1)
    def _():
        o_ref[...] = x_ref[...] + mod_ref[0, gate_idx:gate_idx + 1, :] * acc_scr[...].T


def _peer_dense(h, x, mod, s1m, s2m, thr, c0, u_bf16, vt_bf16):
    n, d = x.shape
    assert PEER_TOK == ROW_TILE
    tok_row = lambda i, e: (i, 0)
    big = pl.BlockSpec((PEER_HEADS, N_KEYS, PEER_TOK), lambda i, e: (0, 0, i))
    small = pl.BlockSpec((PEER_HEADS, 1, PEER_TOK), lambda i, e: (0, 0, i))
    return pl.pallas_call(
        functools.partial(_peer_dense_kernel, gate_idx=5),
        grid=(n // PEER_TOK, N_EXPERTS // PEER_EXP),
        in_specs=[
            pl.BlockSpec((PEER_TOK, d), tok_row),
            pl.BlockSpec((PEER_TOK, d), tok_row),
            pl.BlockSpec((1, N_MOD, d), lambda i, e: (_segment_of_row_block(i), 0, 0)),
            big, big, small, small,
            pl.BlockSpec((PEER_EXP, d), lambda i, e: (e, 0)),
            pl.BlockSpec((d, PEER_EXP), lambda i, e: (0, e)),
        ],
        out_specs=pl.BlockSpec((PEER_TOK, d), tok_row),
        out_shape=jax.ShapeDtypeStruct((n, d), jnp.float32),
        scratch_shapes=[
            pltpu.VMEM((d, PEER_TOK), jnp.bfloat16),
            pltpu.VMEM((PEER_HEADS, N_KEYS, PEER_TOK), jnp.float32),
            pltpu.VMEM((PEER_HEADS, N_KEYS, PEER_TOK), jnp.float32),
            pltpu.VMEM((PEER_EXP, PEER_TOK), jnp.bfloat16),
            pltpu.VMEM((d, PEER_TOK), jnp.float32),
        ],
        compiler_params=pltpu.CompilerParams(
            dimension_semantics=("arbitrary", "arbitrary"), vmem_limit_bytes=VMEM_LIMIT_BYTES),
        name="peer_dense",
    )(h, x, mod, s1m, s2m, thr, c0, u_bf16, vt_bf16)


def _merge_kernel(x_ref, yac_ref, yal_ref, ycc_ref, ycl_ref,
                  sfc_ref, sbc_ref, sfl_ref, sbl_ref, g_ref, bonus_ref, lng_ref, lnb_ref, ones_ref,
                  ga_ref, gb_ref, gc_ref, mod_ref, wa_ref, wb_ref, wc_ref, wo_ref, o_ref, *, gate_idx):
    is_ctx = pl.program_id(0) < N_CTX_TOK // ROW_TILE

    def branch(y, gate_ref, w_ref):
        proj = jnp.dot(y.astype(jnp.bfloat16), w_ref[...], preferred_element_type=jnp.float32)
        return jax.nn.sigmoid(gate_ref[...]) * proj

    def head_mean(v):
        return jnp.dot(v, ones_ref[...], preferred_element_type=jnp.float32,
                       precision=lax.Precision.HIGHEST) * (1.0 / HEAD_DIM)

    y = jnp.where(is_ctx, sfc_ref[...] + sbc_ref[...], sfl_ref[...] + sbl_ref[...])
    dev = y - head_mean(y)
    y = dev * lax.rsqrt(head_mean(dev * dev) + GN_EPS) * lng_ref[...] + lnb_ref[...]
    yb = (y + bonus_ref[...]) * g_ref[...]

    ya = jnp.where(is_ctx, yac_ref[...], yal_ref[...])
    yc = jnp.where(is_ctx, ycc_ref[...], ycl_ref[...])
    merged = branch(ya, ga_ref, wa_ref) + branch(yb, gb_ref, wb_ref) + branch(yc, gc_ref, wc_ref)
    out = jnp.dot(merged.astype(jnp.bfloat16), wo_ref[...], preferred_element_type=jnp.float32)
    o_ref[...] = x_ref[...] + mod_ref[0, gate_idx:gate_idx + 1, :] * out


def _merge(x, ya_ctx, ya_lat, yc_ctx, yc_lat, rwkv_post_in, head_ones, p, mod, wa, wb, wc, wo):
    n_rows, d = x.shape
    n_ctx = N_CTX_TOK // ROW_TILE
    n_lat = N_LAT_TOK // ROW_TILE
    scan_fc, scan_bc, scan_fl, scan_bl, g, bonus, lnx_g, lnx_b = rwkv_post_in
    row = lambda i: (i, 0)
    const = lambda i: (0, 0)
    width = g.shape[1]
    y_spec = pl.BlockSpec((ROW_TILE, width), row)
    ctx_spec = pl.BlockSpec((ROW_TILE, width), lambda i: (jnp.minimum(i, n_ctx - 1), 0))
    lat_spec = pl.BlockSpec((ROW_TILE, width), lambda i: (jnp.clip(i - n_ctx, 0, n_lat - 1), 0))
    vec = pl.BlockSpec((1, width), const)
    w_spec = pl.BlockSpec(wa.shape, const)
    return pl.pallas_call(
        functools.partial(_merge_kernel, gate_idx=2),
        grid=(n_rows // ROW_TILE,),
        in_specs=[
            pl.BlockSpec((ROW_TILE, d), row), ctx_spec, lat_spec, ctx_spec, lat_spec,
            ctx_spec, ctx_spec, lat_spec, lat_spec, y_spec, y_spec, vec, vec, pl.BlockSpec(head_ones.shape, const),
            pl.BlockSpec((ROW_TILE, d), lambda i: (i, 0)),
            pl.BlockSpec((ROW_TILE, d), lambda i: (i, 1)),
            pl.BlockSpec((ROW_TILE, d), lambda i: (i, 2)),
            pl.BlockSpec((1, N_MOD, d), lambda i: (_segment_of_row_block(i), 0, 0)),
            w_spec, w_spec, w_spec, pl.BlockSpec(wo.shape, const),
        ],
        out_specs=pl.BlockSpec((ROW_TILE, d), row),
        out_shape=jax.ShapeDtypeStruct((n_rows, d), jnp.float32),
        compiler_params=pltpu.CompilerParams(
            dimension_semantics=("arbitrary",), vmem_limit_bytes=VMEM_LIMIT_BYTES),
        name="merge_out_proj",
    )(x, ya_ctx, ya_lat, yc_ctx, yc_lat, scan_fc, scan_bc, scan_fl, scan_bl, g, bonus, lnx_g, lnx_b, head_ones,
      p, p, p, mod, wa, wb, wc, wo)


HEAD_PAIRS = B_HEADS // 2
PREP_TOK = 256
SCAN_CHUNK = 128
assert PREP_TOK == SEQ and DEC_SEQ % PREP_TOK == 0 and 2 * HEAD_DIM == LANE
_HI = lax.Precision.HIGHEST


def _head_ones(n):
    r = lax.broadcasted_iota(jnp.int32, (n, n), 0) // HEAD_DIM
    c = lax.broadcasted_iota(jnp.int32, (n, n), 1) // HEAD_DIM
    return (r == c).astype(jnp.float32)


def _softplus(y):
    return jnp.maximum(y, 0.0) + jnp.log(1.0 + jnp.exp(-jnp.abs(y)))


def _rwkv_prep_kernel(pb_ref, prev_ref, next_ref, mu_ref, w0_ref, w2_ref, a0_ref, a2_ref, g2_ref,
                      kks_ref, ka_ref, rk_ref, ones_ref,
                      r_o, kk_o, v_o, w0_o, w1_o, k0_o, k1_o, nb0_o, nb1_o, g_o, bonus_o):
    i = pl.program_id(0)
    n_ctx = N_CTX_TOK // PREP_TOK
    per_lat = DEC_SEQ // PREP_TOK
    j = (i - n_ctx) % per_lat
    starts_seq = jnp.logical_or(i < n_ctx, j == 0)
    ends_seq = jnp.logical_or(i < n_ctx, j == per_lat - 1)

    pb = pb_ref[...]
    rows = lax.broadcasted_iota(jnp.int32, (PREP_TOK, 1), 0)
    before = jnp.where(starts_seq, 0.0, prev_ref[7:8, :])
    after = jnp.where(ends_seq, 0.0, next_ref[0:1, :])
    prev = jnp.where(rows == 0, before, pltpu.roll(pb, 1, 0))
    nxt = jnp.where(rows == PREP_TOK - 1, after, pltpu.roll(pb, PREP_TOK - 1, 0))
    xb = pb + mu_ref[...] * (0.5 * (prev + nxt) - pb)

    w = B_WIDTH
    r, k, v = xb[:, 0:w], xb[:, w:2 * w], xb[:, 2 * w:3 * w]
    wd = xb[:, 3 * w:3 * w + 2 * LORA_W]
    ad = xb[:, 3 * w + 2 * LORA_W:3 * w + 2 * LORA_W + 2 * LORA_A]
    gd = xb[:, 3 * w + 2 * LORA_W + 2 * LORA_A:]
    ones = ones_ref[...]

    def head_sum(x):
        return jnp.dot(x, ones, preferred_element_type=jnp.float32, precision=_HI)

    lw = jnp.dot(jnp.tanh(wd), w2_ref[...], preferred_element_type=jnp.float32, precision=_HI)
    la = jnp.dot(ad, a2_ref[...], preferred_element_type=jnp.float32, precision=_HI)
    kk = k * kks_ref[...]
    kk = kk * lax.rsqrt(head_sum(kk * kk) + 1e-12)
    kd_sum = jnp.zeros_like(k)
    for z, (w_o, k_o, nb_o) in enumerate(((w0_o, k0_o, nb0_o), (w1_o, k1_o, nb1_o))):
        wlog = -_softplus(-(w0_ref[z:z + 1, :] + lw[:, z * w:(z + 1) * w])) - 0.5
        a = jax.nn.sigmoid(a0_ref[z:z + 1, :] + la[:, z * w:(z + 1) * w])
        kd = k * (1.0 + (a - 1.0) * ka_ref[...])
        kd_sum = kd_sum + kd
        w_o[...] = jnp.exp(-jnp.exp(wlog))
        k_o[...] = kd
        nb_o[...] = -(kk * a)
    r_o[...] = r
    kk_o[...] = kk
    v_o[...] = v
    g_o[...] = jnp.dot(jax.nn.sigmoid(gd), g2_ref[...], preferred_element_type=jnp.float32, precision=_HI)
    bonus_o[...] = head_sum(r * kd_sum * rk_ref[...]) * v


def _rwkv_prep(p, mu, w0, w2_pad, a0, a2_pad, g2, kk_scale, ka, rk):
    n = p.shape[0]
    col_blk = (N_BRANCH * D_MODEL + A_COLS) // B_COLS
    assert col_blk * B_COLS == N_BRANCH * D_MODEL + A_COLS
    halo = 8
    per = PREP_TOK // halo
    full = lambda a: pl.BlockSpec(a.shape, lambda i: (0,) * a.ndim)
    ones = _head_ones(B_WIDTH)
    consts = (mu, w0, w2_pad, a0, a2_pad, g2, kk_scale, ka, rk, ones)
    out_spec = pl.BlockSpec((PREP_TOK, B_WIDTH), lambda i: (i, 0))
    return pl.pallas_call(
        _rwkv_prep_kernel,
        grid=(n // PREP_TOK,),
        in_specs=[
            pl.BlockSpec((PREP_TOK, B_COLS), lambda i: (i, col_blk)),
            pl.BlockSpec((halo, B_COLS), lambda i: (jnp.maximum(i * per - 1, 0), col_blk)),
            pl.BlockSpec((halo, B_COLS), lambda i: (jnp.minimum((i + 1) * per, n // halo - 1), col_blk)),
        ] + [full(a) for a in consts],
        out_specs=[out_spec] * 11,
        out_shape=[jax.ShapeDtypeStruct((n, B_WIDTH), jnp.float32)] * 11,
        compiler_params=pltpu.CompilerParams(
            dimension_semantics=("arbitrary",), vmem_limit_bytes=VMEM_LIMIT_BYTES),
        name="rwkv_prep",
    )(p, p, p, *consts)


SCAN_LANES = 2 * LANE


def _head_sums(head_ones, *terms):
    sums = []
    for x, passes in terms:
        hi = x.astype(jnp.bfloat16)
        out = jnp.dot(hi, head_ones, preferred_element_type=jnp.float32)
        if passes == 2:
            lo = (x - hi.astype(jnp.float32)).astype(jnp.bfloat16)
            out = out + jnp.dot(lo, head_ones, preferred_element_type=jnp.float32)
        sums.append(out)
    return sums


def _rwkv_scan_kernel(rf, kkf, vf, wf, kf, nbf, rb, kkb, vb, wb, kb, nbb, s0_ref, ones_ref, eye_ref,
                      yf_ref, yb_ref, sfin_ref, s_scr, *, n_seq):
    c = pl.program_id(1)

    @pl.when(c == 0)
    def _():
        s_scr[...] = s0_ref[...]

    head_ones = ones_ref[...]
    eye = eye_ref[...]
    fwd_refs = (rf, kkf, vf, wf, kf, nbf)
    bwd_refs = (rb, kkb, vb, wb, kb, nbb)
    R, KK, V, W, K, NB = range(6)

    def step(t, carry):
        t_f = pl.ds(t, 1)
        t_b = pl.ds(SCAN_CHUNK - 1 - t, 1)
        rows_f = [[a[g, t_f, :] for a in fwd_refs] for g in range(n_seq)]
        rows_b = [[a[g, t_b, :] for a in bwd_refs] for g in range(n_seq)]

        def both(which, chain):
            g, p = chain
            lanes = slice(p * LANE, (p + 1) * LANE)
            return jnp.concatenate([rows_f[g][which][:, lanes], rows_b[g][which][:, lanes]], axis=1)

        chains = [(g, p) for g in range(n_seq) for p in range(HEAD_PAIRS)]
        rows = {ch: slice(i * HEAD_DIM, (i + 1) * HEAD_DIM) for i, ch in enumerate(chains)}
        s_old = {ch: s_scr[ch[0], ch[1]] for ch in chains}
        s_kk, v_col = _head_sums(
            head_ones,
            (jnp.concatenate([s_old[ch] * both(KK, ch) for ch in chains], axis=0), 1),
            (jnp.concatenate([eye * both(V, ch) for ch in chains], axis=0), 1))
        s_new = {}
        for ch in chains:
            s = s_old[ch] * both(W, ch) + s_kk[rows[ch]] * both(NB, ch) + v_col[rows[ch]] * both(K, ch)
            s_scr[ch[0], ch[1]] = s
            s_new[ch] = s
        (y_all,) = _head_sums(
            head_ones, (jnp.concatenate([s_new[ch] * both(R, ch) for ch in chains], axis=0), 1))
        for g in range(n_seq):
            y_rows = [jnp.sum(y_all[rows[(g, p)]] * eye, axis=0, keepdims=True) for p in range(HEAD_PAIRS)]
            yf_ref[g, t_f, :] = jnp.concatenate([y[:, :LANE] for y in y_rows], axis=1)
            yb_ref[g, t_b, :] = jnp.concatenate([y[:, LANE:] for y in y_rows], axis=1)
        return carry

    lax.fori_loop(0, SCAN_CHUNK, step, 0, unroll=16)

    @pl.when(c == pl.num_programs(1) - 1)
    def _():
        sfin_ref[...] = s_scr[...]


def _rwkv_scan(flat, s0, t_len, first_seq, seq_group):
    r, kk, v, w0, w1, k0, k1, nb0, nb1 = (a.reshape(-1, t_len, B_WIDTH) for a in flat)
    n_seq = s0.shape[0]
    n_chunks = t_len // SCAN_CHUNK
    assert n_seq % seq_group == 0 and first_seq % seq_group == 0 and n_chunks * SCAN_CHUNK == t_len
    first_blk = first_seq // seq_group
    blk = (seq_group, SCAN_CHUNK, B_WIDTH)
    fwd_in = pl.BlockSpec(blk, lambda i, c: (first_blk + i, c, 0))
    bwd_in = pl.BlockSpec(blk, lambda i, c: (first_blk + i, n_chunks - 1 - c, 0))
    fwd = pl.BlockSpec(blk, lambda i, c: (i, c, 0))
    bwd = pl.BlockSpec(blk, lambda i, c: (i, n_chunks - 1 - c, 0))
    state_blk = (seq_group, HEAD_PAIRS, HEAD_DIM, SCAN_LANES)
    state = pl.BlockSpec(state_blk, lambda i, c: (i, 0, 0, 0))
    head_ones = _head_ones(SCAN_LANES).astype(jnp.bfloat16)
    eye = (lax.broadcasted_iota(jnp.int32, (HEAD_DIM, SCAN_LANES), 0)
           == lax.broadcasted_iota(jnp.int32, (HEAD_DIM, SCAN_LANES), 1) % HEAD_DIM).astype(jnp.float32)
    const = lambda a: pl.BlockSpec(a.shape, lambda i, c: (0, 0))
    y_shape = jax.ShapeDtypeStruct((n_seq, t_len, B_WIDTH), jnp.float32)
    return pl.pallas_call(
        functools.partial(_rwkv_scan_kernel, n_seq=seq_group),
        grid=(n_seq // seq_group, n_chunks),
        in_specs=[fwd_in] * 6 + [bwd_in] * 6 + [state, const(head_ones), const(eye)],
        out_specs=[fwd, bwd, state],
        out_shape=[y_shape, y_shape, jax.ShapeDtypeStruct(s0.shape, jnp.float32)],
        scratch_shapes=[pltpu.VMEM(state_blk, jnp.float32)],
        compiler_params=pltpu.CompilerParams(
            dimension_semantics=("arbitrary", "arbitrary"), vmem_limit_bytes=VMEM_LIMIT_BYTES),
        name="rwkv_scan",
    )(r, kk, v, w0, k0, nb0, r, kk, v, w1, k1, nb1, s0, head_ones, eye)


def _state_to_pairs(s):
    b = s.shape[0]
    s = s.reshape(b, 2, HEAD_PAIRS, 2, HEAD_DIM, HEAD_DIM)
    return jnp.transpose(s, (0, 2, 4, 1, 3, 5)).reshape(b, HEAD_PAIRS, HEAD_DIM, SCAN_LANES)


def _state_from_pairs(s):
    b = s.shape[0]
    s = s.reshape(b, HEAD_PAIRS, HEAD_DIM, 2, 2, HEAD_DIM)
    return jnp.transpose(s, (0, 3, 1, 4, 2, 5)).reshape(b, 2, B_HEADS, HEAD_DIM, HEAD_DIM)


def _rwkv_mixer(p, lp, state_lat):
    zeros = jnp.zeros((LORA_W, B_WIDTH), jnp.float32)
    w2_pad = jnp.concatenate([jnp.concatenate([lp["rw_w2"][0], zeros], axis=1),
                              jnp.concatenate([zeros, lp["rw_w2"][1]], axis=1)], axis=0)
    a2_pad = jnp.concatenate([jnp.concatenate([lp["rw_a2"][0], zeros], axis=1),
                              jnp.concatenate([zeros, lp["rw_a2"][1]], axis=1)], axis=0)
    row = lambda a: a.reshape(1, -1)
    prep = _rwkv_prep(p, row(lp["rw_mu"]), lp["rw_w0"], w2_pad, lp["rw_a0"], a2_pad, lp["rw_g2"],
                      row(lp["rw_kk"]), row(lp["rw_ka"]), row(lp["rw_rk"]))
    scan_in, (g, bonus) = prep[:9], prep[9:]
    s0_ctx = jnp.zeros((BATCH, HEAD_PAIRS, HEAD_DIM, SCAN_LANES), jnp.float32)
    yf_c, yb_c, s_ctx = _rwkv_scan(scan_in, s0_ctx, t_len=SEQ, first_seq=0, seq_group=4)
    yf_l, yb_l, _ = _rwkv_scan(scan_in, _state_to_pairs(state_lat), t_len=DEC_SEQ,
                               first_seq=N_CTX_TOK // DEC_SEQ, seq_group=DEC_BATCH)
    flat = lambda a: a.reshape(-1, B_WIDTH)
    post_in = (flat(yf_c), flat(yb_c), flat(yf_l), flat(yb_l), g, bonus, row(lp["rw_lnx_g"]), row(lp["rw_lnx_b"]))
    return post_in, _state_from_pairs(s_ctx)


ATT_SCALE = HEAD_DIM ** -0.5
COL0_QA = N_BRANCH * D_MODEL
COL0_KA = COL0_QA + A_HEADS * HEAD_DIM
COL0_VA = COL0_KA + A_KV * HEAD_DIM
COL0_QC = COL0_QA + A_COLS + B_COLS
COL0_KC = COL0_QC + C_HEADS * HEAD_DIM
COL0_VC = COL0_KC + C_HEADS * HEAD_DIM
A_GROUP_COLS = A_GROUP * HEAD_DIM
assert A_KV * HEAD_DIM == LANE and all(c % LANE == 0 for c in (COL0_QA, COL0_KA, COL0_VA, COL0_QC, COL0_KC, COL0_VC))
assert COL0_QA % A_GROUP_COLS == 0
GRID_ROWS = DEC_SEQ // GRID_W
NA_WIN_ROWS = min(NA_ROWS, GRID_ROWS)
NA_Q_ROWS = 2


def _attend(q, parts, sink=None):
    scores = []
    for k, _, bias in parts:
        s = lax.dot_general(q, k, (((1,), (1,)), ((), ())), preferred_element_type=jnp.float32)
        scores.append(s if bias is None else s + bias)
    m = functools.reduce(jnp.maximum, [jnp.max(s, axis=-1, keepdims=True) for s in scores])
    den = 0.0
    if sink is not None:
        m = jnp.maximum(m, sink)
        den = jnp.exp(sink - m)
    out = 0.0
    for s, (_, v, _) in zip(scores, parts):
        prob = jnp.exp(s - m)
        den = den + jnp.sum(prob, axis=-1, keepdims=True)
        out = out + jnp.dot(prob.astype(jnp.bfloat16), v, preferred_element_type=jnp.float32)
    return out / den


def _scaled_bf16(q):
    return (q * ATT_SCALE).astype(jnp.bfloat16)


def _kv_half(x, kv):
    return jnp.where(kv == 0, x[:, :HEAD_DIM], x[:, HEAD_DIM:])


def _ctx_gqa_kernel(q_ref, k_ref, v_ref, sink_ref, o_ref):
    kv = pl.program_id(1)
    k = _kv_half(k_ref[...], kv).astype(jnp.bfloat16)
    v = _kv_half(v_ref[...], kv).astype(jnp.bfloat16)
    outs = []
    for g in range(A_GROUP):
        q = _scaled_bf16(q_ref[:, g * HEAD_DIM:(g + 1) * HEAD_DIM])
        sink = sink_ref[pl.ds(kv * A_GROUP + g, 1), :][:, 0:1]
        outs.append(_attend(q, [(k, v, None)], sink))
    o_ref[...] = jnp.concatenate(outs, axis=1)


def _ctx_gqa(p, sink_rows):
    q0, k0, v0 = COL0_QA // A_GROUP_COLS, COL0_KA // LANE, COL0_VA // LANE
    return pl.pallas_call(
        _ctx_gqa_kernel,
        grid=(BATCH, A_KV),
        in_specs=[
            pl.BlockSpec((SEQ, A_GROUP_COLS), lambda b, kv: (b, q0 + kv)),
            pl.BlockSpec((SEQ, LANE), lambda b, kv: (b, k0)),
            pl.BlockSpec((SEQ, LANE), lambda b, kv: (b, v0)),
            pl.BlockSpec(sink_rows.shape, lambda b, kv: (0, 0)),
        ],
        out_specs=pl.BlockSpec((SEQ, A_GROUP_COLS), lambda b, kv: (b, kv)),
        out_shape=jax.ShapeDtypeStruct((N_CTX_TOK, A_HEADS * HEAD_DIM), jnp.float32),
        compiler_params=pltpu.CompilerParams(
            dimension_semantics=("arbitrary", "arbitrary"), vmem_limit_bytes=VMEM_LIMIT_BYTES),
        name="ctx_gqa",
    )(p, p, p, sink_rows)


def _ctx_mha_kernel(q_ref, k_ref, v_ref, o_ref):
    outs = []
    for j in range(2):
        lanes = slice(j * HEAD_DIM, (j + 1) * HEAD_DIM)
        outs.append(_attend(_scaled_bf16(q_ref[:, lanes]),
                            [(k_ref[:, lanes].astype(jnp.bfloat16), v_ref[:, lanes].astype(jnp.bfloat16), None)]))
    o_ref[...] = jnp.concatenate(outs, axis=1)


def _ctx_mha(p):
    q0, k0, v0 = COL0_QC // LANE, COL0_KC // LANE, COL0_VC // LANE
    return pl.pallas_call(
        _ctx_mha_kernel,
        grid=(BATCH, C_HEADS // 2),
        in_specs=[
            pl.BlockSpec((SEQ, LANE), lambda b, hp: (b, q0 + hp)),
            pl.BlockSpec((SEQ, LANE), lambda b, hp: (b, k0 + hp)),
            pl.BlockSpec((SEQ, LANE), lambda b, hp: (b, v0 + hp)),
        ],
        out_specs=pl.BlockSpec((SEQ, LANE), lambda b, hp: (b, hp)),
        out_shape=jax.ShapeDtypeStruct((N_CTX_TOK, C_HEADS * HEAD_DIM), jnp.float32),
        compiler_params=pltpu.CompilerParams(
            dimension_semantics=("arbitrary", "arbitrary"), vmem_limit_bytes=VMEM_LIMIT_BYTES),
        name="ctx_mha",
    )(p, p, p)


def _rope(x, cos, sin):
    reps = x.shape[1] // LANE
    if reps > 1:
        cos = jnp.concatenate([cos] * reps, axis=1)
        sin = jnp.concatenate([sin] * reps, axis=1)
    lane = lax.broadcasted_iota(jnp.int32, x.shape, 1)
    first = (lane % (HEAD_DIM // 2)) < (HEAD_DIM // 4)
    quarter = HEAD_DIM // 4
    pieces = [x[:, i * LANE:(i + 1) * LANE] for i in range(reps)]
    up = jnp.concatenate([pltpu.roll(piece, LANE - quarter, 1) for piece in pieces], axis=1)
    down = jnp.concatenate([pltpu.roll(piece, quarter, 1) for piece in pieces], axis=1)
    return x * cos + jnp.where(first, -up, down) * sin


def _lat_window_kernel(q_ref, kp_ref, kc_ref, kn_ref, vp_ref, vc_ref, vn_ref, ck_ref, cv_ref,
                       cosq_ref, sinq_ref, cosp_ref, sinp_ref, cosn_ref, sinn_ref, sink_ref, o_ref):
    qb = pl.program_id(1)
    kv = pl.program_id(2)
    n_blocks = pl.num_programs(1)
    row = lax.broadcasted_iota(jnp.int32, (Q_BLOCK, Q_BLOCK), 0)
    col = lax.broadcasted_iota(jnp.int32, (Q_BLOCK, Q_BLOCK), 1)
    bias_prev = jnp.where(jnp.logical_and(col >= row, qb > 0), 0.0, NEG_INF)
    bias_next = jnp.where(jnp.logical_and(col <= row, qb < n_blocks - 1), 0.0, NEG_INF)
    assert WINDOW == Q_BLOCK

    def keys(k_ref, cos_ref, sin_ref):
        return _kv_half(_rope(k_ref[...], cos_ref[...], sin_ref[...]), kv).astype(jnp.bfloat16)

    def vals(v_ref):
        return _kv_half(v_ref[...], kv).astype(jnp.bfloat16)

    parts = [
        (keys(kp_ref, cosp_ref, sinp_ref), vals(vp_ref), bias_prev),
        (keys(kc_ref, cosq_ref, sinq_ref), vals(vc_ref), None),
        (keys(kn_ref, cosn_ref, sinn_ref), vals(vn_ref), bias_next),
        (_kv_half(ck_ref[0], kv).astype(jnp.bfloat16), _kv_half(cv_ref[0], kv).astype(jnp.bfloat16), None),
    ]
    q_all = _rope(q_ref[...], cosq_ref[...], sinq_ref[...])
    outs = []
    for g in range(A_GROUP):
        sink = sink_ref[pl.ds(kv * A_GROUP + g, 1), :][:, 0:1]
        outs.append(_attend(_scaled_bf16(q_all[:, g * HEAD_DIM:(g + 1) * HEAD_DIM]), parts, sink))
    o_ref[...] = jnp.concatenate(outs, axis=1)


def _lat_window(p, cache_k, cache_v, cos, sin, sink_rows):
    n_blocks = DEC_SEQ // Q_BLOCK
    base = N_CTX_TOK // Q_BLOCK
    q0, k0, v0 = COL0_QA // A_GROUP_COLS, COL0_KA // LANE, COL0_VA // LANE
    prev = lambda b, qb, kv: base + b * n_blocks + jnp.maximum(qb - 1, 0)
    here = lambda b, qb, kv: base + b * n_blocks + qb
    nxt = lambda b, qb, kv: base + b * n_blocks + jnp.minimum(qb + 1, n_blocks - 1)
    slab = lambda rows, c0: pl.BlockSpec((Q_BLOCK, LANE), lambda b, qb, kv: (rows(b, qb, kv), c0))
    table = lambda blk: pl.BlockSpec((Q_BLOCK, LANE), lambda b, qb, kv: (blk(b, qb, kv), 0))
    cache = pl.BlockSpec((1, PAST_LEN, LANE), lambda b, qb, kv: (b, 0, 0))
    return pl.pallas_call(
        _lat_window_kernel,
        grid=(DEC_BATCH, n_blocks, A_KV),
        in_specs=[
            pl.BlockSpec((Q_BLOCK, A_GROUP_COLS), lambda b, qb, kv: (here(b, qb, kv), q0 + kv)),
            slab(prev, k0), slab(here, k0), slab(nxt, k0),
            slab(prev, v0), slab(here, v0), slab(nxt, v0),
            cache, cache,
            table(lambda b, qb, kv: qb), table(lambda b, qb, kv: qb),
            table(lambda b, qb, kv: jnp.maximum(qb - 1, 0)), table(lambda b, qb, kv: jnp.maximum(qb - 1, 0)),
            table(lambda b, qb, kv: jnp.minimum(qb + 1, n_blocks - 1)),
            table(lambda b, qb, kv: jnp.minimum(qb + 1, n_blocks - 1)),
            pl.BlockSpec(sink_rows.shape, lambda b, qb, kv: (0, 0)),
        ],
        out_specs=pl.BlockSpec((Q_BLOCK, A_GROUP_COLS), lambda b, qb, kv: (b * n_blocks + qb, kv)),
        out_shape=jax.ShapeDtypeStruct((N_LAT_TOK, A_HEADS * HEAD_DIM), jnp.float32),
        compiler_params=pltpu.CompilerParams(
            dimension_semantics=("arbitrary", "arbitrary", "arbitrary"), vmem_limit_bytes=VMEM_LIMIT_BYTES),
        name="lat_window",
    )(p, p, p, p, p, p, p, cache_k, cache_v, cos, sin, cos, sin, cos, sin, sink_rows)


N_DC = 2 * NA_COLS - 1
N_DR = 2 * NA_ROWS - 1


def _na_bias_kernel(rpb_ref, o_ref):
    h = pl.program_id(0)
    shape = (GRID_W, 2 * GRID_W)
    qc = lax.broadcasted_iota(jnp.int32, shape, 0)
    kc = lax.broadcasted_iota(jnp.int32, shape, 1) % GRID_W
    second = lax.broadcasted_iota(jnp.int32, shape, 1) >= GRID_W
    start = jnp.clip(qc - NA_COLS // 2, 0, GRID_W - NA_COLS)
    inside = jnp.logical_and(kc >= start, kc < start + NA_COLS)
    dc = jnp.clip(kc - qc, -(NA_COLS - 1), NA_COLS - 1) + NA_COLS - 1
    for dr in range(N_DR - 1):
        bias = jnp.zeros(shape, jnp.float32)
        for d in range(N_DC):
            lo = rpb_ref[(h * N_DR + dr) * N_DC + d]
            hi = rpb_ref[(h * N_DR + dr + 1) * N_DC + d]
            bias = jnp.where(dc == d, jnp.where(second, hi, lo), bias)
        o_ref[0, dr] = jnp.where(inside, bias, NEG_INF)


def _na_bias(rpb):
    return pl.pallas_call(
        _na_bias_kernel,
        grid=(C_HEADS,),
        in_specs=[pl.BlockSpec(memory_space=pltpu.SMEM)],
        out_specs=pl.BlockSpec((1, N_DR - 1, GRID_W, 2 * GRID_W), lambda h: (h, 0, 0, 0)),
        out_shape=jax.ShapeDtypeStruct((C_HEADS, N_DR - 1, GRID_W, 2 * GRID_W), jnp.float32),
        compiler_params=pltpu.CompilerParams(dimension_semantics=("arbitrary",)),
        name="na_bias",
    )(rpb.reshape(-1))


def _lat_neighbourhood_kernel(q_ref, k_ref, v_ref, ck_ref, cv_ref, bias_ref, o_ref):
    hp = pl.program_id(1)
    cache = [(ck_ref[0, :, lanes].astype(jnp.bfloat16), cv_ref[0, :, lanes].astype(jnp.bfloat16), None)
             for lanes in (slice(0, HEAD_DIM), slice(HEAD_DIM, 2 * HEAD_DIM))]
    for i in range(NA_Q_ROWS):
        r = pl.program_id(2) * NA_Q_ROWS + i
        first_row = jnp.clip(r - NA_WIN_ROWS // 2, 0, GRID_ROWS - NA_WIN_ROWS)
        local = pl.ds(pl.multiple_of(first_row * GRID_W, GRID_W), NA_WIN_ROWS * GRID_W)
        dr0 = first_row - r + NA_ROWS - 1
        queries = slice(i * GRID_W, (i + 1) * GRID_W)
        outs = []
        for j in range(2):
            lanes = slice(j * HEAD_DIM, (j + 1) * HEAD_DIM)
            bias = jnp.concatenate([bias_ref[hp * 2 + j, dr0 + a] for a in range(0, NA_WIN_ROWS, 2)], axis=1)
            parts = [(k_ref[local, :][:, lanes].astype(jnp.bfloat16),
                      v_ref[local, :][:, lanes].astype(jnp.bfloat16), bias), cache[j]]
            outs.append(_attend(_scaled_bf16(q_ref[queries, lanes]), parts))
        o_ref[queries, :] = jnp.concatenate(outs, axis=1)


def _lat_neighbourhood(p, cache_k, cache_v, bias_tiles):
    assert NA_WIN_ROWS % 2 == 0 and N_CTX_TOK % DEC_SEQ == 0 and GRID_ROWS % NA_Q_ROWS == 0
    q0, k0, v0 = COL0_QC // LANE, COL0_KC // LANE, COL0_VC // LANE
    q_blocks = GRID_ROWS // NA_Q_ROWS
    row_base = N_CTX_TOK // (NA_Q_ROWS * GRID_W)
    seq_base = N_CTX_TOK // DEC_SEQ
    whole = lambda c0: pl.BlockSpec((DEC_SEQ, LANE), lambda b, hp, r: (seq_base + b, c0 + hp))
    cache = pl.BlockSpec((1, PAST_LEN, LANE), lambda b, hp, r: (b, 0, hp))
    return pl.pallas_call(
        _lat_neighbourhood_kernel,
        grid=(DEC_BATCH, C_HEADS // 2, q_blocks),
        in_specs=[
            pl.BlockSpec((NA_Q_ROWS * GRID_W, LANE), lambda b, hp, r: (row_base + b * q_blocks + r, q0 + hp)),
            whole(k0), whole(v0), cache, cache,
            pl.BlockSpec(bias_tiles.shape, lambda b, hp, r: (0, 0, 0, 0)),
        ],
        out_specs=pl.BlockSpec((NA_Q_ROWS * GRID_W, LANE), lambda b, hp, r: (b * q_blocks + r, hp)),
        out_shape=jax.ShapeDtypeStruct((N_LAT_TOK, C_HEADS * HEAD_DIM), jnp.float32),
        compiler_params=pltpu.CompilerParams(
            dimension_semantics=("arbitrary", "arbitrary", "arbitrary"), vmem_limit_bytes=VMEM_LIMIT_BYTES),
        name="lat_neighbourhood",
    )(p, p, p, cache_k, cache_v, bias_tiles)


def _rope_tables():
    quarter = HEAD_DIM // 4
    t = jnp.arange(DEC_SEQ)
    pos = jnp.stack([t // GRID_W, t % GRID_W], axis=-1).astype(jnp.float32)
    freqs = ROPE_BASE ** (-jnp.arange(quarter, dtype=jnp.float32) / quarter)
    ang = pos[:, :, None] * freqs
    ang = jnp.concatenate([ang, ang], axis=-1).reshape(DEC_SEQ, HEAD_DIM)
    ang = jnp.concatenate([ang] * (LANE // HEAD_DIM), axis=-1)
    return jnp.cos(ang), jnp.sin(ang)


def _final_norm_kernel(x_ref, g_ref, o_ref):
    x = x_ref[...]
    o_ref[...] = x * lax.rsqrt(jnp.mean(x * x, axis=-1, keepdims=True) + RMS_EPS) * g_ref[...]


def _final_norm(x, g):
    n, d = x.shape
    return pl.pallas_call(
        _final_norm_kernel,
        grid=(n // ROW_TILE,),
        in_specs=[pl.BlockSpec((ROW_TILE, d), lambda i: (i, 0)), pl.BlockSpec((1, d), lambda i: (0, 0))],
        out_specs=pl.BlockSpec((ROW_TILE, d), lambda i: (i, 0)),
        out_shape=jax.ShapeDtypeStruct((n, d), jnp.float32),
        compiler_params=pltpu.CompilerParams(dimension_semantics=("arbitrary",)),
        name="final_norm",
    )(x, g)


def kernel(x_prompt, x_sample, cache_a_k, cache_a_v, cache_c_k, cache_c_v, state_rwkv, c, c_ctx, ln1_g, ln2_g, lnf_g, ada_w, ada_b, w_in, a_sink, a_out, rw_mu, rw_w0, rw_w2, rw_a0, rw_a2, rw_g2, rw_kk, rw_ka, rw_rk, rw_lnx_g, rw_lnx_b, rw_out, na_rpb, na_out, w_o, pe_q, pe_subkeys, pe_u, pe_v):
    bf16 = jnp.bfloat16
    x = jnp.concatenate([x_prompt.reshape(N_CTX_TOK, D_MODEL), x_sample.reshape(N_LAT_TOK, D_MODEL)], axis=0)
    cond = jnp.concatenate([c_ctx[None, :], c], axis=0)
    n_mix = A_COLS + B_COLS + C_COLS
    rope_cos, rope_sin = _rope_tables()
    list_ak, list_av, list_ck, list_cv, list_st = [], [], [], [], []
    for l in range(DEPTH):
        lp = {
            "rw_mu": rw_mu[l], "rw_w0": rw_w0[l], "rw_w2": rw_w2[l], "rw_a0": rw_a0[l],
            "rw_a2": rw_a2[l], "rw_g2": rw_g2[l], "rw_kk": rw_kk[l], "rw_ka": rw_ka[l],
            "rw_rk": rw_rk[l], "rw_lnx_g": rw_lnx_g[l], "rw_lnx_b": rw_lnx_b[l],
        }
        mod = _ada_mod(cond, ada_w[l], ada_b[l][None, :]).reshape(N_SEG, N_MOD, D_MODEL)

        w_in_l = jnp.concatenate([w_in[l][:, n_mix:], w_in[l][:, :n_mix]], axis=1).astype(bf16)
        p, _ = _norm_proj(x, ln1_g[l][None, :], mod, w_in_l, 0, 1, col_tile=IN_COLS // 3)
        sink_rows = jnp.broadcast_to(a_sink[l][:, None], (A_HEADS, LANE))
        flat_cache = lambda cache: cache[:, l].reshape(DEC_BATCH, PAST_LEN, -1)
        ya_ctx = _ctx_gqa(p, sink_rows)
        ya_lat = _lat_window(p, flat_cache(cache_a_k), flat_cache(cache_a_v), rope_cos, rope_sin, sink_rows)
        yc_ctx = _ctx_mha(p)
        yc_lat = _lat_neighbourhood(p, flat_cache(cache_c_k), flat_cache(cache_c_v), _na_bias(na_rpb[l]))
        rwkv_post_in, st = _rwkv_mixer(p, lp, state_rwkv[:, l])
        new_cache = lambda col0, heads: p[:N_CTX_TOK, col0:col0 + heads * HEAD_DIM].reshape(BATCH, SEQ, heads, HEAD_DIM)
        list_ak.append(new_cache(COL0_KA, A_KV))
        list_av.append(new_cache(COL0_VA, A_KV))
        list_ck.append(new_cache(COL0_KC, C_HEADS))
        list_cv.append(new_cache(COL0_VC, C_HEADS))
        list_st.append(st)
        x = _merge(x, ya_ctx, ya_lat, yc_ctx, yc_lat, rwkv_post_in, _head_ones(B_WIDTH), p, mod,
                   a_out[l].astype(bf16), rw_out[l].astype(bf16), na_out[l].astype(bf16), w_o[l].astype(bf16))

        q, h2 = _norm_proj(x, ln2_g[l][None, :], mod, pe_q[l].astype(bf16), 3, 4,
                           col_tile=PEER_HEADS * PEER_KEY_DIM)
        s1m, s2m, thr, c0 = _peer_route(q, pe_subkeys[l])
        x = _peer_dense(h2, x, mod, s1m, s2m, thr, c0, pe_u[l].astype(bf16), pe_v[l].T.astype(bf16))

    y = _final_norm(x, lnf_g[None, :])
    y_prompt = y[:N_CTX_TOK].reshape(BATCH, SEQ, D_MODEL)
    y_sample = y[N_CTX_TOK:].reshape(DEC_BATCH, DEC_SEQ, D_MODEL)
    return (y_prompt, y_sample, jnp.stack(list_ak, axis=1), jnp.stack(list_av, axis=1),
            jnp.stack(list_ck, axis=1), jnp.stack(list_cv, axis=1), jnp.stack(list_st, axis=1))
```
